```python
import math
import jax
import jax.numpy as jnp
from jax import lax
import numpy as np

D_MODEL = 1024
BATCH = 2
SEQ = 16384
DEPTH = 4

N_MIXERS = 4
RMS_EPS = 1e-6
LN_EPS = 1e-5
MASK_VALUE = -1e30
POOL_WINDOWS = (2, 4, 8, 16)
POOL_GROUPS = len(POOL_WINDOWS)
POOL_GDIM = D_MODEL // POOL_GROUPS
FOURIER_GROUPS = 4
C_GROUPS = ((128, 1), (512, 4), (2048, 16))
C_HEADS = 8
C_HEAD_DIM = 64
C_QKV_DIM = len(C_GROUPS) * 3 * C_HEADS * C_HEAD_DIM
C_OUT_DIM = C_HEADS * C_HEAD_DIM
CONV_WIDTH = 31
D_FF = 2816
N_EXPERTS = 8
TOP_K = 2
D_FF_EXPERT = 3584
MOE_BLOCK = 512
N_POOL_LAYERS = len(range(0, DEPTH, N_MIXERS))
N_FOURIER_LAYERS = len(range(1, DEPTH, N_MIXERS))
N_ATTN_LAYERS = len(range(2, DEPTH, N_MIXERS))
N_CONV_LAYERS = len(range(3, DEPTH, N_MIXERS))
N_DENSE_LAYERS = len(range(0, DEPTH, 2))
N_MOE_LAYERS = len(range(1, DEPTH, 2))

kernel_name = 'hybrid_interleaved_encoder'


def rms_norm(x, g):
    xf = x.astype(jnp.float32)
    y = xf * lax.rsqrt(jnp.mean(xf * xf, axis=-1, keepdims=True) + RMS_EPS)
    return y.astype(x.dtype) * g


def layer_norm(x, g, b):
    xf = x.astype(jnp.float32)
    mu = jnp.mean(xf, axis=-1, keepdims=True)
    var = jnp.mean(jnp.square(xf - mu), axis=-1, keepdims=True)
    return ((xf - mu) * lax.rsqrt(var + LN_EPS)).astype(x.dtype) * g + b


def pool_mixer(h, w_grp, scale):
    b, s, d = h.shape
    csum = jnp.pad(jnp.cumsum(h.astype(jnp.float32), axis=1), ((0, 0), (1, 0), (0, 0)))
    t = jnp.arange(s)
    outs = []
    for gi, w in enumerate(POOL_WINDOWS):
        lo = jnp.clip(t - w // 2, 0, s)
        hi = jnp.clip(t + w // 2, 0, s)
        sl = slice(gi * POOL_GDIM, (gi + 1) * POOL_GDIM)
        cnt = (hi - lo).astype(jnp.float32)[None, :, None]
        mean = (csum[:, hi, sl] - csum[:, lo, sl]) / cnt
        outs.append(mean - h[:, :, sl].astype(jnp.float32))
    pooled = jnp.stack(outs, axis=2).astype(h.dtype)
    y = jnp.einsum('bsgc,gce->bsge', pooled, w_grp).reshape(b, s, d)
    return y * scale


def fourier_mixer(h, w_out):
    b, s, d = h.shape
    hg = h.astype(jnp.float32).reshape(b, s, FOURIER_GROUPS, d // FOURIER_GROUPS)
    f = jnp.fft.fft2(hg, axes=(1, 3), norm='ortho').real
    return f.reshape(b, s, d).astype(h.dtype) @ w_out


def alibi_slopes():
    n = len(C_GROUPS) * C_HEADS
    return 2.0 ** (-8.0 * jnp.arange(1, n + 1, dtype=jnp.float32) / n)


def dilated_branch(q, k, v, window, dil, slopes):
    b, s, nh, hd = q.shape
    radius = window // (2 * dil)
    seg = s // dil
    bd = b * dil

    def strided(a):
        return a.reshape(b, seg, dil, nh, hd).transpose(0, 2, 1, 3, 4).reshape(bd, seg, nh, hd)

    blk = radius
    nb = -(-seg // blk)
    lp = nb * blk
    qs = jnp.pad(strided(q), ((0, 0), (0, lp - seg), (0, 0), (0, 0))).reshape(bd, nb, blk, nh, hd)
    kv_pad = ((0, 0), (blk, lp - seg + blk), (0, 0), (0, 0))
    ks = jnp.pad(strided(k), kv_pad).reshape(bd, nb + 2, blk, nh, hd)
    vs = jnp.pad(strided(v), kv_pad).reshape(bd, nb + 2, blk, nh, hd)
    kwin = jnp.concatenate([ks[:, :-2], ks[:, 1:-1], ks[:, 2:]], axis=2)
    vwin = jnp.concatenate([vs[:, :-2], vs[:, 1:-1], vs[:, 2:]], axis=2)

    scores = jnp.einsum('bnqhd,bnkhd->bnhqk', qs, kwin,
                        preferred_element_type=jnp.float32) / math.sqrt(hd)
    qi = jnp.arange(blk)[:, None]
    kj = jnp.arange(3 * blk)[None, :]
    rel = kj - blk - qi
    key_pos = (jnp.arange(nb) * blk)[:, None, None] + kj[None] - blk
    valid = (jnp.abs(rel) <= radius)[None] & (key_pos >= 0) & (key_pos < seg)
    dist = (dil * jnp.abs(rel)).astype(jnp.float32)
    scores = scores - slopes.astype(jnp.float32)[:, None, None] * dist
    scores = jnp.where(valid[None, :, None], scores, MASK_VALUE)
    m = jnp.max(scores, axis=-1, keepdims=True)
    p = jnp.exp(scores - m)
    den = jnp.sum(p, axis=-1)
    o = jnp.einsum('bnhqk,bnkhd->bnqhd', p, vwin.astype(jnp.float32))
    o = o / den.transpose(0, 1, 3, 2)[..., None]
    lse = (m[..., 0] + jnp.log(den)).transpose(0, 1, 3, 2)
    o = o.reshape(bd, lp, nh, hd)[:, :seg]
    o = o.reshape(b, dil, seg, nh, hd).transpose(0, 2, 1, 3, 4).reshape(b, s, nh, hd)
    lse = lse.reshape(bd, lp, nh)[:, :seg]
    lse = lse.reshape(b, dil, seg, nh).transpose(0, 2, 1, 3).reshape(b, s, nh)
    return o, lse


def dilated_attention(h, w_qkv, w_out):
    b, s, _ = h.shape
    ng = len(C_GROUPS)
    qkv = (h @ w_qkv).reshape(b, s, ng, 3, C_HEADS, C_HEAD_DIM)
    slopes = alibi_slopes().reshape(ng, C_HEADS)
    outs, lses = [], []
    for gi, (window, dil) in enumerate(C_GROUPS):
        o, lse = dilated_branch(qkv[:, :, gi, 0], qkv[:, :, gi, 1], qkv[:, :, gi, 2],
                                window, dil, slopes[gi])
        outs.append(o)
        lses.append(lse)
    wts = jax.nn.softmax(jnp.stack(lses, axis=0), axis=0)
    o = jnp.sum(wts[..., None] * jnp.stack(outs, axis=0), axis=0)
    return o.reshape(b, s, C_OUT_DIM).astype(h.dtype) @ w_out


def conv_mixer(h, w_pw1, w_dw, b_dw, ln_g, ln_b, w_pw2):
    d = h.shape[-1]
    u = h @ w_pw1
    u = u[..., :d] * jax.nn.sigmoid(u[..., d:])
    u = lax.conv_general_dilated(u, w_dw[:, None, :], window_strides=(1,),
                                 padding=((CONV_WIDTH // 2, CONV_WIDTH // 2),),
                                 dimension_numbers=('NWC', 'WIO', 'NWC'),
                                 feature_group_count=d) + b_dw
    u = jax.nn.silu(layer_norm(u, ln_g, ln_b))
    return u @ w_pw2


def swiglu(h, w1, w3, w2):
    return (jax.nn.silu(h @ w1) * (h @ w3)) @ w2


def moe_swiglu(h, w_router, w1, w3, w2):
    n, d = h.shape
    logits = jnp.dot(h.astype(jnp.float32), w_router.astype(jnp.float32))
    top_val, top_idx = lax.top_k(logits, TOP_K)
    gates = jax.nn.softmax(top_val, axis=-1).astype(h.dtype)
    e_flat = top_idx.reshape(-1).astype(jnp.int32)
    tok_flat = jnp.repeat(jnp.arange(n, dtype=jnp.int32), TOP_K)
    g_flat = gates.reshape(-1)
    a = e_flat.shape[0]
    order = jnp.argsort(e_flat)
    e_s, tok_s, g_s = e_flat[order], tok_flat[order], g_flat[order]
    counts = jnp.bincount(e_flat, length=N_EXPERTS).astype(jnp.int32)
    start = jnp.cumsum(counts) - counts
    pcounts = (counts + MOE_BLOCK - 1) // MOE_BLOCK * MOE_BLOCK
    pend = jnp.cumsum(pcounts)
    pstart = pend - pcounts
    dest = pstart[e_s] + jnp.arange(a, dtype=jnp.int32) - start[e_s]
    n_blocks = -(-a // MOE_BLOCK) + N_EXPERTS
    p = n_blocks * MOE_BLOCK
    row_tok = jnp.zeros((p,), jnp.int32).at[dest].set(tok_s)
    row_gate = jnp.zeros((p,), h.dtype).at[dest].set(g_s)
    block_exp = jnp.searchsorted(pend, jnp.arange(n_blocks, dtype=jnp.int32) * MOE_BLOCK, side='right')
    block_exp = jnp.minimum(block_exp, N_EXPERTS - 1).astype(jnp.int32)
    xb = h[row_tok].reshape(n_blocks, MOE_BLOCK, d)

    def expert_block(args):
        xblk, e = args
        return (jax.nn.silu(xblk @ w1[e]) * (xblk @ w3[e])) @ w2[e]

    yb = lax.map(expert_block, (xb, block_exp))
    y = yb.reshape(p, d) * row_gate[:, None]
    return jnp.zeros_like(h).at[row_tok].add(y)


def setup_inputs(seed: int = 0) -> dict:
    key = jax.random.key(seed)
    ks = jax.random.split(key, 24)
    f32 = jnp.float32

    def nrm(k, shape, fan_in):
        return jax.random.normal(k, shape, f32) * fan_in ** -0.5

    def gain(k, shape):
        return 1.0 + 0.05 * jax.random.normal(k, shape, f32)

    def small(k, shape):
        return 0.02 * jax.random.normal(k, shape, f32)

    return {
        'x': jax.random.normal(ks[0], (BATCH, SEQ, D_MODEL), f32),
        'g_mix': gain(ks[1], (DEPTH, D_MODEL)),
        'g_ffn': gain(ks[2], (DEPTH, D_MODEL)),
        'g_final': gain(ks[3], (D_MODEL,)),
        'a_w_grp': nrm(ks[4], (N_POOL_LAYERS, POOL_GROUPS, POOL_GDIM, POOL_GDIM), POOL_GDIM),
        'a_scale': gain(ks[5], (N_POOL_LAYERS, D_MODEL)),
        'b_w_out': nrm(ks[6], (N_FOURIER_LAYERS, D_MODEL, D_MODEL), D_MODEL),
        'c_w_qkv': nrm(ks[7], (N_ATTN_LAYERS, D_MODEL, C_QKV_DIM), D_MODEL),
        'c_w_out': nrm(ks[8], (N_ATTN_LAYERS, C_OUT_DIM, D_MODEL), C_OUT_DIM),
        'd_w_pw1': nrm(ks[9], (N_CONV_LAYERS, D_MODEL, 2 * D_MODEL), D_MODEL),
        'd_w_dw': nrm(ks[10], (N_CONV_LAYERS, CONV_WIDTH, D_MODEL), CONV_WIDTH),
        'd_b_dw': small(ks[11], (N_CONV_LAYERS, D_MODEL)),
        'd_ln_g': gain(ks[12], (N_CONV_LAYERS, D_MODEL)),
        'd_ln_b': small(ks[13], (N_CONV_LAYERS, D_MODEL)),
        'd_w_pw2': nrm(ks[14], (N_CONV_LAYERS, D_MODEL, D_MODEL), D_MODEL),
        'ffn_w1': nrm(ks[15], (N_DENSE_LAYERS, D_MODEL, D_FF), D_MODEL),
        'ffn_w3': nrm(ks[16], (N_DENSE_LAYERS, D_MODEL, D_FF), D_MODEL),
        'ffn_w2': nrm(ks[17], (N_DENSE_LAYERS, D_FF, D_MODEL), D_FF),
        'moe_router': nrm(ks[18], (N_MOE_LAYERS, D_MODEL, N_EXPERTS), D_MODEL),
        'moe_w1': nrm(ks[19], (N_MOE_LAYERS, N_EXPERTS, D_MODEL, D_FF_EXPERT), D_MODEL),
        'moe_w3': nrm(ks[20], (N_MOE_LAYERS, N_EXPERTS, D_MODEL, D_FF_EXPERT), D_MODEL),
        'moe_w2': nrm(ks[21], (N_MOE_LAYERS, N_EXPERTS, D_FF_EXPERT, D_MODEL), D_FF_EXPERT),
    }


def reference(x, g_mix, g_ffn, g_final, a_w_grp, a_scale, b_w_out, c_w_qkv, c_w_out,
              d_w_pw1, d_w_dw, d_b_dw, d_ln_g, d_ln_b, d_w_pw2,
              ffn_w1, ffn_w3, ffn_w2, moe_router, moe_w1, moe_w3, moe_w2):
    b, s, d = x.shape
    for i in range(DEPTH):
        kind, j = i % N_MIXERS, i // N_MIXERS
        h = rms_norm(x, g_mix[i])
        if kind == 0:
            mix = pool_mixer(h, a_w_grp[j], a_scale[j])
        elif kind == 1:
            mix = fourier_mixer(h, b_w_out[j])
        elif kind == 2:
            mix = dilated_attention(h, c_w_qkv[j], c_w_out[j])
        else:
            mix = conv_mixer(h, d_w_pw1[j], d_w_dw[j], d_b_dw[j], d_ln_g[j], d_ln_b[j], d_w_pw2[j])
        x = x + mix
        h = rms_norm(x, g_ffn[i])
        if i % 2 == 0:
            k = i // 2
            x = x + swiglu(h, ffn_w1[k], ffn_w3[k], ffn_w2[k])
        else:
            k = i // 2
            y = moe_swiglu(h.reshape(b * s, d), moe_router[k], moe_w1[k], moe_w3[k], moe_w2[k])
            x = x + y.reshape(b, s, d)
    return rms_norm(x, g_final)
```

```python
import functools
import math

import numpy as np
import jax
import jax.numpy as jnp
from jax import lax
from jax.experimental import pallas as pl
from jax.experimental.pallas import tpu as pltpu

F32 = jnp.float32
BF16 = jnp.bfloat16

RMS_EPS = 1e-6
LN_EPS = 1e-5
MASK_VALUE = -1e30
POOL_WINDOWS = (2, 4, 8, 16)
FOURIER_GROUPS = 4
ATTN_GROUPS = ((128, 1), (512, 4), (2048, 16))
ATTN_HEADS = 8
ATTN_HEAD_DIM = 64
ATTN_WIDTH = ATTN_HEADS * ATTN_HEAD_DIM
CONV_WIDTH = 31
N_EXPERTS = 8
TOP_K = 2

HALO = 16
FFT_N2 = 128
MOE_ROWS = 512
MOE_CHUNK = 512
VMEM_LIMIT_BYTES = 56 * 1024 * 1024


def _cparams(*sem):
    return pltpu.CompilerParams(dimension_semantics=sem, vmem_limit_bytes=VMEM_LIMIT_BYTES)


def _dot(a, b):
    return jnp.dot(a, b, preferred_element_type=F32)


def _rms(x, g):
    ms = jnp.mean(x * x, axis=-1, keepdims=True)
    return x * lax.rsqrt(ms + RMS_EPS) * g


def _swiglu(hb, w1_ref, w3_ref, w2_ref, fc):
    f = w1_ref.shape[-1]
    acc = None
    for c in range(0, f, fc):
        a = _dot(hb, w1_ref[:, c:c + fc])
        b = _dot(hb, w3_ref[:, c:c + fc])
        g = (a * jax.nn.sigmoid(a) * b).astype(BF16)
        y = _dot(g, w2_ref[c:c + fc, :])
        acc = y if acc is None else acc + y
    return acc


def _route(h32, wrt_ref, idx_ref, gate_ref):
    logits = lax.dot_general(wrt_ref[...], h32, (((1,), (1,)), ((), ())),
                             precision=lax.Precision.HIGHEST, preferred_element_type=F32)
    e_iota = lax.broadcasted_iota(jnp.int32, logits.shape, 0)
    m1 = jnp.max(logits, axis=0, keepdims=True)
    i1 = jnp.min(jnp.where(logits == m1, e_iota, N_EXPERTS), axis=0, keepdims=True)
    rest = jnp.where(e_iota == i1, -jnp.inf, logits)
    m2 = jnp.max(rest, axis=0, keepdims=True)
    i2 = jnp.min(jnp.where(rest == m2, e_iota, N_EXPERTS), axis=0, keepdims=True)
    e2 = jnp.exp(m2 - m1)
    den = 1.0 + e2
    idx_ref[0:1, :] = i1
    idx_ref[1:2, :] = i2
    gate_ref[0:1, :] = 1.0 / den
    gate_ref[1:2, :] = e2 / den


def _moe_prologue(x1, gf_ref, wrt_ref, x_out_ref, h_ref, idx_ref, gate_ref):
    x_out_ref[0] = x1
    h2 = _rms(x1, gf_ref[...])
    h_ref[0] = h2.astype(BF16)
    _route(h2, wrt_ref, idx_ref, gate_ref)


def _pool_ffn_kernel(xp_ref, xc_ref, xn_ref, gm_ref, wg_ref, sc_ref, gf_ref, w1_ref, w3_ref, w2_ref,
                     o_ref, *, ts, seq, fc):
    i = pl.program_id(1)
    xc = xc_ref[0]
    d = xc.shape[1]
    gd = d // len(POOL_WINDOWS)
    n_ext = ts + 2 * HALO
    xe = jnp.concatenate([xp_ref[0], xc, xn_ref[0]], axis=0)
    he = _rms(xe, gm_ref[...])
    pos = i * ts - HALO + lax.broadcasted_iota(jnp.int32, (n_ext, 1), 0)
    he = jnp.where((pos >= 0) & (pos < seq), he, 0.0)

    s = he + pltpu.roll(he, 1, 0)
    sums = [s]
    half = 1
    for _ in POOL_WINDOWS[1:]:
        s = s[:, gd:]
        s = pltpu.roll(s, half, 0) + pltpu.roll(s, n_ext - half, 0)
        sums.append(s)
        half *= 2

    tpos = i * ts + lax.broadcasted_iota(jnp.int32, (ts, 1), 0)
    ys = []
    for gi, w in enumerate(POOL_WINDOWS):
        cnt = jnp.minimum(tpos + w // 2, seq) - jnp.maximum(tpos - w // 2, 0)
        mean = sums[gi][HALO:HALO + ts, :gd] / cnt.astype(F32)
        pooled = mean - he[HALO:HALO + ts, gi * gd:(gi + 1) * gd]
        ys.append(_dot(pooled.astype(BF16), wg_ref[gi]))
    x1 = xc + jnp.concatenate(ys, axis=1) * sc_ref[...]
    hb = _rms(x1, gf_ref[...]).astype(BF16)
    o_ref[0] = x1 + _swiglu(hb, w1_ref, w3_ref, w2_ref, fc)


def _halo_specs(ts, seq, d):
    r = ts // HALO
    last = seq // HALO - 1
    return [
        pl.BlockSpec((1, HALO, d), lambda b, i: (b, jnp.maximum(i * r - 1, 0), 0)),
        pl.BlockSpec((1, ts, d), lambda b, i: (b, i, 0)),
        pl.BlockSpec((1, HALO, d), lambda b, i: (b, jnp.minimum((i + 1) * r, last), 0)),
    ]


def _full(shape):
    n = len(shape)
    return pl.BlockSpec(shape, lambda *_: (0,) * n)


def _pool_ffn_layer(x, g_mix, w_grp, scale, g_ffn, w1, w3, w2, ts):
    bsz, seq, d = x.shape
    f = w1.shape[1]
    kern = functools.partial(_pool_ffn_kernel, ts=ts, seq=seq, fc=256)
    return pl.pallas_call(
        kern,
        grid=(bsz, seq // ts),
        in_specs=_halo_specs(ts, seq, d) + [
            _full((1, d)), _full(w_grp.shape), _full((1, d)), _full((1, d)),
            _full((d, f)), _full((d, f)), _full((f, d))],
        out_specs=pl.BlockSpec((1, ts, d), lambda b, i: (b, i, 0)),
        out_shape=jax.ShapeDtypeStruct(x.shape, F32),
        compiler_params=_cparams("parallel", "parallel"),
        name="pool_ffn",
    )(x, x, x, g_mix.reshape(1, d), w_grp.astype(BF16), scale.reshape(1, d), g_ffn.reshape(1, d),
      w1.astype(BF16), w3.astype(BF16), w2.astype(BF16))


def _fft_stage1_kernel(x_ref, g_ref, f1_ref, o_ref, *, d, nseg):
    for j in range(nseg):
        h = _rms(x_ref[0, :, j * d:(j + 1) * d], g_ref[...]).astype(BF16)
        o_ref[0, :, j * d:(j + 1) * d] = _dot(f1_ref[...], h).astype(BF16)


def _fft_stage2_kernel(br_ref, bi_ref, m_ref, o_ref, *, d, nk):
    n2 = FFT_N2
    for j in range(nk):
        rhs = jnp.concatenate([br_ref[0, j], bi_ref[0, j]], axis=0)
        res = _dot(m_ref[j], rhs)
        o_ref[0, 0, :, j * d:(j + 1) * d] = res[:n2].astype(BF16)
        o_ref[0, 1, :, j * d:(j + 1) * d] = res[n2:].astype(BF16)


def _fourier_out_kernel(x_ref, ar_ref, ai_ref, cc_ref, sc_ref, wo_ref, gf_ref, wrt_ref,
                        xo_ref, h_ref, idx_ref, gate_ref, *, norm):
    d = x_ref.shape[2]
    gd = d // FOURIER_GROUPS
    fs = []
    for gi in range(FOURIER_GROUPS):
        sl = slice(gi * gd, (gi + 1) * gd)
        fs.append(_dot(ar_ref[0, 0, :, sl], cc_ref[...]) + _dot(ai_ref[0, 0, :, sl], sc_ref[...]))
    f = (jnp.concatenate(fs, axis=1) * norm).astype(BF16)
    x1 = x_ref[0] + _dot(f, wo_ref[...])
    _moe_prologue(x1, gf_ref, wrt_ref, xo_ref, h_ref, idx_ref, gate_ref)


def _dft_tables(seq, gd):
    n2 = FFT_N2
    n1 = seq // n2
    k = np.arange(n1)
    ang1 = 2.0 * np.pi * ((k[:, None] * k[None, :]) % n1) / n1
    f1 = np.concatenate([np.cos(ang1), -np.sin(ang1)], axis=0)
    c = np.arange(gd)
    angc = 2.0 * np.pi * ((c[:, None] * c[None, :]) % gd) / gd
    return (jnp.asarray(f1, BF16), jnp.asarray(np.cos(angc), BF16), jnp.asarray(np.sin(angc), BF16))


def _twiddled_dft(seq):
    n2 = FFT_N2
    n1 = seq // n2
    k1 = lax.broadcasted_iota(jnp.int32, (n1, n2, n2), 0)
    k2 = lax.broadcasted_iota(jnp.int32, (n1, n2, n2), 1)
    a = lax.broadcasted_iota(jnp.int32, (n1, n2, n2), 2)
    ang = ((a * (n1 * k2 + k1)) % seq).astype(F32) * (2.0 * math.pi / seq)
    mr = jnp.cos(ang)
    mi = -jnp.sin(ang)
    top = jnp.concatenate([mr, -mi], axis=2)
    bot = jnp.concatenate([mi, mr], axis=2)
    return jnp.concatenate([top, bot], axis=1).astype(BF16)


def _fourier_layer(x, g_mix, w_out, g_ffn, w_router, tm):
    bsz, seq, d = x.shape
    n2 = FFT_N2
    n1 = seq // n2
    gd = d // FOURIER_GROUPS
    f1, cc, sc = _dft_tables(seq, gd)
    mtab = _twiddled_dft(seq)
    nseg = 8
    b1 = pl.pallas_call(
        functools.partial(_fft_stage1_kernel, d=d, nseg=nseg),
        grid=(bsz, n2 // nseg),
        in_specs=[pl.BlockSpec((1, n1, nseg * d), lambda b, t: (b, 0, t)), _full((1, d)), _full(f1.shape)],
        out_specs=pl.BlockSpec((1, 2 * n1, nseg * d), lambda b, t: (b, 0, t)),
        out_shape=jax.ShapeDtypeStruct((bsz, 2 * n1, n2 * d), BF16),
        compiler_params=_cparams("parallel", "parallel"),
        name="fft_stage1",
    )(x.reshape(bsz, n1, n2 * d), g_mix.reshape(1, d), f1)

    nk = 8
    b1v = b1.reshape(bsz, 2 * n1, n2, d)
    a = pl.pallas_call(
        functools.partial(_fft_stage2_kernel, d=d, nk=nk),
        grid=(bsz, n1 // nk),
        in_specs=[pl.BlockSpec((1, nk, n2, d), lambda b, t: (b, t, 0, 0)),
                  pl.BlockSpec((1, nk, n2, d), lambda b, t: (b, n1 // nk + t, 0, 0)),
                  pl.BlockSpec((nk, 2 * n2, 2 * n2), lambda b, t: (t, 0, 0))],
        out_specs=pl.BlockSpec((1, 2, n2, nk * d), lambda b, t: (b, 0, 0, t)),
        out_shape=jax.ShapeDtypeStruct((bsz, 2, n2, n1 * d), BF16),
        compiler_params=_cparams("parallel", "parallel"),
        name="fft_stage2",
    )(b1v, b1v, mtab)

    av = a.reshape(bsz, 2, seq, d)
    n_tok = bsz * seq
    nt = seq // tm
    outs = pl.pallas_call(
        functools.partial(_fourier_out_kernel, norm=1.0 / math.sqrt(seq * gd)),
        grid=(bsz, nt),
        in_specs=[pl.BlockSpec((1, tm, d), lambda b, i: (b, i, 0)),
                  pl.BlockSpec((1, 1, tm, d), lambda b, i: (b, 0, i, 0)),
                  pl.BlockSpec((1, 1, tm, d), lambda b, i: (b, 1, i, 0)),
                  _full((gd, gd)), _full((gd, gd)), _full((d, d)), _full((1, d)), _full((N_EXPERTS, d))],
        out_specs=[pl.BlockSpec((1, tm, d), lambda b, i: (b, i, 0)),
                   pl.BlockSpec((1, tm, d), lambda b, i: (b, i, 0)),
                   pl.BlockSpec((TOP_K, tm), lambda b, i: (0, b * nt + i)),
                   pl.BlockSpec((TOP_K, tm), lambda b, i: (0, b * nt + i))],
        out_shape=[jax.ShapeDtypeStruct(x.shape, F32), jax.ShapeDtypeStruct(x.shape, BF16),
                   jax.ShapeDtypeStruct((TOP_K, n_tok), jnp.int32),
                   jax.ShapeDtypeStruct((TOP_K, n_tok), F32)],
        compiler_params=_cparams("parallel", "parallel"),
        name="fourier_out",
    )(x, av, av, cc, sc, w_out.astype(BF16), g_ffn.reshape(1, d), w_router.T)
    return outs


def _moe_plan(idx, gates, n_tok):
    rows, chunk = MOE_ROWS, MOE_CHUNK
    n_assign = n_tok * TOP_K
    n_blocks = -(-n_assign // rows) + N_EXPERTS
    n_rows = n_blocks * rows
    n_chunks = n_tok // chunk
    max_pairs = n_blocks + N_EXPERTS * (n_chunks - 1)

    e_flat = idx.T.reshape(-1)
    g_flat = gates.T.reshape(-1)
    onehot = (e_flat[:, None] == jnp.arange(N_EXPERTS, dtype=jnp.int32)[None, :]).astype(jnp.int32)
    csum = jnp.cumsum(onehot, axis=0)
    rank = jnp.take_along_axis(csum, e_flat[:, None], axis=1)[:, 0] - 1
    counts = csum[-1]
    pcounts = (counts + rows - 1) // rows * rows
    pend = jnp.cumsum(pcounts)
    pstart = pend - pcounts
    dest = pstart[e_flat] + rank
    tok = jnp.arange(n_assign, dtype=jnp.int32) // TOP_K
    row_tok = jnp.full((n_rows,), -1, jnp.int32).at[dest].set(tok)
    row_gate = jnp.zeros((n_rows,), F32).at[dest].set(g_flat)

    n_used = pend[-1] // rows
    blk = jnp.arange(n_blocks, dtype=jnp.int32)
    block_exp = jnp.minimum(jnp.searchsorted(pend, blk * rows, side='right'), N_EXPERTS - 1).astype(jnp.int32)
    rt = row_tok.reshape(n_blocks, rows)
    used = blk < n_used
    c_lo = jnp.where(used, rt[:, 0] // chunk, 0)
    c_hi = jnp.where(used, jnp.max(rt, axis=1) // chunk, 0)
    npairs = jnp.where(used, c_hi - c_lo + 1, 0)
    cum = jnp.cumsum(npairs)
    total = cum[-1]
    first_pair = cum - npairs

    p = jnp.arange(max_pairs, dtype=jnp.int32)
    valid = p < total
    pb = jnp.minimum(jnp.searchsorted(cum, p, side='right'), n_used - 1).astype(jnp.int32)
    pc = jnp.where(valid, c_lo[pb] + p - first_pair[pb], c_hi[n_used - 1]).astype(jnp.int32)
    p_first = valid & (p == first_pair[pb])
    p_last = valid & (p == cum[pb] - 1)
    pflags = (valid.astype(jnp.int32) + 2 * p_first.astype(jnp.int32) + 4 * p_last.astype(jnp.int32))

    order = jnp.argsort(jnp.where(valid, pc, n_chunks), stable=True)
    cb_sorted = pb[order]
    cc_sorted = pc[order]
    last_valid = total - 1
    cb = jnp.where(valid, cb_sorted, cb_sorted[last_valid]).astype(jnp.int32)
    cc = jnp.where(valid, cc_sorted, n_chunks - 1).astype(jnp.int32)
    prev_cc = jnp.concatenate([jnp.full((1,), -1, jnp.int32), cc[:-1]])
    next_cc = jnp.concatenate([cc[1:], jnp.full((1,), -1, jnp.int32)])
    c_first = valid & (cc != prev_cc)
    c_last = valid & ((cc != next_cc) | (p == last_valid))
    cflags = (valid.astype(jnp.int32) + 2 * c_first.astype(jnp.int32) + 4 * c_last.astype(jnp.int32))

    return dict(n_blocks=n_blocks, max_pairs=max_pairs, row_tok=row_tok, row_gate=row_gate,
                n_used=n_used.reshape(1).astype(jnp.int32), block_exp=block_exp,
                pb=pb, pc=pc, pflags=pflags, cb=cb, cc=cc, cflags=cflags)


def _dispatch_kernel(pb_ref, pc_ref, pf_ref, h_ref, tok_ref, xb_ref, acc_ref):
    p = pl.program_id(0)
    flags = pf_ref[p]
    rows, chunk = acc_ref.shape[0], h_ref.shape[0]

    @pl.when((flags & 2) != 0)
    def _():
        acc_ref[...] = jnp.zeros_like(acc_ref)

    @pl.when((flags & 1) != 0)
    def _():
        rel = tok_ref[...] - pc_ref[p] * chunk
        sel = rel == lax.broadcasted_iota(jnp.int32, (rows, chunk), 1)
        acc_ref[...] += _dot(jnp.where(sel, 1.0, 0.0).astype(BF16), h_ref[...])

    @pl.when((flags & 4) != 0)
    def _():
        xb_ref[...] = acc_ref[...].astype(BF16)


def _expert_kernel(be_ref, nu_ref, xb_ref, gate_ref, w1_ref, w3_ref, w2_ref, y_ref, acc_ref, *, nf, fc):
    b = pl.program_id(0)
    f = pl.program_id(1)

    @pl.when(b < nu_ref[0])
    def _():
        y = _swiglu(xb_ref[...], w1_ref.at[0], w3_ref.at[0], w2_ref.at[0], fc)

        @pl.when(f == 0)
        def _():
            acc_ref[...] = y

        @pl.when(f > 0)
        def _():
            acc_ref[...] += y

        @pl.when(f == nf - 1)
        def _():
            y_ref[...] = (acc_ref[...] * gate_ref[...]).astype(BF16)


def _combine_kernel(cb_ref, cc_ref, cf_ref, y_ref, tok_ref, x_ref, g_ref, o_ref, acc_ref, *, final_norm):
    p = pl.program_id(0)
    flags = cf_ref[p]
    chunk, rows = acc_ref.shape[0], y_ref.shape[0]

    @pl.when((flags & 2) != 0)
    def _():
        acc_ref[...] = jnp.zeros_like(acc_ref)

    @pl.when((flags & 1) != 0)
    def _():
        rel = tok_ref[0] - cc_ref[p] * chunk
        sel = rel == lax.broadcasted_iota(jnp.int32, (chunk, rows), 0)
        acc_ref[...] += _dot(jnp.where(sel, 1.0, 0.0).astype(BF16), y_ref[...])

    @pl.when((flags & 4) != 0)
    def _():
        xo = x_ref[...] + acc_ref[...]
        o_ref[...] = _rms(xo, g_ref[...]) if final_norm else xo


def _moe_layer(x, h, idx, gates, w1, w3, w2, g_final=None):
    bsz, seq, d = x.shape
    n_tok = bsz * seq
    f = w1.shape[2]
    rows, chunk = MOE_ROWS, MOE_CHUNK
    plan = _moe_plan(idx, gates, n_tok)
    n_blocks, max_pairs = plan['n_blocks'], plan['max_pairs']
    n_rows = n_blocks * rows

    xb = pl.pallas_call(
        _dispatch_kernel,
        grid_spec=pltpu.PrefetchScalarGridSpec(
            num_scalar_prefetch=3, grid=(max_pairs,),
            in_specs=[pl.BlockSpec((chunk, d), lambda p, pb, pc, pf: (pc[p], 0)),
                      pl.BlockSpec((rows, 1), lambda p, pb, pc, pf: (pb[p], 0))],
            out_specs=pl.BlockSpec((rows, d), lambda p, pb, pc, pf: (pb[p], 0)),
            scratch_shapes=[pltpu.VMEM((rows, d), F32)]),
        out_shape=jax.ShapeDtypeStruct((n_rows, d), BF16),
        compiler_params=_cparams("arbitrary"),
        name="moe_dispatch",
    )(plan['pb'], plan['pc'], plan['pflags'], h.reshape(n_tok, d), plan['row_tok'].reshape(n_rows, 1))

    nf = 2
    fblk = f // nf

    def blk_idx(b, nu):
        return jnp.minimum(b, nu[0] - 1)

    def f_idx(b, fi, nu):
        return jnp.where(b < nu[0], fi, nf - 1)

    y = pl.pallas_call(
        functools.partial(_expert_kernel, nf=nf, fc=256),
        grid_spec=pltpu.PrefetchScalarGridSpec(
            num_scalar_prefetch=2, grid=(n_blocks, nf),
            in_specs=[pl.BlockSpec((rows, d), lambda b, fi, be, nu: (blk_idx(b, nu), 0)),
                      pl.BlockSpec((rows, 1), lambda b, fi, be, nu: (blk_idx(b, nu), 0)),
                      pl.BlockSpec((1, d, fblk), lambda b, fi, be, nu: (be[blk_idx(b, nu)], 0, f_idx(b, fi, nu))),
                      pl.BlockSpec((1, d, fblk), lambda b, fi, be, nu: (be[blk_idx(b, nu)], 0, f_idx(b, fi, nu))),
                      pl.BlockSpec((1, fblk, d), lambda b, fi, be, nu: (be[blk_idx(b, nu)], f_idx(b, fi, nu), 0))],
            out_specs=pl.BlockSpec((rows, d), lambda b, fi, be, nu: (blk_idx(b, nu), 0)),
            scratch_shapes=[pltpu.VMEM((rows, d), F32)]),
        out_shape=jax.ShapeDtypeStruct((n_rows, d), BF16),
        compiler_params=_cparams("arbitrary", "arbitrary"),
        name="moe_experts",
    )(plan['block_exp'], plan['n_used'], xb, plan['row_gate'].reshape(n_rows, 1),
      w1.astype(BF16), w3.astype(BF16), w2.astype(BF16))

    final_norm = g_final is not None
    g = (g_final if final_norm else jnp.ones((d,), F32)).reshape(1, d)
    out = pl.pallas_call(
        functools.partial(_combine_kernel, final_norm=final_norm),
        grid_spec=pltpu.PrefetchScalarGridSpec(
            num_scalar_prefetch=3, grid=(max_pairs,),
            in_specs=[pl.BlockSpec((rows, d), lambda p, cb, cc, cf: (cb[p], 0)),
                      pl.BlockSpec((1, 1, rows), lambda p, cb, cc, cf: (cb[p], 0, 0)),
                      pl.BlockSpec((chunk, d), lambda p, cb, cc, cf: (cc[p], 0)),
                      pl.BlockSpec((1, d), lambda p, cb, cc, cf: (0, 0))],
            out_specs=pl.BlockSpec((chunk, d), lambda p, cb, cc, cf: (cc[p], 0)),
            scratch_shapes=[pltpu.VMEM((chunk, d), F32)]),
        out_shape=jax.ShapeDtypeStruct((n_tok, d), F32),
        compiler_params=_cparams("arbitrary"),
        name="moe_combine",
    )(plan['cb'], plan['cc'], plan['cflags'], y, plan['row_tok'].reshape(n_blocks, 1, rows),
      x.reshape(n_tok, d), g)
    return out.reshape(bsz, seq, d)


def _qkv_kernel(x_ref, g_ref, w_ref, o_ref, *, nc):
    hb = _rms(x_ref[0], g_ref[...]).astype(BF16)
    for c in range(0, w_ref.shape[1], nc):
        o_ref[0, :, c:c + nc] = _dot(hb, w_ref[:, c:c + nc]).astype(BF16)


def _attn_kernel(q_ref, kp_ref, kc_ref, kn_ref, vp_ref, vc_ref, vn_ref, o_ref, lse_ref,
                 *, tq, seg, dil, radius, slopes):
    j = pl.program_id(2)
    hd = ATTN_HEAD_DIM
    tk = tq + 2 * radius
    q = q_ref[0] * jnp.asarray(1.0 / math.sqrt(hd), BF16)
    k = jnp.concatenate([kp_ref[0], kc_ref[0], kn_ref[0]], axis=0)
    v = jnp.concatenate([vp_ref[0], vc_ref[0], vn_ref[0]], axis=0)
    qi = lax.broadcasted_iota(jnp.int32, (tq, tk), 0)
    kj = lax.broadcasted_iota(jnp.int32, (tq, tk), 1)
    arel = jnp.abs(kj - radius - qi)
    kpos = j * tq - radius + kj
    valid = (arel <= radius) & (kpos >= 0) & (kpos < seg)
    dist = (dil * arel).astype(F32)
    low = lax.broadcasted_iota(jnp.int32, (1, 2 * hd), 1) < hd
    for hp in range(ATTN_HEADS // 2):
        sl = slice(hp * 2 * hd, (hp + 1) * 2 * hd)
        qp, kp, vp = q[:, sl], k[:, sl], v[:, sl]
        o_pair, l_pair = None, None
        for sub in range(2):
            mine = low if sub == 0 else jnp.logical_not(low)
            qm = jnp.where(mine, qp, jnp.zeros_like(qp))
            s = lax.dot_general(qm, kp, (((1,), (1,)), ((), ())), preferred_element_type=F32)
            s = jnp.where(valid, s - slopes[2 * hp + sub] * dist, MASK_VALUE)
            m = jnp.max(s, axis=-1, keepdims=True)
            p = jnp.exp(s - m)
            den = jnp.sum(p, axis=-1, keepdims=True)
            o = _dot(p.astype(BF16), vp) / den
            l = jnp.broadcast_to(m + jnp.log(den), o.shape)
            o_pair = o if sub == 0 else jnp.where(low, o_pair, o)
            l_pair = l if sub == 0 else jnp.where(low, l_pair, l)
        o_ref[0, :, sl] = o_pair.astype(BF16)
        lse_ref[0, :, sl] = l_pair


def _attn_out_ffn_kernel(x_ref, o0_ref, o1_ref, o2_ref, l0_ref, l1_ref, l2_ref, wo_ref, gf_ref,
                         w1_ref, w3_ref, w2_ref, out_ref, *, fc):
    ls = [l0_ref[0], l1_ref[0], l2_ref[0]]
    os_ = [o0_ref[0], o1_ref[0], o2_ref[0]]
    m = jnp.maximum(jnp.maximum(ls[0], ls[1]), ls[2])
    es = [jnp.exp(l - m) for l in ls]
    den = es[0] + es[1] + es[2]
    o = (es[0] * os_[0].astype(F32) + es[1] * os_[1].astype(F32) + es[2] * os_[2].astype(F32)) / den
    x1 = x_ref[0] + _dot(o.astype(BF16), wo_ref[...])
    hb = _rms(x1, gf_ref[...]).astype(BF16)
    out_ref[0] = x1 + _swiglu(hb, w1_ref, w3_ref, w2_ref, fc)


def _alibi_slopes():
    n = len(ATTN_GROUPS) * ATTN_HEADS
    s = np.float32(2.0) ** (np.float32(-8.0) * np.arange(1, n + 1, dtype=np.float32) / np.float32(n))
    return [float(v) for v in s]


def _attention_ffn_layer(x, g_mix, w_qkv, w_out, g_ffn, w1, w3, w2, tm):
    bsz, seq, d = x.shape
    nqkv = w_qkv.shape[1]
    aw = ATTN_WIDTH
    qkv = pl.pallas_call(
        functools.partial(_qkv_kernel, nc=aw),
        grid=(bsz, seq // tm),
        in_specs=[pl.BlockSpec((1, tm, d), lambda b, i: (b, i, 0)), _full((1, d)), _full((d, nqkv))],
        out_specs=pl.BlockSpec((1, tm, nqkv), lambda b, i: (b, i, 0)),
        out_shape=jax.ShapeDtypeStruct((bsz, seq, nqkv), BF16),
        compiler_params=_cparams("parallel", "parallel"),
        name="qkv_proj",
    )(x, g_mix.reshape(1, d), w_qkv.astype(BF16))

    slopes = _alibi_slopes()
    parts_per_row = nqkv // aw
    outs, lses = [], []
    for gi, (window, dil) in enumerate(ATTN_GROUPS):
        radius = window // (2 * dil)
        seg = seq // dil
        tq = min(256, seg)
        hr = tq // radius
        last = seg // radius - 1
        view = qkv.reshape(bsz, seg, dil * nqkv)

        def col(part, r, gi=gi):
            return r * parts_per_row + gi * 3 + part

        def cur(part):
            return pl.BlockSpec((1, tq, aw), lambda b, r, j, part=part: (b, j, col(part, r)))

        def prev(part):
            return pl.BlockSpec((1, radius, aw),
                                lambda b, r, j, part=part: (b, jnp.maximum(j * hr - 1, 0), col(part, r)))

        def nxt(part):
            return pl.BlockSpec((1, radius, aw),
                                lambda b, r, j, part=part: (b, jnp.minimum((j + 1) * hr, last), col(part, r)))

        o, lse = pl.pallas_call(
            functools.partial(_attn_kernel, tq=tq, seg=seg, dil=dil, radius=radius,
                              slopes=slopes[gi * ATTN_HEADS:(gi + 1) * ATTN_HEADS]),
            grid=(bsz, dil, seg // tq),
            in_specs=[cur(0), prev(1), cur(1), nxt(1), prev(2), cur(2), nxt(2)],
            out_specs=[pl.BlockSpec((1, tq, aw), lambda b, r, j: (b, j, r)),
                       pl.BlockSpec((1, tq, aw), lambda b, r, j: (b, j, r))],
            out_shape=[jax.ShapeDtypeStruct((bsz, seg, dil * aw), BF16),
                       jax.ShapeDtypeStruct((bsz, seg, dil * aw), F32)],
            compiler_params=_cparams("parallel", "parallel", "parallel"),
            name=f"dilated_attn_{dil}",
        )(view, view, view, view, view, view, view)
        outs.append(o.reshape(bsz, seq, aw))
        lses.append(lse.reshape(bsz, seq, aw))

    f = w1.shape[1]
    row = lambda width: pl.BlockSpec((1, tm, width), lambda b, i: (b, i, 0))
    return pl.pallas_call(
        functools.partial(_attn_out_ffn_kernel, fc=256),
        grid=(bsz, seq // tm),
        in_specs=[row(d)] + [row(aw)] * 6 + [_full((aw, d)), _full((1, d)),
                                             _full((d, f)), _full((d, f)), _full((f, d))],
        out_specs=row(d),
        out_shape=jax.ShapeDtypeStruct(x.shape, F32),
        compiler_params=_cparams("parallel", "parallel"),
        name="attn_out_ffn",
    )(x, *outs, *lses, w_out.astype(BF16), g_ffn.reshape(1, d), w1.astype(BF16), w3.astype(BF16),
      w2.astype(BF16))


def _conv_glu_kernel(x_ref, g_ref, w_ref, u_ref):
    d = x_ref.shape[2]
    hb = _rms(x_ref[0], g_ref[...]).astype(BF16)
    a = _dot(hb, w_ref[:, :d])
    b = _dot(hb, w_ref[:, d:])
    u_ref[0] = a * jax.nn.sigmoid(b)


def _conv_out_kernel(x_ref, up_ref, uc_ref, un_ref, wdw_ref, bdw_ref, lg_ref, lb_ref, w2_ref, gf_ref, wrt_ref,
                     xo_ref, h_ref, idx_ref, gate_ref, ext_ref, *, ts, seq):
    i = pl.program_id(1)
    n_ext = ts + 2 * HALO
    ue = jnp.concatenate([up_ref[0], uc_ref[0], un_ref[0]], axis=0)
    pos = i * ts - HALO + lax.broadcasted_iota(jnp.int32, (n_ext, 1), 0)
    ext_ref[...] = jnp.where((pos >= 0) & (pos < seq), ue, 0.0)
    half = CONV_WIDTH // 2
    acc = None
    for k in range(CONV_WIDTH):
        start = HALO + k - half
        term = ext_ref[start:start + ts, :] * wdw_ref[k:k + 1, :]
        acc = term if acc is None else acc + term
    u = acc + bdw_ref[...]
    mu = jnp.mean(u, axis=-1, keepdims=True)
    uc = u - mu
    var = jnp.mean(uc * uc, axis=-1, keepdims=True)
    z = uc * lax.rsqrt(var + LN_EPS) * lg_ref[...] + lb_ref[...]
    z = (z * jax.nn.sigmoid(z)).astype(BF16)
    x1 = x_ref[0] + _dot(z, w2_ref[...])
    _moe_prologue(x1, gf_ref, wrt_ref, xo_ref, h_ref, idx_ref, gate_ref)


def _conv_layer(x, g_mix, w_pw1, w_dw, b_dw, ln_g, ln_b, w_pw2, g_ffn, w_router, tm, ts):
    bsz, seq, d = x.shape
    n_tok = bsz * seq
    u = pl.pallas_call(
        _conv_glu_kernel,
        grid=(bsz, seq // tm),
        in_specs=[pl.BlockSpec((1, tm, d), lambda b, i: (b, i, 0)), _full((1, d)), _full((d, 2 * d))],
        out_specs=pl.BlockSpec((1, tm, d), lambda b, i: (b, i, 0)),
        out_shape=jax.ShapeDtypeStruct(x.shape, F32),
        compiler_params=_cparams("parallel", "parallel"),
        name="conv_glu",
    )(x, g_mix.reshape(1, d), w_pw1.astype(BF16))

    nt = seq // ts
    vec = lambda a: a.reshape(1, d)
    return pl.pallas_call(
        functools.partial(_conv_out_kernel, ts=ts, seq=seq),
        grid=(bsz, nt),
        in_specs=[pl.BlockSpec((1, ts, d), lambda b, i: (b, i, 0))] + _halo_specs(ts, seq, d) + [
            _full((CONV_WIDTH, d)), _full((1, d)), _full((1, d)), _full((1, d)), _full((d, d)),
            _full((1, d)), _full((N_EXPERTS, d))],
        out_specs=[pl.BlockSpec((1, ts, d), lambda b, i: (b, i, 0)),
                   pl.BlockSpec((1, ts, d), lambda b, i: (b, i, 0)),
                   pl.BlockSpec((TOP_K, ts), lambda b, i: (0, b * nt + i)),
                   pl.BlockSpec((TOP_K, ts), lambda b, i: (0, b * nt + i))],
        out_shape=[jax.ShapeDtypeStruct(x.shape, F32), jax.ShapeDtypeStruct(x.shape, BF16),
                   jax.ShapeDtypeStruct((TOP_K, n_tok), jnp.int32),
                   jax.ShapeDtypeStruct((TOP_K, n_tok), F32)],
        scratch_shapes=[pltpu.VMEM((ts + 2 * HALO, d), F32)],
        compiler_params=_cparams("parallel", "parallel"),
        name="conv_out",
    )(x, u, u, u, w_dw, vec(b_dw), vec(ln_g), vec(ln_b), w_pw2.astype(BF16), vec(g_ffn), w_router.T)


def kernel(x, g_mix, g_ffn, g_final, a_w_grp, a_scale, b_w_out, c_w_qkv, c_w_out, d_w_pw1, d_w_dw, d_b_dw,
           d_ln_g, d_ln_b, d_w_pw2, ffn_w1, ffn_w3, ffn_w2, moe_router, moe_w1, moe_w3, moe_w2):
    assert g_mix.shape[0] == 4, "one layer of each mixer kind"
    tm = min(512, x.shape[1])
    x = _pool_ffn_layer(x, g_mix[0], a_w_grp[0], a_scale[0], g_ffn[0], ffn_w1[0], ffn_w3[0], ffn_w2[0], tm)
    x, h, idx, gates = _fourier_layer(x, g_mix[1], b_w_out[0], g_ffn[1], moe_router[0], tm)
    x = _moe_layer(x, h, idx, gates, moe_w1[0], moe_w3[0], moe_w2[0])
    x = _attention_ffn_layer(x, g_mix[2], c_w_qkv[0], c_w_out[0], g_ffn[2], ffn_w1[1], ffn_w3[1], ffn_w2[1], tm)
    x, h, idx, gates = _conv_layer(x, g_mix[3], d_w_pw1[0], d_w_dw[0], d_b_dw[0], d_ln_g[0], d_ln_b[0],
                                   d_w_pw2[0], g_ffn[3], moe_router[1], tm, min(256, x.shape[1]))
    return _moe_layer(x, h, idx, gates, moe_w1[1], moe_w3[1], moe_w2[1], g_final=g_final)
```

```python
import functools
import math

import numpy as np
import jax
import jax.numpy as jnp
from jax import lax
from jax.experimental import pallas as pl
from jax.experimental.pallas import tpu as pltpu

F32 = jnp.float32
BF16 = jnp.bfloat16

RMS_EPS = 1e-6
LN_EPS = 1e-5
MASK_VALUE = -1e30
POOL_WINDOWS = (2, 4, 8, 16)
FOURIER_GROUPS = 4
ATTN_GROUPS = ((128, 1), (512, 4), (2048, 16))
ATTN_HEADS = 8
ATTN_HEAD_DIM = 64
ATTN_WIDTH = ATTN_HEADS * ATTN_HEAD_DIM
CONV_WIDTH = 31
N_EXPERTS = 8
TOP_K = 2

LANES = 128
HALO = 16
FFT_N2 = 128
MOE_ROWS = 512
MOE_CHUNK = 512
VMEM_LIMIT_BYTES = 56 * 1024 * 1024


def _cparams(*sem):
    return pltpu.CompilerParams(dimension_semantics=sem, vmem_limit_bytes=VMEM_LIMIT_BYTES)


def _dot(a, b):
    return jnp.dot(a, b, preferred_element_type=F32)


def _rms(x, g):
    ms = jnp.mean(x * x, axis=-1, keepdims=True)
    return x * lax.rsqrt(ms + RMS_EPS) * g


def _swiglu(hb, w1_ref, w3_ref, w2_ref, fc):
    f = w1_ref.shape[-1]
    acc = None
    for c in range(0, f, fc):
        a = _dot(hb, w1_ref[:, c:c + fc])
        b = _dot(hb, w3_ref[:, c:c + fc])
        g = (a * jax.nn.sigmoid(a) * b).astype(BF16)
        y = _dot(g, w2_ref[c:c + fc, :])
        acc = y if acc is None else acc + y
    return acc


def _route(h32, wrt_ref, idx_ref, gate_ref):
    logits = lax.dot_general(wrt_ref[...], h32, (((1,), (1,)), ((), ())),
                             precision=lax.Precision.HIGHEST, preferred_element_type=F32)
    e_iota = lax.broadcasted_iota(jnp.int32, logits.shape, 0)
    m1 = jnp.max(logits, axis=0, keepdims=True)
    i1 = jnp.min(jnp.where(logits == m1, e_iota, N_EXPERTS), axis=0, keepdims=True)
    rest = jnp.where(e_iota == i1, -jnp.inf, logits)
    m2 = jnp.max(rest, axis=0, keepdims=True)
    i2 = jnp.min(jnp.where(rest == m2, e_iota, N_EXPERTS), axis=0, keepdims=True)
    e2 = jnp.exp(m2 - m1)
    den = 1.0 + e2
    idx_ref[0:1, :] = i1
    idx_ref[1:2, :] = i2
    gate_ref[0:1, :] = 1.0 / den
    gate_ref[1:2, :] = e2 / den


def _moe_prologue(x1, gf_ref, wrt_ref, x_out_ref, h_ref, idx_ref, gate_ref):
    x_out_ref[0] = x1
    h2 = _rms(x1, gf_ref[...])
    h_ref[0] = h2.astype(BF16)
    _route(h2, wrt_ref, idx_ref, gate_ref)


def _pool_ffn_kernel(xp_ref, xc_ref, xn_ref, gm_ref, wg_ref, sc_ref, gf_ref, w1_ref, w3_ref, w2_ref,
                     o_ref, *, ts, seq, fc):
    i = pl.program_id(1)
    xc = xc_ref[0]
    d = xc.shape[1]
    gd = d // len(POOL_WINDOWS)
    n_ext = ts + 2 * HALO
    xe = jnp.concatenate([xp_ref[0], xc, xn_ref[0]], axis=0)
    he = _rms(xe, gm_ref[...])
    pos = i * ts - HALO + lax.broadcasted_iota(jnp.int32, (n_ext, 1), 0)
    he = jnp.where((pos >= 0) & (pos < seq), he, 0.0)

    s = he + pltpu.roll(he, 1, 0)
    sums = [s]
    half = 1
    for _ in POOL_WINDOWS[1:]:
        s = s[:, gd:]
        s = pltpu.roll(s, half, 0) + pltpu.roll(s, n_ext - half, 0)
        sums.append(s)
        half *= 2

    tpos = i * ts + lax.broadcasted_iota(jnp.int32, (ts, 1), 0)
    ys = []
    for gi, w in enumerate(POOL_WINDOWS):
        cnt = jnp.minimum(tpos + w // 2, seq) - jnp.maximum(tpos - w // 2, 0)
        mean = sums[gi][HALO:HALO + ts, :gd] / cnt.astype(F32)
        pooled = mean - he[HALO:HALO + ts, gi * gd:(gi + 1) * gd]
        ys.append(_dot(pooled.astype(BF16), wg_ref[gi]))
    x1 = xc + jnp.concatenate(ys, axis=1) * sc_ref[...]
    hb = _rms(x1, gf_ref[...]).astype(BF16)
    o_ref[0] = x1 + _swiglu(hb, w1_ref, w3_ref, w2_ref, fc)


def _halo_specs(ts, seq, d):
    r = ts // HALO
    last = seq // HALO - 1
    return [
        pl.BlockSpec((1, HALO, d), lambda b, i: (b, jnp.maximum(i * r - 1, 0), 0)),
        pl.BlockSpec((1, ts, d), lambda b, i: (b, i, 0)),
        pl.BlockSpec((1, HALO, d), lambda b, i: (b, jnp.minimum((i + 1) * r, last), 0)),
    ]


def _full(shape):
    n = len(shape)
    return pl.BlockSpec(shape, lambda *_: (0,) * n)


def _pool_ffn_layer(x, g_mix, w_grp, scale, g_ffn, w1, w3, w2, ts):
    bsz, seq, d = x.shape
    f = w1.shape[1]
    kern = functools.partial(_pool_ffn_kernel, ts=ts, seq=seq, fc=256)
    return pl.pallas_call(
        kern,
        grid=(bsz, seq // ts),
        in_specs=_halo_specs(ts, seq, d) + [
            _full((1, d)), _full(w_grp.shape), _full((1, d)), _full((1, d)),
            _full((d, f)), _full((d, f)), _full((f, d))],
        out_specs=pl.BlockSpec((1, ts, d), lambda b, i: (b, i, 0)),
        out_shape=jax.ShapeDtypeStruct(x.shape, F32),
        compiler_params=_cparams("parallel", "parallel"),
        name="pool_ffn",
    )(x, x, x, g_mix.reshape(1, d), w_grp.astype(BF16), scale.reshape(1, d), g_ffn.reshape(1, d),
      w1.astype(BF16), w3.astype(BF16), w2.astype(BF16))


def _fft_stage1_kernel(x_ref, g_ref, f1_ref, o_ref, *, nseg):
    for j in range(nseg):
        h = _rms(x_ref[0, :, j, :], g_ref[...]).astype(BF16)
        o_ref[0, j] = _dot(f1_ref[...], h)


def _fft_stage2_kernel(br_ref, bi_ref, m_ref, o_ref, *, nk):
    n2 = FFT_N2
    for j in range(nk):
        rhs = jnp.concatenate([br_ref[0, :, j, :], bi_ref[0, :, j, :]], axis=0).astype(BF16)
        res = _dot(m_ref[j], rhs)
        o_ref[0, 0, :, j, :] = res[:n2]
        o_ref[0, 1, :, j, :] = res[n2:]


def _fourier_out_kernel(x_ref, ar_ref, ai_ref, cc_ref, sc_ref, wo_ref, gf_ref, wrt_ref,
                        xo_ref, h_ref, idx_ref, gate_ref, *, norm):
    d = x_ref.shape[2]
    gd = d // FOURIER_GROUPS
    fs = []
    for gi in range(FOURIER_GROUPS):
        sl = slice(gi * gd, (gi + 1) * gd)
        fs.append(_dot(ar_ref[0, 0, :, sl].astype(BF16), cc_ref[...])
                  + _dot(ai_ref[0, 0, :, sl].astype(BF16), sc_ref[...]))
    f = (jnp.concatenate(fs, axis=1) * norm).astype(BF16)
    x1 = x_ref[0] + _dot(f, wo_ref[...])
    _moe_prologue(x1, gf_ref, wrt_ref, xo_ref, h_ref, idx_ref, gate_ref)


def _dft_tables(seq, gd):
    n2 = FFT_N2
    n1 = seq // n2
    k = np.arange(n1)
    ang1 = 2.0 * np.pi * ((k[:, None] * k[None, :]) % n1) / n1
    f1 = np.concatenate([np.cos(ang1), -np.sin(ang1)], axis=0)
    c = np.arange(gd)
    angc = 2.0 * np.pi * ((c[:, None] * c[None, :]) % gd) / gd
    return (jnp.asarray(f1, BF16), jnp.asarray(np.cos(angc), BF16), jnp.asarray(np.sin(angc), BF16))


def _twiddled_dft(seq):
    n2 = FFT_N2
    n1 = seq // n2
    k1 = lax.broadcasted_iota(jnp.int32, (n1, n2, n2), 0)
    k2 = lax.broadcasted_iota(jnp.int32, (n1, n2, n2), 1)
    a = lax.broadcasted_iota(jnp.int32, (n1, n2, n2), 2)
    ang = ((a * (n1 * k2 + k1)) % seq).astype(F32) * (2.0 * math.pi / seq)
    mr = jnp.cos(ang)
    mi = -jnp.sin(ang)
    top = jnp.concatenate([mr, -mi], axis=2)
    bot = jnp.concatenate([mi, mr], axis=2)
    return jnp.concatenate([top, bot], axis=1).astype(BF16)


def _fourier_layer(x, g_mix, w_out, g_ffn, w_router, tm):
    bsz, seq, d = x.shape
    n2 = FFT_N2
    n1 = seq // n2
    gd = d // FOURIER_GROUPS
    f1, cc, sc = _dft_tables(seq, gd)
    mtab = _twiddled_dft(seq)
    nseg = 8
    b1 = pl.pallas_call(
        functools.partial(_fft_stage1_kernel, nseg=nseg),
        grid=(bsz, n2 // nseg),
        in_specs=[pl.BlockSpec((1, n1, nseg, d), lambda b, t: (b, 0, t, 0)), _full((1, d)), _full(f1.shape)],
        out_specs=pl.BlockSpec((1, nseg, 2 * n1, d), lambda b, t: (b, t, 0, 0)),
        out_shape=jax.ShapeDtypeStruct((bsz, n2, 2 * n1, d), F32),
        compiler_params=_cparams("parallel", "parallel"),
        name="fft_stage1",
    )(x.reshape(bsz, n1, n2, d), g_mix.reshape(1, d), f1)

    nk = 8
    a = pl.pallas_call(
        functools.partial(_fft_stage2_kernel, nk=nk),
        grid=(bsz, n1 // nk),
        in_specs=[pl.BlockSpec((1, n2, nk, d), lambda b, t: (b, 0, t, 0)),
                  pl.BlockSpec((1, n2, nk, d), lambda b, t: (b, 0, n1 // nk + t, 0)),
                  pl.BlockSpec((nk, 2 * n2, 2 * n2), lambda b, t: (t, 0, 0))],
        out_specs=pl.BlockSpec((1, 2, n2, nk, d), lambda b, t: (b, 0, 0, t, 0)),
        out_shape=jax.ShapeDtypeStruct((bsz, 2, n2, n1, d), F32),
        compiler_params=_cparams("parallel", "parallel"),
        name="fft_stage2",
    )(b1, b1, mtab)

    av = a.reshape(bsz, 2, seq, d)
    n_tok = bsz * seq
    nt = seq // tm
    outs = pl.pallas_call(
        functools.partial(_fourier_out_kernel, norm=1.0 / math.sqrt(seq * gd)),
        grid=(bsz, nt),
        in_specs=[pl.BlockSpec((1, tm, d), lambda b, i: (b, i, 0)),
                  pl.BlockSpec((1, 1, tm, d), lambda b, i: (b, 0, i, 0)),
                  pl.BlockSpec((1, 1, tm, d), lambda b, i: (b, 1, i, 0)),
                  _full((gd, gd)), _full((gd, gd)), _full((d, d)), _full((1, d)), _full((N_EXPERTS, d))],
        out_specs=[pl.BlockSpec((1, tm, d), lambda b, i: (b, i, 0)),
                   pl.BlockSpec((1, tm, d), lambda b, i: (b, i, 0)),
                   pl.BlockSpec((TOP_K, tm), lambda b, i: (0, b * nt + i)),
                   pl.BlockSpec((TOP_K, tm), lambda b, i: (0, b * nt + i))],
        out_shape=[jax.ShapeDtypeStruct(x.shape, F32), jax.ShapeDtypeStruct(x.shape, BF16),
                   jax.ShapeDtypeStruct((TOP_K, n_tok), jnp.int32),
                   jax.ShapeDtypeStruct((TOP_K, n_tok), F32)],
        compiler_params=_cparams("parallel", "parallel"),
        name="fourier_out",
    )(x, av, av, cc, sc, w_out.astype(BF16), g_ffn.reshape(1, d), w_router.T)
    return outs


def _moe_plan(idx, gates, n_tok):
    rows, chunk = MOE_ROWS, MOE_CHUNK
    n_assign = n_tok * TOP_K
    n_blocks = -(-n_assign // rows) + N_EXPERTS
    n_rows = n_blocks * rows
    n_chunks = n_tok // chunk
    max_pairs = n_blocks + N_EXPERTS * (n_chunks - 1)

    e_flat = idx.T.reshape(-1)
    g_flat = gates.T.reshape(-1)
    onehot = (e_flat[:, None] == jnp.arange(N_EXPERTS, dtype=jnp.int32)[None, :]).astype(jnp.int32)
    csum = jnp.cumsum(onehot, axis=0)
    rank = jnp.take_along_axis(csum, e_flat[:, None], axis=1)[:, 0] - 1
    counts = csum[-1]
    pcounts = (counts + rows - 1) // rows * rows
    pend = jnp.cumsum(pcounts)
    pstart = pend - pcounts
    dest = pstart[e_flat] + rank
    tok = jnp.arange(n_assign, dtype=jnp.int32) // TOP_K
    row_tok = jnp.full((n_rows,), -1, jnp.int32).at[dest].set(tok)
    row_gate = jnp.zeros((n_rows,), F32).at[dest].set(g_flat)

    n_used = pend[-1] // rows
    blk = jnp.arange(n_blocks, dtype=jnp.int32)
    block_exp = jnp.minimum(jnp.searchsorted(pend, blk * rows, side='right'), N_EXPERTS - 1).astype(jnp.int32)
    rt = row_tok.reshape(n_blocks, rows)
    used = blk < n_used
    c_lo = jnp.where(used, rt[:, 0] // chunk, 0)
    c_hi = jnp.where(used, jnp.max(rt, axis=1) // chunk, 0)
    npairs = jnp.where(used, c_hi - c_lo + 1, 0)
    cum = jnp.cumsum(npairs)
    total = cum[-1]
    first_pair = cum - npairs

    p = jnp.arange(max_pairs, dtype=jnp.int32)
    valid = p < total
    pb = jnp.minimum(jnp.searchsorted(cum, p, side='right'), n_used - 1).astype(jnp.int32)
    pc = jnp.where(valid, c_lo[pb] + p - first_pair[pb], c_hi[n_used - 1]).astype(jnp.int32)
    p_first = valid & (p == first_pair[pb])
    p_last = valid & (p == cum[pb] - 1)
    pflags = (valid.astype(jnp.int32) + 2 * p_first.astype(jnp.int32) + 4 * p_last.astype(jnp.int32))

    order = jnp.argsort(jnp.where(valid, pc, n_chunks), stable=True)
    cb_sorted = pb[order]
    cc_sorted = pc[order]
    last_valid = total - 1
    cb = jnp.where(valid, cb_sorted, cb_sorted[last_valid]).astype(jnp.int32)
    cc = jnp.where(valid, cc_sorted, n_chunks - 1).astype(jnp.int32)
    prev_cc = jnp.concatenate([jnp.full((1,), -1, jnp.int32), cc[:-1]])
    next_cc = jnp.concatenate([cc[1:], jnp.full((1,), -1, jnp.int32)])
    c_first = valid & (cc != prev_cc)
    c_last = valid & ((cc != next_cc) | (p == last_valid))
    cflags = (valid.astype(jnp.int32) + 2 * c_first.astype(jnp.int32) + 4 * c_last.astype(jnp.int32))

    return dict(n_blocks=n_blocks, max_pairs=max_pairs, row_tok=row_tok, row_gate=row_gate,
                n_used=n_used.reshape(1).astype(jnp.int32), block_exp=block_exp,
                pb=pb, pc=pc, pflags=pflags, cb=cb, cc=cc, cflags=cflags)


def _dispatch_kernel(pb_ref, pc_ref, pf_ref, h_ref, tok_ref, xb_ref, acc_ref):
    p = pl.program_id(0)
    flags = pf_ref[p]
    rows, chunk = acc_ref.shape[0], h_ref.shape[0]

    @pl.when((flags & 2) != 0)
    def _():
        acc_ref[...] = jnp.zeros_like(acc_ref)

    @pl.when((flags & 1) != 0)
    def _():
        rel = tok_ref[...] - pc_ref[p] * chunk
        sel = rel == lax.broadcasted_iota(jnp.int32, (rows, chunk), 1)
        acc_ref[...] += _dot(jnp.where(sel, 1.0, 0.0).astype(BF16), h_ref[...])

    @pl.when((flags & 4) != 0)
    def _():
        xb_ref[...] = acc_ref[...].astype(BF16)


def _expert_kernel(be_ref, nu_ref, xb_ref, gate_ref, w1_ref, w3_ref, w2_ref, y_ref, acc_ref, *, nf, fc):
    b = pl.program_id(0)
    f = pl.program_id(1)

    @pl.when(b < nu_ref[0])
    def _():
        y = _swiglu(xb_ref[...], w1_ref.at[0], w3_ref.at[0], w2_ref.at[0], fc)

        @pl.when(f == 0)
        def _():
            acc_ref[...] = y

        @pl.when(f > 0)
        def _():
            acc_ref[...] += y

        @pl.when(f == nf - 1)
        def _():
            y_ref[...] = (acc_ref[...] * gate_ref[...]).astype(BF16)


def _combine_kernel(cb_ref, cc_ref, cf_ref, y_ref, tok_ref, x_ref, g_ref, o_ref, acc_ref, *, final_norm):
    p = pl.program_id(0)
    flags = cf_ref[p]
    chunk, rows = acc_ref.shape[0], y_ref.shape[0]

    @pl.when((flags & 2) != 0)
    def _():
        acc_ref[...] = jnp.zeros_like(acc_ref)

    @pl.when((flags & 1) != 0)
    def _():
        rel = tok_ref[0] - cc_ref[p] * chunk
        sel = rel == lax.broadcasted_iota(jnp.int32, (chunk, rows), 0)
        acc_ref[...] += _dot(jnp.where(sel, 1.0, 0.0).astype(BF16), y_ref[...])

    @pl.when((flags & 4) != 0)
    def _():
        xo = x_ref[...] + acc_ref[...]
        o_ref[...] = _rms(xo, g_ref[...]) if final_norm else xo


def _moe_layer(x, h, idx, gates, w1, w3, w2, g_final=None):
    bsz, seq, d = x.shape
    n_tok = bsz * seq
    f = w1.shape[2]
    rows, chunk = MOE_ROWS, MOE_CHUNK
    plan = _moe_plan(idx, gates, n_tok)
    n_blocks, max_pairs = plan['n_blocks'], plan['max_pairs']
    n_rows = n_blocks * rows

    xb = pl.pallas_call(
        _dispatch_kernel,
        grid_spec=pltpu.PrefetchScalarGridSpec(
            num_scalar_prefetch=3, grid=(max_pairs,),
            in_specs=[pl.BlockSpec((chunk, d), lambda p, pb, pc, pf: (pc[p], 0)),
                      pl.BlockSpec((rows, 1), lambda p, pb, pc, pf: (pb[p], 0))],
            out_specs=pl.BlockSpec((rows, d), lambda p, pb, pc, pf: (pb[p], 0)),
            scratch_shapes=[pltpu.VMEM((rows, d), F32)]),
        out_shape=jax.ShapeDtypeStruct((n_rows, d), BF16),
        compiler_params=_cparams("arbitrary"),
        name="moe_dispatch",
    )(plan['pb'], plan['pc'], plan['pflags'], h.reshape(n_tok, d), plan['row_tok'].reshape(n_rows, 1))

    nf = 2
    fblk = f // nf

    def blk_idx(b, nu):
        return jnp.minimum(b, nu[0] - 1)

    def f_idx(b, fi, nu):
        return jnp.where(b < nu[0], fi, nf - 1)

    y = pl.pallas_call(
        functools.partial(_expert_kernel, nf=nf, fc=256),
        grid_spec=pltpu.PrefetchScalarGridSpec(
            num_scalar_prefetch=2, grid=(n_blocks, nf),
            in_specs=[pl.BlockSpec((rows, d), lambda b, fi, be, nu: (blk_idx(b, nu), 0)),
                      pl.BlockSpec((rows, 1), lambda b, fi, be, nu: (blk_idx(b, nu), 0)),
                      pl.BlockSpec((1, d, fblk), lambda b, fi, be, nu: (be[blk_idx(b, nu)], 0, f_idx(b, fi, nu))),
                      pl.BlockSpec((1, d, fblk), lambda b, fi, be, nu: (be[blk_idx(b, nu)], 0, f_idx(b, fi, nu))),
                      pl.BlockSpec((1, fblk, d), lambda b, fi, be, nu: (be[blk_idx(b, nu)], f_idx(b, fi, nu), 0))],
            out_specs=pl.BlockSpec((rows, d), lambda b, fi, be, nu: (blk_idx(b, nu), 0)),
            scratch_shapes=[pltpu.VMEM((rows, d), F32)]),
        out_shape=jax.ShapeDtypeStruct((n_rows, d), BF16),
        compiler_params=_cparams("arbitrary", "arbitrary"),
        name="moe_experts",
    )(plan['block_exp'], plan['n_used'], xb, plan['row_gate'].reshape(n_rows, 1),
      w1.astype(BF16), w3.astype(BF16), w2.astype(BF16))

    final_norm = g_final is not None
    g = (g_final if final_norm else jnp.ones((d,), F32)).reshape(1, d)
    out = pl.pallas_call(
        functools.partial(_combine_kernel, final_norm=final_norm),
        grid_spec=pltpu.PrefetchScalarGridSpec(
            num_scalar_prefetch=3, grid=(max_pairs,),
            in_specs=[pl.BlockSpec((rows, d), lambda p, cb, cc, cf: (cb[p], 0)),
                      pl.BlockSpec((1, 1, rows), lambda p, cb, cc, cf: (cb[p], 0, 0)),
                      pl.BlockSpec((chunk, d), lambda p, cb, cc, cf: (cc[p], 0)),
                      pl.BlockSpec((1, d), lambda p, cb, cc, cf: (0, 0))],
            out_specs=pl.BlockSpec((chunk, d), lambda p, cb, cc, cf: (cc[p], 0)),
            scratch_shapes=[pltpu.VMEM((chunk, d), F32)]),
        out_shape=jax.ShapeDtypeStruct((n_tok, d), F32),
        compiler_params=_cparams("arbitrary"),
        name="moe_combine",
    )(plan['cb'], plan['cc'], plan['cflags'], y, plan['row_tok'].reshape(n_blocks, 1, rows),
      x.reshape(n_tok, d), g)
    return out.reshape(bsz, seq, d)


def _qkv_kernel(x_ref, g_ref, w_ref, *rest, nc):
    o_refs, h_ref = rest[:-1], rest[-1]
    tm = x_ref.shape[1]
    gw = 3 * ATTN_WIDTH
    h = _rms(x_ref[0], g_ref[...])
    nlc = h.shape[1] // LANES
    for c in range(nlc):
        h_ref[c] = h[:, c * LANES:(c + 1) * LANES]
    for gi, (_, dil) in enumerate(ATTN_GROUPS):
        n = tm // dil
        hb = h if dil == 1 else jnp.concatenate(
            [jnp.concatenate([h_ref[c, pl.ds(r, n, stride=dil), :] for c in range(nlc)], axis=1)
             for r in range(dil)], axis=0)
        hb = hb.astype(BF16)
        for c in range(gi * gw, (gi + 1) * gw, nc):
            res = _dot(hb, w_ref[:, c:c + nc]).astype(BF16)
            for r in range(dil):
                o_refs[gi][0, :, r * gw + c - gi * gw:r * gw + c - gi * gw + nc] = res[r * n:(r + 1) * n]


def _attn_kernel(q_ref, kp_ref, kc_ref, kn_ref, vp_ref, vc_ref, vn_ref, o_ref, lse_ref,
                 *, tq, seg, dil, radius, slopes):
    j = pl.program_id(2)
    hd = ATTN_HEAD_DIM
    tk = tq + 2 * radius
    q = q_ref[0] * jnp.asarray(1.0 / math.sqrt(hd), BF16)
    k = jnp.concatenate([kp_ref[0], kc_ref[0], kn_ref[0]], axis=0)
    v = jnp.concatenate([vp_ref[0], vc_ref[0], vn_ref[0]], axis=0)
    qi = lax.broadcasted_iota(jnp.int32, (tq, tk), 0)
    kj = lax.broadcasted_iota(jnp.int32, (tq, tk), 1)
    arel = jnp.abs(kj - radius - qi)
    kpos = j * tq - radius + kj
    valid = (arel <= radius) & (kpos >= 0) & (kpos < seg)
    dist = (dil * arel).astype(F32)
    low = lax.broadcasted_iota(jnp.int32, (1, 2 * hd), 1) < hd
    for hp in range(ATTN_HEADS // 2):
        sl = slice(hp * 2 * hd, (hp + 1) * 2 * hd)
        qp, kp, vp = q[:, sl], k[:, sl], v[:, sl]
        o_pair, l_pair = None, None
        for sub in range(2):
            mine = low if sub == 0 else jnp.logical_not(low)
            qm = jnp.where(mine, qp, jnp.zeros_like(qp))
            s = lax.dot_general(qm, kp, (((1,), (1,)), ((), ())), preferred_element_type=F32)
            s = jnp.where(valid, s - slopes[2 * hp + sub] * dist, MASK_VALUE)
            m = jnp.max(s, axis=-1, keepdims=True)
            p = jnp.exp(s - m)
            den = jnp.sum(p, axis=-1, keepdims=True)
            o = _dot(p.astype(BF16), vp) / den
            l = jnp.broadcast_to(m + jnp.log(den), o.shape)
            o_pair = o if sub == 0 else jnp.where(low, o_pair, o)
            l_pair = l if sub == 0 else jnp.where(low, l_pair, l)
        o_ref[0, :, sl] = o_pair.astype(BF16)
        lse_ref[0, :, sl] = l_pair


def _attn_out_ffn_kernel(x_ref, o0_ref, o1_ref, o2_ref, l0_ref, l1_ref, l2_ref, wo_ref, gf_ref,
                         w1_ref, w3_ref, w2_ref, out_ref, so_ref, sl_ref, *, fc):
    aw = ATTN_WIDTH
    ls, os_ = [], []
    for gi, (o_ref, l_ref) in enumerate(((o0_ref, l0_ref), (o1_ref, l1_ref), (o2_ref, l2_ref))):
        dil = ATTN_GROUPS[gi][1]
        if dil == 1:
            os_.append(o_ref[0].astype(F32))
            ls.append(l_ref[0])
            continue
        n = o_ref.shape[1]
        nlc = aw // LANES
        for r in range(dil):
            ov = o_ref[0, :, r * aw:(r + 1) * aw].astype(F32)
            lv = l_ref[0, :, r * aw:(r + 1) * aw]
            for c in range(nlc):
                so_ref[gi - 1, c, pl.ds(r, n, stride=dil), :] = ov[:, c * LANES:(c + 1) * LANES]
                sl_ref[gi - 1, c, pl.ds(r, n, stride=dil), :] = lv[:, c * LANES:(c + 1) * LANES]
        os_.append(jnp.concatenate([so_ref[gi - 1, c] for c in range(nlc)], axis=1))
        ls.append(jnp.concatenate([sl_ref[gi - 1, c] for c in range(nlc)], axis=1))
    m = jnp.maximum(jnp.maximum(ls[0], ls[1]), ls[2])
    es = [jnp.exp(l - m) for l in ls]
    den = es[0] + es[1] + es[2]
    o = (es[0] * os_[0] + es[1] * os_[1] + es[2] * os_[2]) / den
    x1 = x_ref[0] + _dot(o.astype(BF16), wo_ref[...])
    hb = _rms(x1, gf_ref[...]).astype(BF16)
    out_ref[0] = x1 + _swiglu(hb, w1_ref, w3_ref, w2_ref, fc)


def _alibi_slopes():
    n = len(ATTN_GROUPS) * ATTN_HEADS
    s = np.float32(2.0) ** (np.float32(-8.0) * np.arange(1, n + 1, dtype=np.float32) / np.float32(n))
    return [float(v) for v in s]


def _attention_ffn_layer(x, g_mix, w_qkv, w_out, g_ffn, w1, w3, w2, tm):
    bsz, seq, d = x.shape
    nqkv = w_qkv.shape[1]
    aw = ATTN_WIDTH
    gw = 3 * aw
    assert nqkv == len(ATTN_GROUPS) * gw
    qkvs = pl.pallas_call(
        functools.partial(_qkv_kernel, nc=aw),
        grid=(bsz, seq // tm),
        in_specs=[pl.BlockSpec((1, tm, d), lambda b, i: (b, i, 0)), _full((1, d)), _full((d, nqkv))],
        out_specs=[pl.BlockSpec((1, tm // dil, dil * gw), lambda b, i: (b, i, 0)) for _, dil in ATTN_GROUPS],
        out_shape=[jax.ShapeDtypeStruct((bsz, seq // dil, dil * gw), BF16) for _, dil in ATTN_GROUPS],
        scratch_shapes=[pltpu.VMEM((d // LANES, tm, LANES), F32)],
        compiler_params=_cparams("parallel", "parallel"),
        name="qkv_proj",
    )(x, g_mix.reshape(1, d), w_qkv.astype(BF16))

    slopes = _alibi_slopes()
    outs, lses = [], []
    for gi, (window, dil) in enumerate(ATTN_GROUPS):
        radius = window // (2 * dil)
        seg = seq // dil
        tq = min(256, seg)
        hr = tq // radius
        last = seg // radius - 1
        view = qkvs[gi]

        def col(part, r):
            return r * 3 + part

        def cur(part):
            return pl.BlockSpec((1, tq, aw), lambda b, r, j, part=part: (b, j, col(part, r)))

        def prev(part):
            return pl.BlockSpec((1, radius, aw),
                                lambda b, r, j, part=part: (b, jnp.maximum(j * hr - 1, 0), col(part, r)))

        def nxt(part):
            return pl.BlockSpec((1, radius, aw),
                                lambda b, r, j, part=part: (b, jnp.minimum((j + 1) * hr, last), col(part, r)))

        o, lse = pl.pallas_call(
            functools.partial(_attn_kernel, tq=tq, seg=seg, dil=dil, radius=radius,
                              slopes=slopes[gi * ATTN_HEADS:(gi + 1) * ATTN_HEADS]),
            grid=(bsz, dil, seg // tq),
            in_specs=[cur(0), prev(1), cur(1), nxt(1), prev(2), cur(2), nxt(2)],
            out_specs=[pl.BlockSpec((1, tq, aw), lambda b, r, j: (b, j, r)),
                       pl.BlockSpec((1, tq, aw), lambda b, r, j: (b, j, r))],
            out_shape=[jax.ShapeDtypeStruct((bsz, seg, dil * aw), BF16),
                       jax.ShapeDtypeStruct((bsz, seg, dil * aw), F32)],
            compiler_params=_cparams("parallel", "parallel", "parallel"),
            name=f"dilated_attn_{dil}",
        )(view, view, view, view, view, view, view)
        outs.append(o)
        lses.append(lse)

    f = w1.shape[1]
    row = lambda width: pl.BlockSpec((1, tm, width), lambda b, i: (b, i, 0))
    dilated = [pl.BlockSpec((1, tm // dil, dil * aw), lambda b, i: (b, i, 0)) for _, dil in ATTN_GROUPS]
    return pl.pallas_call(
        functools.partial(_attn_out_ffn_kernel, fc=256),
        grid=(bsz, seq // tm),
        in_specs=[row(d)] + dilated + dilated + [_full((aw, d)), _full((1, d)),
                                                 _full((d, f)), _full((d, f)), _full((f, d))],
        out_specs=row(d),
        out_shape=jax.ShapeDtypeStruct(x.shape, F32),
        scratch_shapes=[pltpu.VMEM((len(ATTN_GROUPS) - 1, aw // LANES, tm, LANES), F32),
                        pltpu.VMEM((len(ATTN_GROUPS) - 1, aw // LANES, tm, LANES), F32)],
        compiler_params=_cparams("parallel", "parallel"),
        name="attn_out_ffn",
    )(x, *outs, *lses, w_out.astype(BF16), g_ffn.reshape(1, d), w1.astype(BF16), w3.astype(BF16),
      w2.astype(BF16))


def _conv_glu_kernel(x_ref, g_ref, w_ref, u_ref):
    d = x_ref.shape[2]
    hb = _rms(x_ref[0], g_ref[...]).astype(BF16)
    a = _dot(hb, w_ref[:, :d])
    b = _dot(hb, w_ref[:, d:])
    u_ref[0] = a * jax.nn.sigmoid(b)


def _conv_out_kernel(x_ref, up_ref, uc_ref, un_ref, wdw_ref, bdw_ref, lg_ref, lb_ref, w2_ref, gf_ref, wrt_ref,
                     xo_ref, h_ref, idx_ref, gate_ref, ext_ref, *, ts, seq):
    i = pl.program_id(1)
    n_ext = ts + 2 * HALO
    ue = jnp.concatenate([up_ref[0], uc_ref[0], un_ref[0]], axis=0)
    pos = i * ts - HALO + lax.broadcasted_iota(jnp.int32, (n_ext, 1), 0)
    ext_ref[...] = jnp.where((pos >= 0) & (pos < seq), ue, 0.0)
    half = CONV_WIDTH // 2
    acc = None
    for k in range(CONV_WIDTH):
        start = HALO + k - half
        term = ext_ref[start:start + ts, :] * wdw_ref[k:k + 1, :]
        acc = term if acc is None else acc + term
    u = acc + bdw_ref[...]
    mu = jnp.mean(u, axis=-1, keepdims=True)
    uc = u - mu
    var = jnp.mean(uc * uc, axis=-1, keepdims=True)
    z = uc * lax.rsqrt(var + LN_EPS) * lg_ref[...] + lb_ref[...]
    z = (z * jax.nn.sigmoid(z)).astype(BF16)
    x1 = x_ref[0] + _dot(z, w2_ref[...])
    _moe_prologue(x1, gf_ref, wrt_ref, xo_ref, h_ref, idx_ref, gate_ref)


def _conv_layer(x, g_mix, w_pw1, w_dw, b_dw, ln_g, ln_b, w_pw2, g_ffn, w_router, tm, ts):
    bsz, seq, d = x.shape
    n_tok = bsz * seq
    u = pl.pallas_call(
        _conv_glu_kernel,
        grid=(bsz, seq // tm),
        in_specs=[pl.BlockSpec((1, tm, d), lambda b, i: (b, i, 0)), _full((1, d)), _full((d, 2 * d))],
        out_specs=pl.BlockSpec((1, tm, d), lambda b, i: (b, i, 0)),
        out_shape=jax.ShapeDtypeStruct(x.shape, F32),
        compiler_params=_cparams("parallel", "parallel"),
        name="conv_glu",
    )(x, g_mix.reshape(1, d), w_pw1.astype(BF16))

    nt = seq // ts
    vec = lambda a: a.reshape(1, d)
    return pl.pallas_call(
        functools.partial(_conv_out_kernel, ts=ts, seq=seq),
        grid=(bsz, nt),
        in_specs=[pl.BlockSpec((1, ts, d), lambda b, i: (b, i, 0))] + _halo_specs(ts, seq, d) + [
            _full((CONV_WIDTH, d)), _full((1, d)), _full((1, d)), _full((1, d)), _full((d, d)),
            _full((1, d)), _full((N_EXPERTS, d))],
        out_specs=[pl.BlockSpec((1, ts, d), lambda b, i: (b, i, 0)),
                   pl.BlockSpec((1, ts, d), lambda b, i: (b, i, 0)),
                   pl.BlockSpec((TOP_K, ts), lambda b, i: (0, b * nt + i)),
                   pl.BlockSpec((TOP_K, ts), lambda b, i: (0, b * nt + i))],
        out_shape=[jax.ShapeDtypeStruct(x.shape, F32), jax.ShapeDtypeStruct(x.shape, BF16),
                   jax.ShapeDtypeStruct((TOP_K, n_tok), jnp.int32),
                   jax.ShapeDtypeStruct((TOP_K, n_tok), F32)],
        scratch_shapes=[pltpu.VMEM((ts + 2 * HALO, d), F32)],
        compiler_params=_cparams("parallel", "parallel"),
        name="conv_out",
    )(x, u, u, u, w_dw, vec(b_dw), vec(ln_g), vec(ln_b), w_pw2.astype(BF16), vec(g_ffn), w_router.T)


def kernel(x, g_mix, g_ffn, g_final, a_w_grp, a_scale, b_w_out, c_w_qkv, c_w_out, d_w_pw1, d_w_dw, d_b_dw,
           d_ln_g, d_ln_b, d_w_pw2, ffn_w1, ffn_w3, ffn_w2, moe_router, moe_w1, moe_w3, moe_w2):
    assert g_mix.shape[0] == 4, "one layer of each mixer kind"
    tm = min(512, x.shape[1])
    x = _pool_ffn_layer(x, g_mix[0], a_w_grp[0], a_scale[0], g_ffn[0], ffn_w1[0], ffn_w3[0], ffn_w2[0], tm)
    x, h, idx, gates = _fourier_layer(x, g_mix[1], b_w_out[0], g_ffn[1], moe_router[0], tm)
    x = _moe_layer(x, h, idx, gates, moe_w1[0], moe_w3[0], moe_w2[0])
    x = _attention_ffn_layer(x, g_mix[2], c_w_qkv[0], c_w_out[0], g_ffn[2], ffn_w1[1], ffn_w3[1], ffn_w2[1], tm)
    x, h, idx, gates = _conv_layer(x, g_mix[3], d_w_pw1[0], d_w_dw[0], d_b_dw[0], d_ln_g[0], d_ln_b[0],
                                   d_w_pw2[0], g_ffn[3], moe_router[1], tm, min(256, x.shape[1]))
    return _moe_layer(x, h, idx, gates, moe_w1[1], moe_w3[1], moe_w2[1], g_final=g_final)
```

```python
import functools
import math

import numpy as np
import jax
import jax.numpy as jnp
from jax import lax
from jax.experimental import pallas as pl
from jax.experimental.pallas import tpu as pltpu

F32 = jnp.float32
BF16 = jnp.bfloat16

RMS_EPS = 1e-6
LN_EPS = 1e-5
MASK_VALUE = -1e30
POOL_WINDOWS = (2, 4, 8, 16)
FOURIER_GROUPS = 4
ATTN_GROUPS = ((128, 1), (512, 4), (2048, 16))
ATTN_HEADS = 8
ATTN_HEAD_DIM = 64
ATTN_WIDTH = ATTN_HEADS * ATTN_HEAD_DIM
CONV_WIDTH = 31
N_EXPERTS = 8
TOP_K = 2

LANES = 128
HALO = 16
FFT_N2 = 128
MOE_ROWS = 512
MOE_CHUNK = 512
SEG_ALIGN = 16
SEG_PIECES = (512, 256, 128, 64, 32, 16)
LOCAL_ROWS = 1152
VMEM_LIMIT_BYTES = 56 * 1024 * 1024


def _cparams(*sem):
    return pltpu.CompilerParams(dimension_semantics=sem, vmem_limit_bytes=VMEM_LIMIT_BYTES)


def _dot(a, b):
    return jnp.dot(a, b, preferred_element_type=F32)


def _rms(x, g):
    ms = jnp.mean(x * x, axis=-1, keepdims=True)
    return x * lax.rsqrt(ms + RMS_EPS) * g


def _swiglu(hb, w1_ref, w3_ref, w2_ref, fc):
    f = w1_ref.shape[-1]
    acc = None
    for c in range(0, f, fc):
        a = _dot(hb, w1_ref[:, c:c + fc])
        b = _dot(hb, w3_ref[:, c:c + fc])
        g = (a * jax.nn.sigmoid(a) * b).astype(BF16)
        y = _dot(g, w2_ref[c:c + fc, :])
        acc = y if acc is None else acc + y
    return acc


def _route(h32, wrt_ref, idx_ref, gate_ref):
    logits = lax.dot_general(wrt_ref[...], h32, (((1,), (1,)), ((), ())),
                             precision=lax.Precision.HIGHEST, preferred_element_type=F32)
    e_iota = lax.broadcasted_iota(jnp.int32, logits.shape, 0)
    m1 = jnp.max(logits, axis=0, keepdims=True)
    i1 = jnp.min(jnp.where(logits == m1, e_iota, N_EXPERTS), axis=0, keepdims=True)
    rest = jnp.where(e_iota == i1, -jnp.inf, logits)
    m2 = jnp.max(rest, axis=0, keepdims=True)
    i2 = jnp.min(jnp.where(rest == m2, e_iota, N_EXPERTS), axis=0, keepdims=True)
    e2 = jnp.exp(m2 - m1)
    den = 1.0 + e2
    idx_ref[0:1, :] = i1
    idx_ref[1:2, :] = i2
    gate_ref[0:1, :] = 1.0 / den
    gate_ref[1:2, :] = e2 / den


def _moe_prologue(x1, gf_ref, wrt_ref, x_out_ref, h_ref, idx_ref, gate_ref):
    x_out_ref[0] = x1
    h2 = _rms(x1, gf_ref[...])
    h_ref[0] = h2.astype(BF16)
    _route(h2, wrt_ref, idx_ref, gate_ref)


def _pool_ffn_kernel(xp_ref, xc_ref, xn_ref, gm_ref, wg_ref, sc_ref, gf_ref, w1_ref, w3_ref, w2_ref,
                     o_ref, *, ts, seq, fc):
    i = pl.program_id(1)
    xc = xc_ref[0]
    d = xc.shape[1]
    gd = d // len(POOL_WINDOWS)
    n_ext = ts + 2 * HALO
    xe = jnp.concatenate([xp_ref[0], xc, xn_ref[0]], axis=0)
    he = _rms(xe, gm_ref[...])
    pos = i * ts - HALO + lax.broadcasted_iota(jnp.int32, (n_ext, 1), 0)
    he = jnp.where((pos >= 0) & (pos < seq), he, 0.0)

    s = he + pltpu.roll(he, 1, 0)
    sums = [s]
    half = 1
    for _ in POOL_WINDOWS[1:]:
        s = s[:, gd:]
        s = pltpu.roll(s, half, 0) + pltpu.roll(s, n_ext - half, 0)
        sums.append(s)
        half *= 2

    tpos = i * ts + lax.broadcasted_iota(jnp.int32, (ts, 1), 0)
    ys = []
    for gi, w in enumerate(POOL_WINDOWS):
        cnt = jnp.minimum(tpos + w // 2, seq) - jnp.maximum(tpos - w // 2, 0)
        mean = sums[gi][HALO:HALO + ts, :gd] / cnt.astype(F32)
        pooled = mean - he[HALO:HALO + ts, gi * gd:(gi + 1) * gd]
        ys.append(_dot(pooled.astype(BF16), wg_ref[gi]))
    x1 = xc + jnp.concatenate(ys, axis=1) * sc_ref[...]
    hb = _rms(x1, gf_ref[...]).astype(BF16)
    o_ref[0] = x1 + _swiglu(hb, w1_ref, w3_ref, w2_ref, fc)


def _halo_specs(ts, seq, d):
    r = ts // HALO
    last = seq // HALO - 1
    return [
        pl.BlockSpec((1, HALO, d), lambda b, i: (b, jnp.maximum(i * r - 1, 0), 0)),
        pl.BlockSpec((1, ts, d), lambda b, i: (b, i, 0)),
        pl.BlockSpec((1, HALO, d), lambda b, i: (b, jnp.minimum((i + 1) * r, last), 0)),
    ]


def _full(shape):
    n = len(shape)
    return pl.BlockSpec(shape, lambda *_: (0,) * n)


def _pool_ffn_layer(x, g_mix, w_grp, scale, g_ffn, w1, w3, w2, ts):
    bsz, seq, d = x.shape
    f = w1.shape[1]
    kern = functools.partial(_pool_ffn_kernel, ts=ts, seq=seq, fc=256)
    return pl.pallas_call(
        kern,
        grid=(bsz, seq // ts),
        in_specs=_halo_specs(ts, seq, d) + [
            _full((1, d)), _full(w_grp.shape), _full((1, d)), _full((1, d)),
            _full((d, f)), _full((d, f)), _full((f, d))],
        out_specs=pl.BlockSpec((1, ts, d), lambda b, i: (b, i, 0)),
        out_shape=jax.ShapeDtypeStruct(x.shape, F32),
        compiler_params=_cparams("parallel", "parallel"),
        name="pool_ffn",
    )(x, x, x, g_mix.reshape(1, d), w_grp.astype(BF16), scale.reshape(1, d), g_ffn.reshape(1, d),
      w1.astype(BF16), w3.astype(BF16), w2.astype(BF16))


def _fft_stage1_kernel(x_ref, g_ref, f1_ref, o_ref, *, nseg):
    for j in range(nseg):
        h = _rms(x_ref[0, :, j, :], g_ref[...]).astype(BF16)
        o_ref[0, j] = _dot(f1_ref[...], h)


def _fft_stage2_kernel(br_ref, bi_ref, m_ref, o_ref, *, nk):
    n2 = FFT_N2
    for j in range(nk):
        rhs = jnp.concatenate([br_ref[0, :, j, :], bi_ref[0, :, j, :]], axis=0).astype(BF16)
        res = _dot(m_ref[j], rhs)
        o_ref[0, 0, :, j, :] = res[:n2]
        o_ref[0, 1, :, j, :] = res[n2:]


def _fourier_out_kernel(x_ref, ar_ref, ai_ref, cc_ref, sc_ref, wo_ref, gf_ref, wrt_ref,
                        xo_ref, h_ref, idx_ref, gate_ref, *, norm):
    d = x_ref.shape[2]
    gd = d // FOURIER_GROUPS
    fs = []
    for gi in range(FOURIER_GROUPS):
        sl = slice(gi * gd, (gi + 1) * gd)
        fs.append(_dot(ar_ref[0, 0, :, sl].astype(BF16), cc_ref[...])
                  + _dot(ai_ref[0, 0, :, sl].astype(BF16), sc_ref[...]))
    f = (jnp.concatenate(fs, axis=1) * norm).astype(BF16)
    x1 = x_ref[0] + _dot(f, wo_ref[...])
    _moe_prologue(x1, gf_ref, wrt_ref, xo_ref, h_ref, idx_ref, gate_ref)


def _dft_tables(seq, gd):
    n2 = FFT_N2
    n1 = seq // n2
    k = np.arange(n1)
    ang1 = 2.0 * np.pi * ((k[:, None] * k[None, :]) % n1) / n1
    f1 = np.concatenate([np.cos(ang1), -np.sin(ang1)], axis=0)
    c = np.arange(gd)
    angc = 2.0 * np.pi * ((c[:, None] * c[None, :]) % gd) / gd
    return (jnp.asarray(f1, BF16), jnp.asarray(np.cos(angc), BF16), jnp.asarray(np.sin(angc), BF16))


def _twiddled_dft(seq):
    n2 = FFT_N2
    n1 = seq // n2
    k1 = lax.broadcasted_iota(jnp.int32, (n1, n2, n2), 0)
    k2 = lax.broadcasted_iota(jnp.int32, (n1, n2, n2), 1)
    a = lax.broadcasted_iota(jnp.int32, (n1, n2, n2), 2)
    ang = ((a * (n1 * k2 + k1)) % seq).astype(F32) * (2.0 * math.pi / seq)
    mr = jnp.cos(ang)
    mi = -jnp.sin(ang)
    top = jnp.concatenate([mr, -mi], axis=2)
    bot = jnp.concatenate([mi, mr], axis=2)
    return jnp.concatenate([top, bot], axis=1).astype(BF16)


def _fourier_layer(x, g_mix, w_out, g_ffn, w_router, tm):
    bsz, seq, d = x.shape
    n2 = FFT_N2
    n1 = seq // n2
    gd = d // FOURIER_GROUPS
    f1, cc, sc = _dft_tables(seq, gd)
    mtab = _twiddled_dft(seq)
    nseg = 8
    b1 = pl.pallas_call(
        functools.partial(_fft_stage1_kernel, nseg=nseg),
        grid=(bsz, n2 // nseg),
        in_specs=[pl.BlockSpec((1, n1, nseg, d), lambda b, t: (b, 0, t, 0)), _full((1, d)), _full(f1.shape)],
        out_specs=pl.BlockSpec((1, nseg, 2 * n1, d), lambda b, t: (b, t, 0, 0)),
        out_shape=jax.ShapeDtypeStruct((bsz, n2, 2 * n1, d), F32),
        compiler_params=_cparams("parallel", "parallel"),
        name="fft_stage1",
    )(x.reshape(bsz, n1, n2, d), g_mix.reshape(1, d), f1)

    nk = 8
    a = pl.pallas_call(
        functools.partial(_fft_stage2_kernel, nk=nk),
        grid=(bsz, n1 // nk),
        in_specs=[pl.BlockSpec((1, n2, nk, d), lambda b, t: (b, 0, t, 0)),
                  pl.BlockSpec((1, n2, nk, d), lambda b, t: (b, 0, n1 // nk + t, 0)),
                  pl.BlockSpec((nk, 2 * n2, 2 * n2), lambda b, t: (t, 0, 0))],
        out_specs=pl.BlockSpec((1, 2, n2, nk, d), lambda b, t: (b, 0, 0, t, 0)),
        out_shape=jax.ShapeDtypeStruct((bsz, 2, n2, n1, d), F32),
        compiler_params=_cparams("parallel", "parallel"),
        name="fft_stage2",
    )(b1, b1, mtab)

    av = a.reshape(bsz, 2, seq, d)
    n_tok = bsz * seq
    nt = seq // tm
    outs = pl.pallas_call(
        functools.partial(_fourier_out_kernel, norm=1.0 / math.sqrt(seq * gd)),
        grid=(bsz, nt),
        in_specs=[pl.BlockSpec((1, tm, d), lambda b, i: (b, i, 0)),
                  pl.BlockSpec((1, 1, tm, d), lambda b, i: (b, 0, i, 0)),
                  pl.BlockSpec((1, 1, tm, d), lambda b, i: (b, 1, i, 0)),
                  _full((gd, gd)), _full((gd, gd)), _full((d, d)), _full((1, d)), _full((N_EXPERTS, d))],
        out_specs=[pl.BlockSpec((1, tm, d), lambda b, i: (b, i, 0)),
                   pl.BlockSpec((1, tm, d), lambda b, i: (b, i, 0)),
                   pl.BlockSpec((TOP_K, tm), lambda b, i: (0, b * nt + i)),
                   pl.BlockSpec((TOP_K, tm), lambda b, i: (0, b * nt + i))],
        out_shape=[jax.ShapeDtypeStruct(x.shape, F32), jax.ShapeDtypeStruct(x.shape, BF16),
                   jax.ShapeDtypeStruct((TOP_K, n_tok), jnp.int32),
                   jax.ShapeDtypeStruct((TOP_K, n_tok), F32)],
        compiler_params=_cparams("parallel", "parallel"),
        name="fourier_out",
    )(x, av, av, cc, sc, w_out.astype(BF16), g_ffn.reshape(1, d), w_router.T)
    return outs


def _moe_plan(idx, n_tok):
    chunk, rows = MOE_CHUNK, MOE_ROWS
    nc = n_tok // chunk
    e_ca = idx.T.reshape(nc, chunk * TOP_K)
    oh = (e_ca[:, :, None] == jnp.arange(N_EXPERTS, dtype=jnp.int32)).astype(jnp.int32)
    cs = jnp.cumsum(oh, axis=1)
    rank = jnp.sum(cs * oh, axis=2) - 1
    cnt = cs[:, -1, :]
    seg_len = (cnt + SEG_ALIGN - 1) // SEG_ALIGN * SEG_ALIGN
    loff = jnp.cumsum(seg_len, axis=1) - seg_len
    region = (jnp.sum(seg_len, axis=0) + rows - 1) // rows * rows
    pend = jnp.cumsum(region)
    goff = (pend - region)[None, :] + jnp.cumsum(seg_len, axis=0) - seg_len
    lrow = jnp.sum(loff[:, None, :] * oh, axis=2) + rank
    n_blocks = (n_tok * TOP_K + nc * N_EXPERTS * (SEG_ALIGN - 1)) // rows + N_EXPERTS
    blk = jnp.arange(n_blocks, dtype=jnp.int32)
    block_exp = jnp.minimum(jnp.searchsorted(pend, blk * rows, side='right'), N_EXPERTS - 1).astype(jnp.int32)
    return dict(
        n_blocks=n_blocks,
        seg_len=seg_len.reshape(-1).astype(jnp.int32), loff=loff.reshape(-1).astype(jnp.int32),
        goff=goff.reshape(-1).astype(jnp.int32),
        lrow_lanes=lrow.reshape(nc, chunk, TOP_K).transpose(0, 2, 1).astype(jnp.int32),
        lrow_cols=lrow.reshape(n_tok, TOP_K).astype(jnp.int32),
        n_used=(pend[-1] // rows).reshape(1).astype(jnp.int32), block_exp=block_exp)


def _segment_copies(c, len_ref, loff_ref, goff_ref, make, act):
    for e in range(N_EXPERTS):
        ln = len_ref[c * N_EXPERTS + e]
        lo = loff_ref[c * N_EXPERTS + e]
        go = goff_ref[c * N_EXPERTS + e]
        for size in SEG_PIECES:
            done = ln & (-2 * size)

            @pl.when((ln & size) != 0)
            def _():
                act(make(pl.multiple_of(lo + done, SEG_ALIGN), pl.multiple_of(go + done, SEG_ALIGN), size))


def _dispatch_kernel(len_ref, loff_ref, goff_ref, h_ref, lrow_ref, xb_in_ref, xb_ref, stage_ref, sem):
    del xb_in_ref
    c = pl.program_id(0)
    slot = c % 2
    n_local, chunk = stage_ref.shape[1], h_ref.shape[0]

    def make(slot_):
        def _make(lr, gr, size):
            return pltpu.make_async_copy(stage_ref.at[slot_, pl.ds(lr, size)], xb_ref.at[pl.ds(gr, size)],
                                         sem.at[slot_])
        return _make

    r_iota = lax.broadcasted_iota(jnp.int32, (n_local, chunk), 0)
    sel = (lrow_ref[0, 0:1, :] == r_iota) | (lrow_ref[0, 1:2, :] == r_iota)
    stage_ref[slot] = _dot(jnp.where(sel, 1.0, 0.0).astype(BF16), h_ref[...]).astype(BF16)

    @pl.when(c > 0)
    def _():
        _segment_copies(c - 1, len_ref, loff_ref, goff_ref, make(1 - slot), lambda cp: cp.wait())

    _segment_copies(c, len_ref, loff_ref, goff_ref, make(slot), lambda cp: cp.start())

    @pl.when(c == pl.num_programs(0) - 1)
    def _():
        _segment_copies(c, len_ref, loff_ref, goff_ref, make(slot), lambda cp: cp.wait())


def _expert_kernel(be_ref, nu_ref, xb_ref, w1_ref, w3_ref, w2_ref, y_ref, acc_ref, *, nf, fc):
    b = pl.program_id(0)
    f = pl.program_id(1)

    @pl.when(b < nu_ref[0])
    def _():
        y = _swiglu(xb_ref[...], w1_ref.at[0], w3_ref.at[0], w2_ref.at[0], fc)

        @pl.when(f == 0)
        def _():
            acc_ref[...] = y

        @pl.when(f > 0)
        def _():
            acc_ref[...] += y

        @pl.when(f == nf - 1)
        def _():
            y_ref[...] = acc_ref[...].astype(BF16)

    @pl.when((b >= nu_ref[0]) & (f == nf - 1))
    def _():
        y_ref[...] = jnp.zeros_like(y_ref)


def _combine_kernel(len_ref, loff_ref, goff_ref, y_ref, lcol_ref, gcol_ref, x_ref, g_ref, o_ref, ybuf_ref, sem,
                    *, final_norm):
    c = pl.program_id(0)
    nc = pl.num_programs(0)
    slot = c % 2
    chunk, n_local = x_ref.shape[0], ybuf_ref.shape[1]

    def make(slot_):
        def _make(lr, gr, size):
            return pltpu.make_async_copy(y_ref.at[pl.ds(gr, size)], ybuf_ref.at[slot_, pl.ds(lr, size)],
                                         sem.at[slot_])
        return _make

    @pl.when(c == 0)
    def _():
        ybuf_ref[...] = jnp.zeros_like(ybuf_ref)
        _segment_copies(c, len_ref, loff_ref, goff_ref, make(slot), lambda cp: cp.start())

    @pl.when(c + 1 < nc)
    def _():
        _segment_copies(c + 1, len_ref, loff_ref, goff_ref, make(1 - slot), lambda cp: cp.start())

    _segment_copies(c, len_ref, loff_ref, goff_ref, make(slot), lambda cp: cp.wait())

    yl = ybuf_ref[slot]
    r_iota = lax.broadcasted_iota(jnp.int32, (chunk, n_local), 1)
    xo = x_ref[...]
    for k in range(TOP_K):
        sel = lcol_ref[:, k:k + 1] == r_iota
        xo = xo + gcol_ref[:, k:k + 1] * _dot(jnp.where(sel, 1.0, 0.0).astype(BF16), yl)
    o_ref[...] = _rms(xo, g_ref[...]) if final_norm else xo


def _moe_layer(x, h, idx, gates, w1, w3, w2, g_final=None):
    bsz, seq, d = x.shape
    n_tok = bsz * seq
    f = w1.shape[2]
    rows, chunk = MOE_ROWS, MOE_CHUNK
    nc = n_tok // chunk
    plan = _moe_plan(idx, n_tok)
    n_blocks = plan['n_blocks']
    n_rows = n_blocks * rows
    seg = (plan['seg_len'], plan['loff'], plan['goff'])

    xb = pl.pallas_call(
        _dispatch_kernel,
        grid_spec=pltpu.PrefetchScalarGridSpec(
            num_scalar_prefetch=3, grid=(nc,),
            in_specs=[pl.BlockSpec((chunk, d), lambda c, *_: (c, 0)),
                      pl.BlockSpec((1, TOP_K, chunk), lambda c, *_: (c, 0, 0)),
                      pl.BlockSpec(memory_space=pl.ANY)],
            out_specs=pl.BlockSpec(memory_space=pl.ANY),
            scratch_shapes=[pltpu.VMEM((2, LOCAL_ROWS, d), BF16), pltpu.SemaphoreType.DMA((2,))]),
        out_shape=jax.ShapeDtypeStruct((n_rows, d), BF16),
        input_output_aliases={5: 0},
        compiler_params=_cparams("arbitrary"),
        name="moe_dispatch",
    )(*seg, h.reshape(n_tok, d), plan['lrow_lanes'], jnp.zeros((n_rows, d), BF16))

    nf = 2
    fblk = f // nf

    def blk_idx(b, nu):
        return jnp.minimum(b, nu[0] - 1)

    def f_idx(b, fi, nu):
        return jnp.where(b < nu[0], fi, nf - 1)

    y = pl.pallas_call(
        functools.partial(_expert_kernel, nf=nf, fc=256),
        grid_spec=pltpu.PrefetchScalarGridSpec(
            num_scalar_prefetch=2, grid=(n_blocks, nf),
            in_specs=[pl.BlockSpec((rows, d), lambda b, fi, be, nu: (blk_idx(b, nu), 0)),
                      pl.BlockSpec((1, d, fblk), lambda b, fi, be, nu: (be[blk_idx(b, nu)], 0, f_idx(b, fi, nu))),
                      pl.BlockSpec((1, d, fblk), lambda b, fi, be, nu: (be[blk_idx(b, nu)], 0, f_idx(b, fi, nu))),
                      pl.BlockSpec((1, fblk, d), lambda b, fi, be, nu: (be[blk_idx(b, nu)], f_idx(b, fi, nu), 0))],
            out_specs=pl.BlockSpec((rows, d), lambda b, fi, be, nu: (b, 0)),
            scratch_shapes=[pltpu.VMEM((rows, d), F32)]),
        out_shape=jax.ShapeDtypeStruct((n_rows, d), BF16),
        compiler_params=_cparams("arbitrary", "arbitrary"),
        name="moe_experts",
    )(plan['block_exp'], plan['n_used'], xb, w1.astype(BF16), w3.astype(BF16), w2.astype(BF16))

    final_norm = g_final is not None
    g = (g_final if final_norm else jnp.ones((d,), F32)).reshape(1, d)
    out = pl.pallas_call(
        functools.partial(_combine_kernel, final_norm=final_norm),
        grid_spec=pltpu.PrefetchScalarGridSpec(
            num_scalar_prefetch=3, grid=(nc,),
            in_specs=[pl.BlockSpec(memory_space=pl.ANY),
                      pl.BlockSpec((chunk, TOP_K), lambda c, *_: (c, 0)),
                      pl.BlockSpec((chunk, TOP_K), lambda c, *_: (c, 0)),
                      pl.BlockSpec((chunk, d), lambda c, *_: (c, 0)),
                      pl.BlockSpec((1, d), lambda c, *_: (0, 0))],
            out_specs=pl.BlockSpec((chunk, d), lambda c, *_: (c, 0)),
            scratch_shapes=[pltpu.VMEM((2, LOCAL_ROWS, d), BF16), pltpu.SemaphoreType.DMA((2,))]),
        out_shape=jax.ShapeDtypeStruct((n_tok, d), F32),
        compiler_params=_cparams("arbitrary"),
        name="moe_combine",
    )(*seg, y, plan['lrow_cols'], gates.T, x.reshape(n_tok, d), g)
    return out.reshape(bsz, seq, d)


def _qkv_kernel(x_ref, g_ref, w_ref, *rest, nc):
    o_refs, h_ref = rest[:-1], rest[-1]
    tm = x_ref.shape[1]
    gw = 3 * ATTN_WIDTH
    h = _rms(x_ref[0], g_ref[...])
    nlc = h.shape[1] // LANES
    for c in range(nlc):
        h_ref[c] = h[:, c * LANES:(c + 1) * LANES]
    for gi, (_, dil) in enumerate(ATTN_GROUPS):
        n = tm // dil
        hb = h if dil == 1 else jnp.concatenate(
            [jnp.concatenate([h_ref[c, pl.ds(r, n, stride=dil), :] for c in range(nlc)], axis=1)
             for r in range(dil)], axis=0)
        hb = hb.astype(BF16)
        for c in range(gi * gw, (gi + 1) * gw, nc):
            res = _dot(hb, w_ref[:, c:c + nc]).astype(BF16)
            for r in range(dil):
                o_refs[gi][0, :, r * gw + c - gi * gw:r * gw + c - gi * gw + nc] = res[r * n:(r + 1) * n]


def _attn_kernel(q_ref, kp_ref, kc_ref, kn_ref, vp_ref, vc_ref, vn_ref, o_ref, lse_ref,
                 *, tq, seg, dil, radius, slopes):
    j = pl.program_id(2)
    hd = ATTN_HEAD_DIM
    tk = tq + 2 * radius
    q = q_ref[0] * jnp.asarray(1.0 / math.sqrt(hd), BF16)
    k = jnp.concatenate([kp_ref[0], kc_ref[0], kn_ref[0]], axis=0)
    v = jnp.concatenate([vp_ref[0], vc_ref[0], vn_ref[0]], axis=0)
    qi = lax.broadcasted_iota(jnp.int32, (tq, tk), 0)
    kj = lax.broadcasted_iota(jnp.int32, (tq, tk), 1)
    arel = jnp.abs(kj - radius - qi)
    kpos = j * tq - radius + kj
    valid = (arel <= radius) & (kpos >= 0) & (kpos < seg)
    dist = (dil * arel).astype(F32)
    low = lax.broadcasted_iota(jnp.int32, (1, 2 * hd), 1) < hd
    for hp in range(ATTN_HEADS // 2):
        sl = slice(hp * 2 * hd, (hp + 1) * 2 * hd)
        qp, kp, vp = q[:, sl], k[:, sl], v[:, sl]
        o_pair, l_pair = None, None
        for sub in range(2):
            mine = low if sub == 0 else jnp.logical_not(low)
            qm = jnp.where(mine, qp, jnp.zeros_like(qp))
            s = lax.dot_general(qm, kp, (((1,), (1,)), ((), ())), preferred_element_type=F32)
            s = jnp.where(valid, s - slopes[2 * hp + sub] * dist, MASK_VALUE)
            m = jnp.max(s, axis=-1, keepdims=True)
            p = jnp.exp(s - m)
            den = jnp.sum(p, axis=-1, keepdims=True)
            o = _dot(p.astype(BF16), vp) / den
            l = jnp.broadcast_to(m + jnp.log(den), o.shape)
            o_pair = o if sub == 0 else jnp.where(low, o_pair, o)
            l_pair = l if sub == 0 else jnp.where(low, l_pair, l)
        o_ref[0, :, sl] = o_pair.astype(BF16)
        lse_ref[0, :, sl] = l_pair


def _attn_out_ffn_kernel(x_ref, o0_ref, o1_ref, o2_ref, l0_ref, l1_ref, l2_ref, wo_ref, gf_ref,
                         w1_ref, w3_ref, w2_ref, out_ref, so_ref, sl_ref, *, fc):
    aw = ATTN_WIDTH
    ls, os_ = [], []
    for gi, (o_ref, l_ref) in enumerate(((o0_ref, l0_ref), (o1_ref, l1_ref), (o2_ref, l2_ref))):
        dil = ATTN_GROUPS[gi][1]
        if dil == 1:
            os_.append(o_ref[0].astype(F32))
            ls.append(l_ref[0])
            continue
        n = o_ref.shape[1]
        nlc = aw // LANES
        for r in range(dil):
            ov = o_ref[0, :, r * aw:(r + 1) * aw].astype(F32)
            lv = l_ref[0, :, r * aw:(r + 1) * aw]
            for c in range(nlc):
                so_ref[gi - 1, c, pl.ds(r, n, stride=dil), :] = ov[:, c * LANES:(c + 1) * LANES]
                sl_ref[gi - 1, c, pl.ds(r, n, stride=dil), :] = lv[:, c * LANES:(c + 1) * LANES]
        os_.append(jnp.concatenate([so_ref[gi - 1, c] for c in range(nlc)], axis=1))
        ls.append(jnp.concatenate([sl_ref[gi - 1, c] for c in range(nlc)], axis=1))
    m = jnp.maximum(jnp.maximum(ls[0], ls[1]), ls[2])
    es = [jnp.exp(l - m) for l in ls]
    den = es[0] + es[1] + es[2]
    o = (es[0] * os_[0] + es[1] * os_[1] + es[2] * os_[2]) / den
    x1 = x_ref[0] + _dot(o.astype(BF16), wo_ref[...])
    hb = _rms(x1, gf_ref[...]).astype(BF16)
    out_ref[0] = x1 + _swiglu(hb, w1_ref, w3_ref, w2_ref, fc)


def _alibi_slopes():
    n = len(ATTN_GROUPS) * ATTN_HEADS
    s = np.float32(2.0) ** (np.float32(-8.0) * np.arange(1, n + 1, dtype=np.float32) / np.float32(n))
    return [float(v) for v in s]


def _attention_ffn_layer(x, g_mix, w_qkv, w_out, g_ffn, w1, w3, w2, tm):
    bsz, seq, d = x.shape
    nqkv = w_qkv.shape[1]
    aw = ATTN_WIDTH
    gw = 3 * aw
    assert nqkv == len(ATTN_GROUPS) * gw
    qkvs = pl.pallas_call(
        functools.partial(_qkv_kernel, nc=aw),
        grid=(bsz, seq // tm),
        in_specs=[pl.BlockSpec((1, tm, d), lambda b, i: (b, i, 0)), _full((1, d)), _full((d, nqkv))],
        out_specs=[pl.BlockSpec((1, tm // dil, dil * gw), lambda b, i: (b, i, 0)) for _, dil in ATTN_GROUPS],
        out_shape=[jax.ShapeDtypeStruct((bsz, seq // dil, dil * gw), BF16) for _, dil in ATTN_GROUPS],
        scratch_shapes=[pltpu.VMEM((d // LANES, tm, LANES), F32)],
        compiler_params=_cparams("parallel", "parallel"),
        name="qkv_proj",
    )(x, g_mix.reshape(1, d), w_qkv.astype(BF16))

    slopes = _alibi_slopes()
    outs, lses = [], []
    for gi, (window, dil) in enumerate(ATTN_GROUPS):
        radius = window // (2 * dil)
        seg = seq // dil
        tq = min(256, seg)
        hr = tq // radius
        last = seg // radius - 1
        view = qkvs[gi]

        def col(part, r):
            return r * 3 + part

        def cur(part):
            return pl.BlockSpec((1, tq, aw), lambda b, r, j, part=part: (b, j, col(part, r)))

        def prev(part):
            return pl.BlockSpec((1, radius, aw),
                                lambda b, r, j, part=part: (b, jnp.maximum(j * hr - 1, 0), col(part, r)))

        def nxt(part):
            return pl.BlockSpec((1, radius, aw),
                                lambda b, r, j, part=part: (b, jnp.minimum((j + 1) * hr, last), col(part, r)))

        o, lse = pl.pallas_call(
            functools.partial(_attn_kernel, tq=tq, seg=seg, dil=dil, radius=radius,
                              slopes=slopes[gi * ATTN_HEADS:(gi + 1) * ATTN_HEADS]),
            grid=(bsz, dil, seg // tq),
            in_specs=[cur(0), prev(1), cur(1), nxt(1), prev(2), cur(2), nxt(2)],
            out_specs=[pl.BlockSpec((1, tq, aw), lambda b, r, j: (b, j, r)),
                       pl.BlockSpec((1, tq, aw), lambda b, r, j: (b, j, r))],
            out_shape=[jax.ShapeDtypeStruct((bsz, seg, dil * aw), BF16),
                       jax.ShapeDtypeStruct((bsz, seg, dil * aw), F32)],
            compiler_params=_cparams("parallel", "parallel", "parallel"),
            name=f"dilated_attn_{dil}",
        )(view, view, view, view, view, view, view)
        outs.append(o)
        lses.append(lse)

    f = w1.shape[1]
    row = lambda width: pl.BlockSpec((1, tm, width), lambda b, i: (b, i, 0))
    dilated = [pl.BlockSpec((1, tm // dil, dil * aw), lambda b, i: (b, i, 0)) for _, dil in ATTN_GROUPS]
    return pl.pallas_call(
        functools.partial(_attn_out_ffn_kernel, fc=256),
        grid=(bsz, seq // tm),
        in_specs=[row(d)] + dilated + dilated + [_full((aw, d)), _full((1, d)),
                                                 _full((d, f)), _full((d, f)), _full((f, d))],
        out_specs=row(d),
        out_shape=jax.ShapeDtypeStruct(x.shape, F32),
        scratch_shapes=[pltpu.VMEM((len(ATTN_GROUPS) - 1, aw // LANES, tm, LANES), F32),
                        pltpu.VMEM((len(ATTN_GROUPS) - 1, aw // LANES, tm, LANES), F32)],
        compiler_params=_cparams("parallel", "parallel"),
        name="attn_out_ffn",
    )(x, *outs, *lses, w_out.astype(BF16), g_ffn.reshape(1, d), w1.astype(BF16), w3.astype(BF16),
      w2.astype(BF16))


def _conv_glu_kernel(x_ref, g_ref, w_ref, u_ref):
    d = x_ref.shape[2]
    hb = _rms(x_ref[0], g_ref[...]).astype(BF16)
    a = _dot(hb, w_ref[:, :d])
    b = _dot(hb, w_ref[:, d:])
    u_ref[0] = a * jax.nn.sigmoid(b)


def _conv_out_kernel(x_ref, up_ref, uc_ref, un_ref, wdw_ref, bdw_ref, lg_ref, lb_ref, w2_ref, gf_ref, wrt_ref,
                     xo_ref, h_ref, idx_ref, gate_ref, ext_ref, *, ts, seq):
    i = pl.program_id(1)
    n_ext = ts + 2 * HALO
    ue = jnp.concatenate([up_ref[0], uc_ref[0], un_ref[0]], axis=0)
    pos = i * ts - HALO + lax.broadcasted_iota(jnp.int32, (n_ext, 1), 0)
    ext_ref[...] = jnp.where((pos >= 0) & (pos < seq), ue, 0.0)
    half = CONV_WIDTH // 2
    acc = None
    for k in range(CONV_WIDTH):
        start = HALO + k - half
        term = ext_ref[start:start + ts, :] * wdw_ref[k:k + 1, :]
        acc = term if acc is None else acc + term
    u = acc + bdw_ref[...]
    mu = jnp.mean(u, axis=-1, keepdims=True)
    uc = u - mu
    var = jnp.mean(uc * uc, axis=-1, keepdims=True)
    z = uc * lax.rsqrt(var + LN_EPS) * lg_ref[...] + lb_ref[...]
    z = (z * jax.nn.sigmoid(z)).astype(BF16)
    x1 = x_ref[0] + _dot(z, w2_ref[...])
    _moe_prologue(x1, gf_ref, wrt_ref, xo_ref, h_ref, idx_ref, gate_ref)


def _conv_layer(x, g_mix, w_pw1, w_dw, b_dw, ln_g, ln_b, w_pw2, g_ffn, w_router, tm, ts):
    bsz, seq, d = x.shape
    n_tok = bsz * seq
    u = pl.pallas_call(
        _conv_glu_kernel,
        grid=(bsz, seq // tm),
        in_specs=[pl.BlockSpec((1, tm, d), lambda b, i: (b, i, 0)), _full((1, d)), _full((d, 2 * d))],
        out_specs=pl.BlockSpec((1, tm, d), lambda b, i: (b, i, 0)),
        out_shape=jax.ShapeDtypeStruct(x.shape, F32),
        compiler_params=_cparams("parallel", "parallel"),
        name="conv_glu",
    )(x, g_mix.reshape(1, d), w_pw1.astype(BF16))

    nt = seq // ts
    vec = lambda a: a.reshape(1, d)
    return pl.pallas_call(
        functools.partial(_conv_out_kernel, ts=ts, seq=seq),
        grid=(bsz, nt),
        in_specs=[pl.BlockSpec((1, ts, d), lambda b, i: (b, i, 0))] + _halo_specs(ts, seq, d) + [
            _full((CONV_WIDTH, d)), _full((1, d)), _full((1, d)), _full((1, d)), _full((d, d)),
            _full((1, d)), _full((N_EXPERTS, d))],
        out_specs=[pl.BlockSpec((1, ts, d), lambda b, i: (b, i, 0)),
                   pl.BlockSpec((1, ts, d), lambda b, i: (b, i, 0)),
                   pl.BlockSpec((TOP_K, ts), lambda b, i: (0, b * nt + i)),
                   pl.BlockSpec((TOP_K, ts), lambda b, i: (0, b * nt + i))],
        out_shape=[jax.ShapeDtypeStruct(x.shape, F32), jax.ShapeDtypeStruct(x.shape, BF16),
                   jax.ShapeDtypeStruct((TOP_K, n_tok), jnp.int32),
                   jax.ShapeDtypeStruct((TOP_K, n_tok), F32)],
        scratch_shapes=[pltpu.VMEM((ts + 2 * HALO, d), F32)],
        compiler_params=_cparams("parallel", "parallel"),
        name="conv_out",
    )(x, u, u, u, w_dw, vec(b_dw), vec(ln_g), vec(ln_b), w_pw2.astype(BF16), vec(g_ffn), w_router.T)


def kernel(x, g_mix, g_ffn, g_final, a_w_grp, a_scale, b_w_out, c_w_qkv, c_w_out, d_w_pw1, d_w_dw, d_b_dw,
           d_ln_g, d_ln_b, d_w_pw2, ffn_w1, ffn_w3, ffn_w2, moe_router, moe_w1, moe_w3, moe_w2):
    assert g_mix.shape[0] == 4, "one layer of each mixer kind"
    tm = min(512, x.shape[1])
    x = _pool_ffn_layer(x, g_mix[0], a_w_grp[0], a_scale[0], g_ffn[0], ffn_w1[0], ffn_w3[0], ffn_w2[0], tm)
    x, h, idx, gates = _fourier_layer(x, g_mix[1], b_w_out[0], g_ffn[1], moe_router[0], tm)
    x = _moe_layer(x, h, idx, gates, moe_w1[0], moe_w3[0], moe_w2[0])
    x = _attention_ffn_layer(x, g_mix[2], c_w_qkv[0], c_w_out[0], g_ffn[2], ffn_w1[1], ffn_w3[1], ffn_w2[1], tm)
    x, h, idx, gates = _conv_layer(x, g_mix[3], d_w_pw1[0], d_w_dw[0], d_b_dw[0], d_ln_g[0], d_ln_b[0],
                                   d_w_pw2[0], g_ffn[3], moe_router[1], tm, min(256, x.shape[1]))
    return _moe_layer(x, h, idx, gates, moe_w1[1], moe_w3[1], moe_w2[1], g_final=g_final)
```

```python
import functools
import math

import numpy as np
import jax
import jax.numpy as jnp
from jax import lax
from jax.experimental import pallas as pl
from jax.experimental.pallas import tpu as pltpu

F32 = jnp.float32
BF16 = jnp.bfloat16

RMS_EPS = 1e-6
LN_EPS = 1e-5
MASK_VALUE = -1e30
POOL_WINDOWS = (2, 4, 8, 16)
FOURIER_GROUPS = 4
ATTN_GROUPS = ((128, 1), (512, 4), (2048, 16))
ATTN_HEADS = 8
ATTN_HEAD_DIM = 64
ATTN_WIDTH = ATTN_HEADS * ATTN_HEAD_DIM
CONV_WIDTH = 31
N_EXPERTS = 8
TOP_K = 2

LANES = 128
SUBLANES = 8
HALO = 16
FFT_N2 = 128
MOE_ROWS = 512
MOE_CHUNK = 512
SEG_ALIGN = 16
SEG_PIECES = (512, 256, 128, 64, 32, 16)
LOCAL_ROWS = 1152
VMEM_LIMIT_BYTES = 56 * 1024 * 1024


def _cparams(*sem):
    return pltpu.CompilerParams(dimension_semantics=sem, vmem_limit_bytes=VMEM_LIMIT_BYTES)


def _dot(a, b):
    return jnp.dot(a, b, preferred_element_type=F32)


def _rms(x, g):
    ms = jnp.mean(x * x, axis=-1, keepdims=True)
    return x * lax.rsqrt(ms + RMS_EPS) * g


def _swiglu(hb, w1_ref, w3_ref, w2_ref, fc):
    f = w1_ref.shape[-1]
    acc = None
    for c in range(0, f, fc):
        a = _dot(hb, w1_ref[:, c:c + fc])
        b = _dot(hb, w3_ref[:, c:c + fc])
        g = (a * jax.nn.sigmoid(a) * b).astype(BF16)
        y = _dot(g, w2_ref[c:c + fc, :])
        acc = y if acc is None else acc + y
    return acc


def _route(h32, wrt_ref, idx_ref, gate_ref):
    logits = lax.dot_general(wrt_ref[...], h32, (((1,), (1,)), ((), ())),
                             precision=lax.Precision.HIGHEST, preferred_element_type=F32)
    e_iota = lax.broadcasted_iota(jnp.int32, logits.shape, 0)
    m1 = jnp.max(logits, axis=0, keepdims=True)
    i1 = jnp.min(jnp.where(logits == m1, e_iota, N_EXPERTS), axis=0, keepdims=True)
    rest = jnp.where(e_iota == i1, -jnp.inf, logits)
    m2 = jnp.max(rest, axis=0, keepdims=True)
    i2 = jnp.min(jnp.where(rest == m2, e_iota, N_EXPERTS), axis=0, keepdims=True)
    e2 = jnp.exp(m2 - m1)
    den = 1.0 + e2
    idx_ref[0:1, :] = i1
    idx_ref[1:2, :] = i2
    gate_ref[0:1, :] = 1.0 / den
    gate_ref[1:2, :] = e2 / den


def _moe_prologue(x1, gf_ref, wrt_ref, x_out_ref, h_ref, idx_ref, gate_ref):
    x_out_ref[0] = x1
    h2 = _rms(x1, gf_ref[...])
    h_ref[0] = h2.astype(BF16)
    _route(h2, wrt_ref, idx_ref, gate_ref)


def _pool_ffn_kernel(xp_ref, xc_ref, xn_ref, gm_ref, wg_ref, sc_ref, gf_ref, w1_ref, w3_ref, w2_ref,
                     o_ref, *, ts, seq, fc):
    i = pl.program_id(1)
    xc = xc_ref[0]
    d = xc.shape[1]
    gd = d // len(POOL_WINDOWS)
    n_ext = ts + 2 * HALO
    xe = jnp.concatenate([xp_ref[0], xc, xn_ref[0]], axis=0)
    he = _rms(xe, gm_ref[...])
    pos = i * ts - HALO + lax.broadcasted_iota(jnp.int32, (n_ext, 1), 0)
    he = jnp.where((pos >= 0) & (pos < seq), he, 0.0)

    s = he + pltpu.roll(he, 1, 0)
    sums = [s]
    half = 1
    for _ in POOL_WINDOWS[1:]:
        s = s[:, gd:]
        s = pltpu.roll(s, half, 0) + pltpu.roll(s, n_ext - half, 0)
        sums.append(s)
        half *= 2

    tpos = i * ts + lax.broadcasted_iota(jnp.int32, (ts, 1), 0)
    ys = []
    for gi, w in enumerate(POOL_WINDOWS):
        cnt = jnp.minimum(tpos + w // 2, seq) - jnp.maximum(tpos - w // 2, 0)
        mean = sums[gi][HALO:HALO + ts, :gd] / cnt.astype(F32)
        pooled = mean - he[HALO:HALO + ts, gi * gd:(gi + 1) * gd]
        ys.append(_dot(pooled.astype(BF16), wg_ref[gi]))
    x1 = xc + jnp.concatenate(ys, axis=1) * sc_ref[...]
    hb = _rms(x1, gf_ref[...]).astype(BF16)
    o_ref[0] = x1 + _swiglu(hb, w1_ref.at[0], w3_ref.at[0], w2_ref.at[0], fc)


def _halo_specs(ts, seq, d):
    r = ts // HALO
    last = seq // HALO - 1
    return [
        pl.BlockSpec((1, HALO, d), lambda b, i: (b, jnp.maximum(i * r - 1, 0), 0)),
        pl.BlockSpec((1, ts, d), lambda b, i: (b, i, 0)),
        pl.BlockSpec((1, HALO, d), lambda b, i: (b, jnp.minimum((i + 1) * r, last), 0)),
    ]


def _full(shape):
    n = len(shape)
    return pl.BlockSpec(shape, lambda *_: (0,) * n)


def _stacked(shape, layer):
    n = len(shape)
    return pl.BlockSpec((1,) + tuple(shape), lambda *_: (layer,) + (0,) * n)


def _pool_ffn_layer(x, g_mix, w_grp, scale, g_ffn, w1, w3, w2, layer, ts):
    bsz, seq, d = x.shape
    f = w1.shape[2]
    kern = functools.partial(_pool_ffn_kernel, ts=ts, seq=seq, fc=256)
    return pl.pallas_call(
        kern,
        grid=(bsz, seq // ts),
        in_specs=_halo_specs(ts, seq, d) + [
            _full((1, d)), _full(w_grp.shape), _full((1, d)), _full((1, d)),
            _stacked((d, f), layer), _stacked((d, f), layer), _stacked((f, d), layer)],
        out_specs=pl.BlockSpec((1, ts, d), lambda b, i: (b, i, 0)),
        out_shape=jax.ShapeDtypeStruct(x.shape, F32),
        compiler_params=_cparams("parallel", "parallel"),
        name="pool_ffn",
    )(x, x, x, g_mix.reshape(1, d), w_grp.astype(BF16), scale.reshape(1, d), g_ffn.reshape(1, d), w1, w3, w2)


def _fft_stage1_kernel(x_ref, g_ref, f1_ref, o_ref, *, nseg):
    for j in range(nseg):
        h = _rms(x_ref[0, :, j, :], g_ref[...]).astype(BF16)
        o_ref[0, j] = _dot(f1_ref[...], h)


def _fft_stage2_kernel(br_ref, bi_ref, m_ref, o_ref, *, nk):
    n2 = FFT_N2
    for j in range(nk):
        rhs = jnp.concatenate([br_ref[0, :, j, :], bi_ref[0, :, j, :]], axis=0).astype(BF16)
        res = _dot(m_ref[j], rhs)
        o_ref[0, 0, :, j, :] = res[:n2]
        o_ref[0, 1, :, j, :] = res[n2:]


def _fourier_out_kernel(x_ref, ar_ref, ai_ref, cc_ref, sc_ref, wo_ref, gf_ref, wrt_ref,
                        xo_ref, h_ref, idx_ref, gate_ref, *, norm):
    d = x_ref.shape[2]
    gd = d // FOURIER_GROUPS
    fs = []
    for gi in range(FOURIER_GROUPS):
        sl = slice(gi * gd, (gi + 1) * gd)
        fs.append(_dot(ar_ref[0, 0, :, sl].astype(BF16), cc_ref[...])
                  + _dot(ai_ref[0, 0, :, sl].astype(BF16), sc_ref[...]))
    f = (jnp.concatenate(fs, axis=1) * norm).astype(BF16)
    x1 = x_ref[0] + _dot(f, wo_ref[...])
    _moe_prologue(x1, gf_ref, wrt_ref, xo_ref, h_ref, idx_ref, gate_ref)


def _dft_tables(seq, gd):
    n2 = FFT_N2
    n1 = seq // n2
    k = np.arange(n1)
    ang1 = 2.0 * np.pi * ((k[:, None] * k[None, :]) % n1) / n1
    f1 = np.concatenate([np.cos(ang1), -np.sin(ang1)], axis=0)
    c = np.arange(gd)
    angc = 2.0 * np.pi * ((c[:, None] * c[None, :]) % gd) / gd
    return (jnp.asarray(f1, BF16), jnp.asarray(np.cos(angc), BF16), jnp.asarray(np.sin(angc), BF16))


def _twiddled_dft(seq):
    n2 = FFT_N2
    n1 = seq // n2
    k1 = lax.broadcasted_iota(jnp.int32, (n1, n2, n2), 0)
    k2 = lax.broadcasted_iota(jnp.int32, (n1, n2, n2), 1)
    a = lax.broadcasted_iota(jnp.int32, (n1, n2, n2), 2)
    ang = ((a * (n1 * k2 + k1)) % seq).astype(F32) * (2.0 * math.pi / seq)
    mr = jnp.cos(ang)
    mi = -jnp.sin(ang)
    top = jnp.concatenate([mr, -mi], axis=2)
    bot = jnp.concatenate([mi, mr], axis=2)
    return jnp.concatenate([top, bot], axis=1).astype(BF16)


def _fourier_layer(x, g_mix, w_out, g_ffn, w_router, tm):
    bsz, seq, d = x.shape
    n2 = FFT_N2
    n1 = seq // n2
    gd = d // FOURIER_GROUPS
    f1, cc, sc = _dft_tables(seq, gd)
    mtab = _twiddled_dft(seq)
    nseg = 8
    b1 = pl.pallas_call(
        functools.partial(_fft_stage1_kernel, nseg=nseg),
        grid=(bsz, n2 // nseg),
        in_specs=[pl.BlockSpec((1, n1, nseg, d), lambda b, t: (b, 0, t, 0)), _full((1, d)), _full(f1.shape)],
        out_specs=pl.BlockSpec((1, nseg, 2 * n1, d), lambda b, t: (b, t, 0, 0)),
        out_shape=jax.ShapeDtypeStruct((bsz, n2, 2 * n1, d), F32),
        compiler_params=_cparams("parallel", "parallel"),
        name="fft_stage1",
    )(x.reshape(bsz, n1, n2, d), g_mix.reshape(1, d), f1)

    nk = 8
    a = pl.pallas_call(
        functools.partial(_fft_stage2_kernel, nk=nk),
        grid=(bsz, n1 // nk),
        in_specs=[pl.BlockSpec((1, n2, nk, d), lambda b, t: (b, 0, t, 0)),
                  pl.BlockSpec((1, n2, nk, d), lambda b, t: (b, 0, n1 // nk + t, 0)),
                  pl.BlockSpec((nk, 2 * n2, 2 * n2), lambda b, t: (t, 0, 0))],
        out_specs=pl.BlockSpec((1, 2, n2, nk, d), lambda b, t: (b, 0, 0, t, 0)),
        out_shape=jax.ShapeDtypeStruct((bsz, 2, n2, n1, d), F32),
        compiler_params=_cparams("parallel", "parallel"),
        name="fft_stage2",
    )(b1, b1, mtab)

    av = a.reshape(bsz, 2, seq, d)
    n_tok = bsz * seq
    nt = seq // tm
    outs = pl.pallas_call(
        functools.partial(_fourier_out_kernel, norm=1.0 / math.sqrt(seq * gd)),
        grid=(bsz, nt),
        in_specs=[pl.BlockSpec((1, tm, d), lambda b, i: (b, i, 0)),
                  pl.BlockSpec((1, 1, tm, d), lambda b, i: (b, 0, i, 0)),
                  pl.BlockSpec((1, 1, tm, d), lambda b, i: (b, 1, i, 0)),
                  _full((gd, gd)), _full((gd, gd)), _full((d, d)), _full((1, d)), _full((N_EXPERTS, d))],
        out_specs=[pl.BlockSpec((1, tm, d), lambda b, i: (b, i, 0)),
                   pl.BlockSpec((1, tm, d), lambda b, i: (b, i, 0)),
                   pl.BlockSpec((TOP_K, tm), lambda b, i: (0, b * nt + i)),
                   pl.BlockSpec((TOP_K, tm), lambda b, i: (0, b * nt + i))],
        out_shape=[jax.ShapeDtypeStruct(x.shape, F32), jax.ShapeDtypeStruct(x.shape, BF16),
                   jax.ShapeDtypeStruct((TOP_K, n_tok), jnp.int32),
                   jax.ShapeDtypeStruct((TOP_K, n_tok), F32)],
        compiler_params=_cparams("parallel", "parallel"),
        name="fourier_out",
    )(x, av, av, cc, sc, w_out.astype(BF16), g_ffn.reshape(1, d), w_router.T)
    return outs


def _moe_plan(idx, n_tok):
    chunk, rows = MOE_CHUNK, MOE_ROWS
    nc = n_tok // chunk
    t_iota = jnp.arange(chunk, dtype=jnp.int32)
    before = (t_iota[:, None] < t_iota[None, :]).astype(BF16)
    lrow_lanes, seg_len = pl.pallas_call(
        _moe_rank_kernel,
        grid=(nc,),
        in_specs=[pl.BlockSpec((TOP_K, chunk), lambda c: (0, c)), _full((chunk, chunk))],
        out_specs=[pl.BlockSpec((1, TOP_K, chunk), lambda c: (c, 0, 0)),
                   pl.BlockSpec((1, N_EXPERTS, LANES), lambda c: (c, 0, 0))],
        out_shape=[jax.ShapeDtypeStruct((nc, TOP_K, chunk), jnp.int32),
                   jax.ShapeDtypeStruct((nc, N_EXPERTS, LANES), jnp.int32)],
        compiler_params=_cparams("parallel"),
        name="moe_rank",
    )(idx, before)
    seg_len = seg_len[:, :, 0]
    loff = jnp.cumsum(seg_len, axis=1) - seg_len
    region = (jnp.sum(seg_len, axis=0) + rows - 1) // rows * rows
    pend = jnp.cumsum(region)
    goff = (pend - region)[None, :] + jnp.cumsum(seg_len, axis=0) - seg_len
    n_blocks = (n_tok * TOP_K + nc * N_EXPERTS * (SEG_ALIGN - 1)) // rows + N_EXPERTS
    blk = jnp.arange(n_blocks, dtype=jnp.int32)
    block_exp = jnp.minimum(jnp.searchsorted(pend, blk * rows, side='right'), N_EXPERTS - 1).astype(jnp.int32)
    return dict(
        n_blocks=n_blocks,
        seg_len=seg_len.reshape(-1), loff=loff.reshape(-1).astype(jnp.int32), goff=goff.reshape(-1).astype(jnp.int32),
        lrow_lanes=lrow_lanes,
        lrow_cols=lrow_lanes.transpose(0, 2, 1).reshape(n_tok, TOP_K),
        n_used=(pend[-1] // rows).reshape(1).astype(jnp.int32), block_exp=block_exp)


def _moe_rank_kernel(idx_ref, before_ref, lrow_ref, len_ref):
    e_iota = lax.broadcasted_iota(jnp.int32, (N_EXPERTS, idx_ref.shape[1]), 0)
    picks = [jnp.where(idx_ref[k:k + 1, :] == e_iota, 1.0, 0.0) for k in range(TOP_K)]
    total = picks[0]
    for m in picks[1:]:
        total = total + m
    earlier = _dot(total.astype(BF16), before_ref[...])
    cnt = jnp.sum(total, axis=1, keepdims=True).astype(jnp.int32)
    seg = (cnt + (SEG_ALIGN - 1)) & (-SEG_ALIGN)
    e_col = lax.broadcasted_iota(jnp.int32, (N_EXPERTS, 1), 0)
    loff = jnp.zeros((N_EXPERTS, 1), jnp.int32)
    for e in range(N_EXPERTS - 1):
        loff = loff + jnp.where(e_col > e, seg[e:e + 1, :], 0)
    base = loff.astype(F32) + earlier
    for k in range(TOP_K):
        lrow_ref[0, k:k + 1, :] = jnp.sum(picks[k] * base, axis=0, keepdims=True).astype(jnp.int32)
        base = base + picks[k]
    len_ref[0] = jnp.broadcast_to(seg, (N_EXPERTS, LANES))


def _segment_copies(c, len_ref, loff_ref, goff_ref, make, act):
    for e in range(N_EXPERTS):
        ln = len_ref[c * N_EXPERTS + e]
        lo = loff_ref[c * N_EXPERTS + e]
        go = goff_ref[c * N_EXPERTS + e]
        for size in SEG_PIECES:
            done = ln & (-2 * size)

            @pl.when((ln & size) != 0)
            def _():
                act(make(pl.multiple_of(lo + done, SEG_ALIGN), pl.multiple_of(go + done, SEG_ALIGN), size))


def _dispatch_kernel(len_ref, loff_ref, goff_ref, h_ref, lrow_ref, xb_in_ref, xb_ref, stage_ref, sem):
    del xb_in_ref
    c = pl.program_id(0)
    slot = c % 2
    n_local, chunk = stage_ref.shape[1], h_ref.shape[0]

    def make(slot_):
        def _make(lr, gr, size):
            return pltpu.make_async_copy(stage_ref.at[slot_, pl.ds(lr, size)], xb_ref.at[pl.ds(gr, size)],
                                         sem.at[slot_])
        return _make

    r_iota = lax.broadcasted_iota(jnp.int32, (n_local, chunk), 0)
    sel = (lrow_ref[0, 0:1, :] == r_iota) | (lrow_ref[0, 1:2, :] == r_iota)
    stage_ref[slot] = _dot(jnp.where(sel, 1.0, 0.0).astype(BF16), h_ref[...]).astype(BF16)

    @pl.when(c > 0)
    def _():
        _segment_copies(c - 1, len_ref, loff_ref, goff_ref, make(1 - slot), lambda cp: cp.wait())

    _segment_copies(c, len_ref, loff_ref, goff_ref, make(slot), lambda cp: cp.start())

    @pl.when(c == pl.num_programs(0) - 1)
    def _():
        _segment_copies(c, len_ref, loff_ref, goff_ref, make(slot), lambda cp: cp.wait())


def _expert_kernel(be_ref, nu_ref, xb_ref, w1_ref, w3_ref, w2_ref, y_ref, acc_ref, *, nf, fc):
    b = pl.program_id(0)
    f = pl.program_id(1)

    @pl.when(b < nu_ref[0])
    def _():
        y = _swiglu(xb_ref[...], w1_ref.at[0, 0], w3_ref.at[0, 0], w2_ref.at[0, 0], fc)

        @pl.when(f == 0)
        def _():
            acc_ref[...] = y

        @pl.when(f > 0)
        def _():
            acc_ref[...] += y

        @pl.when(f == nf - 1)
        def _():
            y_ref[...] = acc_ref[...].astype(BF16)

    @pl.when((b >= nu_ref[0]) & (f == nf - 1))
    def _():
        y_ref[...] = jnp.zeros_like(y_ref)


def _combine_kernel(len_ref, loff_ref, goff_ref, y_ref, lcol_ref, gcol_ref, x_ref, g_ref, o_ref, ybuf_ref, sem,
                    *, final_norm):
    c = pl.program_id(0)
    nc = pl.num_programs(0)
    slot = c % 2
    chunk, n_local = x_ref.shape[0], ybuf_ref.shape[1]

    def make(slot_):
        def _make(lr, gr, size):
            return pltpu.make_async_copy(y_ref.at[pl.ds(gr, size)], ybuf_ref.at[slot_, pl.ds(lr, size)],
                                         sem.at[slot_])
        return _make

    @pl.when(c == 0)
    def _():
        ybuf_ref[...] = jnp.zeros_like(ybuf_ref)
        _segment_copies(c, len_ref, loff_ref, goff_ref, make(slot), lambda cp: cp.start())

    @pl.when(c + 1 < nc)
    def _():
        _segment_copies(c + 1, len_ref, loff_ref, goff_ref, make(1 - slot), lambda cp: cp.start())

    _segment_copies(c, len_ref, loff_ref, goff_ref, make(slot), lambda cp: cp.wait())

    yl = ybuf_ref[slot]
    r_iota = lax.broadcasted_iota(jnp.int32, (chunk, n_local), 1)
    xo = x_ref[...]
    for k in range(TOP_K):
        sel = lcol_ref[:, k:k + 1] == r_iota
        xo = xo + gcol_ref[:, k:k + 1] * _dot(jnp.where(sel, 1.0, 0.0).astype(BF16), yl)
    o_ref[...] = _rms(xo, g_ref[...]) if final_norm else xo


def _moe_layer(x, h, idx, gates, w1, w3, w2, layer, g_final=None):
    bsz, seq, d = x.shape
    n_tok = bsz * seq
    f = w1.shape[3]
    rows, chunk = MOE_ROWS, MOE_CHUNK
    nc = n_tok // chunk
    plan = _moe_plan(idx, n_tok)
    n_blocks = plan['n_blocks']
    n_rows = n_blocks * rows
    seg = (plan['seg_len'], plan['loff'], plan['goff'])

    xb = pl.pallas_call(
        _dispatch_kernel,
        grid_spec=pltpu.PrefetchScalarGridSpec(
            num_scalar_prefetch=3, grid=(nc,),
            in_specs=[pl.BlockSpec((chunk, d), lambda c, *_: (c, 0)),
                      pl.BlockSpec((1, TOP_K, chunk), lambda c, *_: (c, 0, 0)),
                      pl.BlockSpec(memory_space=pl.ANY)],
            out_specs=pl.BlockSpec(memory_space=pl.ANY),
            scratch_shapes=[pltpu.VMEM((2, LOCAL_ROWS, d), BF16), pltpu.SemaphoreType.DMA((2,))]),
        out_shape=jax.ShapeDtypeStruct((n_rows, d), BF16),
        input_output_aliases={5: 0},
        compiler_params=_cparams("arbitrary"),
        name="moe_dispatch",
    )(*seg, h.reshape(n_tok, d), plan['lrow_lanes'], jnp.zeros((n_rows, d), BF16))

    nf = 2
    fblk = f // nf

    def blk_idx(b, nu):
        return jnp.minimum(b, nu[0] - 1)

    def f_idx(b, fi, nu):
        return jnp.where(b < nu[0], fi, nf - 1)

    y = pl.pallas_call(
        functools.partial(_expert_kernel, nf=nf, fc=256),
        grid_spec=pltpu.PrefetchScalarGridSpec(
            num_scalar_prefetch=2, grid=(n_blocks, nf),
            in_specs=[pl.BlockSpec((rows, d), lambda b, fi, be, nu: (blk_idx(b, nu), 0)),
                      pl.BlockSpec((1, 1, d, fblk),
                                   lambda b, fi, be, nu: (layer, be[blk_idx(b, nu)], 0, f_idx(b, fi, nu))),
                      pl.BlockSpec((1, 1, d, fblk),
                                   lambda b, fi, be, nu: (layer, be[blk_idx(b, nu)], 0, f_idx(b, fi, nu))),
                      pl.BlockSpec((1, 1, fblk, d),
                                   lambda b, fi, be, nu: (layer, be[blk_idx(b, nu)], f_idx(b, fi, nu), 0))],
            out_specs=pl.BlockSpec((rows, d), lambda b, fi, be, nu: (b, 0)),
            scratch_shapes=[pltpu.VMEM((rows, d), F32)]),
        out_shape=jax.ShapeDtypeStruct((n_rows, d), BF16),
        compiler_params=_cparams("arbitrary", "arbitrary"),
        name="moe_experts",
    )(plan['block_exp'], plan['n_used'], xb, w1, w3, w2)

    final_norm = g_final is not None
    g = (g_final if final_norm else jnp.ones((d,), F32)).reshape(1, d)
    out = pl.pallas_call(
        functools.partial(_combine_kernel, final_norm=final_norm),
        grid_spec=pltpu.PrefetchScalarGridSpec(
            num_scalar_prefetch=3, grid=(nc,),
            in_specs=[pl.BlockSpec(memory_space=pl.ANY),
                      pl.BlockSpec((chunk, TOP_K), lambda c, *_: (c, 0)),
                      pl.BlockSpec((chunk, TOP_K), lambda c, *_: (c, 0)),
                      pl.BlockSpec((chunk, d), lambda c, *_: (c, 0)),
                      pl.BlockSpec((1, d), lambda c, *_: (0, 0))],
            out_specs=pl.BlockSpec((chunk, d), lambda c, *_: (c, 0)),
            scratch_shapes=[pltpu.VMEM((2, LOCAL_ROWS, d), BF16), pltpu.SemaphoreType.DMA((2,))]),
        out_shape=jax.ShapeDtypeStruct((n_tok, d), F32),
        compiler_params=_cparams("arbitrary"),
        name="moe_combine",
    )(*seg, y, plan['lrow_cols'], gates.T, x.reshape(n_tok, d), g)
    return out.reshape(bsz, seq, d)


def _qkv_kernel(x_ref, g_ref, w_ref, *rest, nc):
    o_refs, h_ref = rest[:-1], rest[-1]
    tm = x_ref.shape[1]
    gw = 3 * ATTN_WIDTH
    h = _rms(x_ref[0], g_ref[...])
    nlc = h.shape[1] // LANES
    for c in range(nlc):
        h_ref[c] = h[:, c * LANES:(c + 1) * LANES]
    for gi, (_, dil) in enumerate(ATTN_GROUPS):
        n = tm // dil
        hb = h if dil == 1 else jnp.concatenate(
            [jnp.concatenate([h_ref[c, pl.ds(r, n, stride=dil), :] for c in range(nlc)], axis=1)
             for r in range(dil)], axis=0)
        hb = hb.astype(BF16)
        for c in range(gi * gw, (gi + 1) * gw, nc):
            res = _dot(hb, w_ref[:, c:c + nc]).astype(BF16)
            for r in range(dil):
                o_refs[gi][0, :, r * gw + c - gi * gw:r * gw + c - gi * gw + nc] = res[r * n:(r + 1) * n]


def _attn_kernel(q_ref, kp_ref, kc_ref, kn_ref, vp_ref, vc_ref, vn_ref, o_ref, lse_ref,
                 *, tq, seg, dil, radius, slopes):
    j = pl.program_id(2)
    hd = ATTN_HEAD_DIM
    tk = tq + 2 * radius
    q = q_ref[0] * jnp.asarray(1.0 / math.sqrt(hd), BF16)
    k = jnp.concatenate([kp_ref[0], kc_ref[0], kn_ref[0]], axis=0)
    v = jnp.concatenate([vp_ref[0], vc_ref[0], vn_ref[0]], axis=0)
    qi = lax.broadcasted_iota(jnp.int32, (tq, tk), 0)
    kj = lax.broadcasted_iota(jnp.int32, (tq, tk), 1)
    arel = jnp.abs(kj - radius - qi)
    kpos = j * tq - radius + kj
    valid = (arel <= radius) & (kpos >= 0) & (kpos < seg)
    dist = (dil * arel).astype(F32)
    low = lax.broadcasted_iota(jnp.int32, (1, 2 * hd), 1) < hd
    for hp in range(ATTN_HEADS // 2):
        sl = slice(hp * 2 * hd, (hp + 1) * 2 * hd)
        qp, kp, vp = q[:, sl], k[:, sl], v[:, sl]
        o_pair, l_pair = None, None
        for sub in range(2):
            mine = low if sub == 0 else jnp.logical_not(low)
            qm = jnp.where(mine, qp, jnp.zeros_like(qp))
            s = lax.dot_general(qm, kp, (((1,), (1,)), ((), ())), preferred_element_type=F32)
            s = jnp.where(valid, s - slopes[2 * hp + sub] * dist, MASK_VALUE)
            m = jnp.max(s, axis=-1, keepdims=True)
            p = jnp.exp(s - m)
            den = jnp.sum(p, axis=-1, keepdims=True)
            o = _dot(p.astype(BF16), vp) / den
            l = jnp.broadcast_to(m + jnp.log(den), o.shape)
            o_pair = o if sub == 0 else jnp.where(low, o_pair, o)
            l_pair = l if sub == 0 else jnp.where(low, l_pair, l)
        o_ref[0, :, sl] = o_pair.astype(BF16)
        lse_ref[0, :, sl] = l_pair


def _attn_out_ffn_kernel(x_ref, o0_ref, o1_ref, o2_ref, l0_ref, l1_ref, l2_ref, wo_ref, gf_ref,
                         w1_ref, w3_ref, w2_ref, out_ref, so_ref, sl_ref, *, fc):
    aw = ATTN_WIDTH
    ls, os_ = [], []
    for gi, (o_ref, l_ref) in enumerate(((o0_ref, l0_ref), (o1_ref, l1_ref), (o2_ref, l2_ref))):
        dil = ATTN_GROUPS[gi][1]
        if dil == 1:
            os_.append(o_ref[0].astype(F32))
            ls.append(l_ref[0])
            continue
        n = o_ref.shape[1]
        nlc = aw // LANES
        for r in range(dil):
            ov = o_ref[0, :, r * aw:(r + 1) * aw].astype(F32)
            lv = l_ref[0, :, r * aw:(r + 1) * aw]
            for c in range(nlc):
                so_ref[gi - 1, c, pl.ds(r, n, stride=dil), :] = ov[:, c * LANES:(c + 1) * LANES]
                sl_ref[gi - 1, c, pl.ds(r, n, stride=dil), :] = lv[:, c * LANES:(c + 1) * LANES]
        os_.append(jnp.concatenate([so_ref[gi - 1, c] for c in range(nlc)], axis=1))
        ls.append(jnp.concatenate([sl_ref[gi - 1, c] for c in range(nlc)], axis=1))
    m = jnp.maximum(jnp.maximum(ls[0], ls[1]), ls[2])
    es = [jnp.exp(l - m) for l in ls]
    den = es[0] + es[1] + es[2]
    o = (es[0] * os_[0] + es[1] * os_[1] + es[2] * os_[2]) / den
    x1 = x_ref[0] + _dot(o.astype(BF16), wo_ref[...])
    hb = _rms(x1, gf_ref[...]).astype(BF16)
    out_ref[0] = x1 + _swiglu(hb, w1_ref.at[0], w3_ref.at[0], w2_ref.at[0], fc)


def _alibi_slopes():
    n = len(ATTN_GROUPS) * ATTN_HEADS
    s = np.float32(2.0) ** (np.float32(-8.0) * np.arange(1, n + 1, dtype=np.float32) / np.float32(n))
    return [float(v) for v in s]


def _attention_ffn_layer(x, g_mix, w_qkv, w_out, g_ffn, w1, w3, w2, layer, tm):
    bsz, seq, d = x.shape
    nqkv = w_qkv.shape[1]
    aw = ATTN_WIDTH
    gw = 3 * aw
    assert nqkv == len(ATTN_GROUPS) * gw
    qkvs = pl.pallas_call(
        functools.partial(_qkv_kernel, nc=aw),
        grid=(bsz, seq // tm),
        in_specs=[pl.BlockSpec((1, tm, d), lambda b, i: (b, i, 0)), _full((1, d)), _full((d, nqkv))],
        out_specs=[pl.BlockSpec((1, tm // dil, dil * gw), lambda b, i: (b, i, 0)) for _, dil in ATTN_GROUPS],
        out_shape=[jax.ShapeDtypeStruct((bsz, seq // dil, dil * gw), BF16) for _, dil in ATTN_GROUPS],
        scratch_shapes=[pltpu.VMEM((d // LANES, tm, LANES), F32)],
        compiler_params=_cparams("parallel", "parallel"),
        name="qkv_proj",
    )(x, g_mix.reshape(1, d), w_qkv.astype(BF16))

    slopes = _alibi_slopes()
    outs, lses = [], []
    for gi, (window, dil) in enumerate(ATTN_GROUPS):
        radius = window // (2 * dil)
        seg = seq // dil
        tq = min(256, seg)
        hr = tq // radius
        last = seg // radius - 1
        view = qkvs[gi]

        def col(part, r):
            return r * 3 + part

        def cur(part):
            return pl.BlockSpec((1, tq, aw), lambda b, r, j, part=part: (b, j, col(part, r)))

        def prev(part):
            return pl.BlockSpec((1, radius, aw),
                                lambda b, r, j, part=part: (b, jnp.maximum(j * hr - 1, 0), col(part, r)))

        def nxt(part):
            return pl.BlockSpec((1, radius, aw),
                                lambda b, r, j, part=part: (b, jnp.minimum((j + 1) * hr, last), col(part, r)))

        o, lse = pl.pallas_call(
            functools.partial(_attn_kernel, tq=tq, seg=seg, dil=dil, radius=radius,
                              slopes=slopes[gi * ATTN_HEADS:(gi + 1) * ATTN_HEADS]),
            grid=(bsz, dil, seg // tq),
            in_specs=[cur(0), prev(1), cur(1), nxt(1), prev(2), cur(2), nxt(2)],
            out_specs=[pl.BlockSpec((1, tq, aw), lambda b, r, j: (b, j, r)),
                       pl.BlockSpec((1, tq, aw), lambda b, r, j: (b, j, r))],
            out_shape=[jax.ShapeDtypeStruct((bsz, seg, dil * aw), BF16),
                       jax.ShapeDtypeStruct((bsz, seg, dil * aw), F32)],
            compiler_params=_cparams("parallel", "parallel", "parallel"),
            name=f"dilated_attn_{dil}",
        )(view, view, view, view, view, view, view)
        outs.append(o)
        lses.append(lse)

    f = w1.shape[2]
    row = lambda width: pl.BlockSpec((1, tm, width), lambda b, i: (b, i, 0))
    dilated = [pl.BlockSpec((1, tm // dil, dil * aw), lambda b, i: (b, i, 0)) for _, dil in ATTN_GROUPS]
    return pl.pallas_call(
        functools.partial(_attn_out_ffn_kernel, fc=256),
        grid=(bsz, seq // tm),
        in_specs=[row(d)] + dilated + dilated + [
            _full((aw, d)), _full((1, d)),
            _stacked((d, f), layer), _stacked((d, f), layer), _stacked((f, d), layer)],
        out_specs=row(d),
        out_shape=jax.ShapeDtypeStruct(x.shape, F32),
        scratch_shapes=[pltpu.VMEM((len(ATTN_GROUPS) - 1, aw // LANES, tm, LANES), F32),
                        pltpu.VMEM((len(ATTN_GROUPS) - 1, aw // LANES, tm, LANES), F32)],
        compiler_params=_cparams("parallel", "parallel"),
        name="attn_out_ffn",
    )(x, *outs, *lses, w_out.astype(BF16), g_ffn.reshape(1, d), w1, w3, w2)


def _conv_glu_kernel(x_ref, g_ref, w_ref, u_ref):
    d = x_ref.shape[2]
    hb = _rms(x_ref[0], g_ref[...]).astype(BF16)
    a = _dot(hb, w_ref[:, :d])
    b = _dot(hb, w_ref[:, d:])
    u_ref[0] = a * jax.nn.sigmoid(b)


def _conv_out_kernel(x_ref, up_ref, uc_ref, un_ref, wdw_ref, bdw_ref, lg_ref, lb_ref, w2_ref, gf_ref, wrt_ref,
                     xo_ref, h_ref, idx_ref, gate_ref, ext_ref, *, ts, seq):
    i = pl.program_id(1)
    n_ext = ts + 2 * HALO
    ue = jnp.concatenate([up_ref[0], uc_ref[0], un_ref[0]], axis=0)
    pos = i * ts - HALO + lax.broadcasted_iota(jnp.int32, (n_ext, 1), 0)
    ext_ref[0] = jnp.where((pos >= 0) & (pos < seq), ue, 0.0)
    for r in range(1, SUBLANES):
        ext_ref[r, 0:n_ext - SUBLANES, :] = ext_ref[0, r:r + n_ext - SUBLANES, :]
    half = CONV_WIDTH // 2
    acc = None
    for k in range(CONV_WIDTH):
        start = HALO + k - half
        r = start % SUBLANES
        term = ext_ref[r, start - r:start - r + ts, :] * wdw_ref[k:k + 1, :]
        acc = term if acc is None else acc + term
    u = acc + bdw_ref[...]
    mu = jnp.mean(u, axis=-1, keepdims=True)
    uc = u - mu
    var = jnp.mean(uc * uc, axis=-1, keepdims=True)
    z = uc * lax.rsqrt(var + LN_EPS) * lg_ref[...] + lb_ref[...]
    z = (z * jax.nn.sigmoid(z)).astype(BF16)
    x1 = x_ref[0] + _dot(z, w2_ref[...])
    _moe_prologue(x1, gf_ref, wrt_ref, xo_ref, h_ref, idx_ref, gate_ref)


def _conv_layer(x, g_mix, w_pw1, w_dw, b_dw, ln_g, ln_b, w_pw2, g_ffn, w_router, tm, ts):
    bsz, seq, d = x.shape
    n_tok = bsz * seq
    u = pl.pallas_call(
        _conv_glu_kernel,
        grid=(bsz, seq // tm),
        in_specs=[pl.BlockSpec((1, tm, d), lambda b, i: (b, i, 0)), _full((1, d)), _full((d, 2 * d))],
        out_specs=pl.BlockSpec((1, tm, d), lambda b, i: (b, i, 0)),
        out_shape=jax.ShapeDtypeStruct(x.shape, F32),
        compiler_params=_cparams("parallel", "parallel"),
        name="conv_glu",
    )(x, g_mix.reshape(1, d), w_pw1.astype(BF16))

    nt = seq // ts
    vec = lambda a: a.reshape(1, d)
    return pl.pallas_call(
        functools.partial(_conv_out_kernel, ts=ts, seq=seq),
        grid=(bsz, nt),
        in_specs=[pl.BlockSpec((1, ts, d), lambda b, i: (b, i, 0))] + _halo_specs(ts, seq, d) + [
            _full((CONV_WIDTH, d)), _full((1, d)), _full((1, d)), _full((1, d)), _full((d, d)),
            _full((1, d)), _full((N_EXPERTS, d))],
        out_specs=[pl.BlockSpec((1, ts, d), lambda b, i: (b, i, 0)),
                   pl.BlockSpec((1, ts, d), lambda b, i: (b, i, 0)),
                   pl.BlockSpec((TOP_K, ts), lambda b, i: (0, b * nt + i)),
                   pl.BlockSpec((TOP_K, ts), lambda b, i: (0, b * nt + i))],
        out_shape=[jax.ShapeDtypeStruct(x.shape, F32), jax.ShapeDtypeStruct(x.shape, BF16),
                   jax.ShapeDtypeStruct((TOP_K, n_tok), jnp.int32),
                   jax.ShapeDtypeStruct((TOP_K, n_tok), F32)],
        scratch_shapes=[pltpu.VMEM((SUBLANES, ts + 2 * HALO, d), F32)],
        compiler_params=_cparams("parallel", "parallel"),
        name="conv_out",
    )(x, u, u, u, w_dw, vec(b_dw), vec(ln_g), vec(ln_b), w_pw2.astype(BF16), vec(g_ffn), w_router.T)


def kernel(x, g_mix, g_ffn, g_final, a_w_grp, a_scale, b_w_out, c_w_qkv, c_w_out, d_w_pw1, d_w_dw, d_b_dw,
           d_ln_g, d_ln_b, d_w_pw2, ffn_w1, ffn_w3, ffn_w2, moe_router, moe_w1, moe_w3, moe_w2):
    assert g_mix.shape[0] == 4, "one layer of each mixer kind"
    tm = min(512, x.shape[1])
    ffn_w = [w.astype(BF16) for w in (ffn_w1, ffn_w3, ffn_w2)]
    moe_w = [w.astype(BF16) for w in (moe_w1, moe_w3, moe_w2)]
    x = _pool_ffn_layer(x, g_mix[0], a_w_grp[0], a_scale[0], g_ffn[0], *ffn_w, 0, tm)
    x, h, idx, gates = _fourier_layer(x, g_mix[1], b_w_out[0], g_ffn[1], moe_router[0], tm)
    x = _moe_layer(x, h, idx, gates, *moe_w, 0)
    x = _attention_ffn_layer(x, g_mix[2], c_w_qkv[0], c_w_out[0], g_ffn[2], *ffn_w, 1, tm)
    x, h, idx, gates = _conv_layer(x, g_mix[3], d_w_pw1[0], d_w_dw[0], d_b_dw[0], d_ln_g[0], d_ln_b[0],
                                   d_w_pw2[0], g_ffn[3], moe_router[1], tm, tm)
    return _moe_layer(x, h, idx, gates, *moe_w, 1, g_final=g_final)
```

```python
import functools
import math

import numpy as np
import jax
import jax.numpy as jnp
from jax import lax
from jax.experimental import pallas as pl
from jax.experimental.pallas import tpu as pltpu

F32 = jnp.float32
BF16 = jnp.bfloat16

RMS_EPS = 1e-6
LN_EPS = 1e-5
MASK_VALUE = -1e30
POOL_WINDOWS = (2, 4, 8, 16)
FOURIER_GROUPS = 4
ATTN_GROUPS = ((128, 1), (512, 4), (2048, 16))
ATTN_HEADS = 8
ATTN_HEAD_DIM = 64
ATTN_WIDTH = ATTN_HEADS * ATTN_HEAD_DIM
CONV_WIDTH = 31
N_EXPERTS = 8
TOP_K = 2

LANES = 128
SUBLANES = 8
HALO = 16
FFT_N2 = 128
MOE_ROWS = 512
MOE_CHUNK = 512
SEG_ALIGN = 16
SEG_PIECES = (512, 256, 128, 64, 32, 16)
LOCAL_ROWS = 1152
VMEM_LIMIT_BYTES = 56 * 1024 * 1024


def _cparams(*sem):
    return pltpu.CompilerParams(dimension_semantics=sem, vmem_limit_bytes=VMEM_LIMIT_BYTES)


def _dot(a, b):
    return jnp.dot(a, b, preferred_element_type=F32)


def _rms(x, g):
    ms = jnp.mean(x * x, axis=-1, keepdims=True)
    return x * lax.rsqrt(ms + RMS_EPS) * g


def _swiglu(hb, w1_ref, w3_ref, w2_ref, g_ref, fc):
    f = w1_ref.shape[-1]
    for c in range(0, f, fc):
        a = _dot(hb, w1_ref[:, c:c + fc])
        b = _dot(hb, w3_ref[:, c:c + fc])
        g_ref[:, c:c + fc] = (a * jax.nn.sigmoid(a) * b).astype(BF16)
    return _dot(g_ref[...], w2_ref[...])


def _route(h32, wrt_ref, idx_ref, gate_ref):
    logits = lax.dot_general(wrt_ref[...], h32, (((1,), (1,)), ((), ())),
                             precision=lax.Precision.HIGHEST, preferred_element_type=F32)
    e_iota = lax.broadcasted_iota(jnp.int32, logits.shape, 0)
    m1 = jnp.max(logits, axis=0, keepdims=True)
    i1 = jnp.min(jnp.where(logits == m1, e_iota, N_EXPERTS), axis=0, keepdims=True)
    rest = jnp.where(e_iota == i1, -jnp.inf, logits)
    m2 = jnp.max(rest, axis=0, keepdims=True)
    i2 = jnp.min(jnp.where(rest == m2, e_iota, N_EXPERTS), axis=0, keepdims=True)
    e2 = jnp.exp(m2 - m1)
    den = 1.0 + e2
    idx_ref[0:1, :] = i1
    idx_ref[1:2, :] = i2
    gate_ref[0:1, :] = 1.0 / den
    gate_ref[1:2, :] = e2 / den


def _moe_prologue(x1, gf_ref, wrt_ref, x_out_ref, h_ref, idx_ref, gate_ref):
    x_out_ref[0] = x1
    h2 = _rms(x1, gf_ref[...])
    h_ref[0] = h2.astype(BF16)
    _route(h2, wrt_ref, idx_ref, gate_ref)


def _pool_ffn_kernel(xp_ref, xc_ref, xn_ref, gm_ref, wg_ref, sc_ref, gf_ref, w1_ref, w3_ref, w2_ref,
                     o_ref, hid_ref, *, ts, seq, fc):
    i = pl.program_id(1)
    xc = xc_ref[0]
    d = xc.shape[1]
    gd = d // len(POOL_WINDOWS)
    n_ext = ts + 2 * HALO
    xe = jnp.concatenate([xp_ref[0], xc, xn_ref[0]], axis=0)
    he = _rms(xe, gm_ref[...])
    pos = i * ts - HALO + lax.broadcasted_iota(jnp.int32, (n_ext, 1), 0)
    he = jnp.where((pos >= 0) & (pos < seq), he, 0.0)

    s = he + pltpu.roll(he, 1, 0)
    sums = [s]
    half = 1
    for _ in POOL_WINDOWS[1:]:
        s = s[:, gd:]
        s = pltpu.roll(s, half, 0) + pltpu.roll(s, n_ext - half, 0)
        sums.append(s)
        half *= 2

    tpos = i * ts + lax.broadcasted_iota(jnp.int32, (ts, 1), 0)
    ys = []
    for gi, w in enumerate(POOL_WINDOWS):
        cnt = jnp.minimum(tpos + w // 2, seq) - jnp.maximum(tpos - w // 2, 0)
        mean = sums[gi][HALO:HALO + ts, :gd] / cnt.astype(F32)
        pooled = mean - he[HALO:HALO + ts, gi * gd:(gi + 1) * gd]
        ys.append(_dot(pooled.astype(BF16), wg_ref[gi]))
    x1 = xc + jnp.concatenate(ys, axis=1) * sc_ref[...]
    hb = _rms(x1, gf_ref[...]).astype(BF16)
    o_ref[0] = x1 + _swiglu(hb, w1_ref.at[0], w3_ref.at[0], w2_ref.at[0], hid_ref, fc)


def _halo_specs(ts, seq, d):
    r = ts // HALO
    last = seq // HALO - 1
    return [
        pl.BlockSpec((1, HALO, d), lambda b, i: (b, jnp.maximum(i * r - 1, 0), 0)),
        pl.BlockSpec((1, ts, d), lambda b, i: (b, i, 0)),
        pl.BlockSpec((1, HALO, d), lambda b, i: (b, jnp.minimum((i + 1) * r, last), 0)),
    ]


def _full(shape):
    n = len(shape)
    return pl.BlockSpec(shape, lambda *_: (0,) * n)


def _stacked(shape, layer):
    n = len(shape)
    return pl.BlockSpec((1,) + tuple(shape), lambda *_: (layer,) + (0,) * n)


def _pool_ffn_layer(x, g_mix, w_grp, scale, g_ffn, w1, w3, w2, layer, ts):
    bsz, seq, d = x.shape
    f = w1.shape[2]
    kern = functools.partial(_pool_ffn_kernel, ts=ts, seq=seq, fc=256)
    return pl.pallas_call(
        kern,
        grid=(bsz, seq // ts),
        in_specs=_halo_specs(ts, seq, d) + [
            _full((1, d)), _full(w_grp.shape), _full((1, d)), _full((1, d)),
            _stacked((d, f), layer), _stacked((d, f), layer), _stacked((f, d), layer)],
        out_specs=pl.BlockSpec((1, ts, d), lambda b, i: (b, i, 0)),
        out_shape=jax.ShapeDtypeStruct(x.shape, F32),
        scratch_shapes=[pltpu.VMEM((ts, f), BF16)],
        compiler_params=_cparams("parallel", "parallel"),
        name="pool_ffn",
    )(x, x, x, g_mix.reshape(1, d), w_grp.astype(BF16), scale.reshape(1, d), g_ffn.reshape(1, d), w1, w3, w2)


def _fft_stage1_kernel(x_ref, g_ref, f1_ref, o_ref, *, nseg):
    for j in range(nseg):
        h = _rms(x_ref[0, :, j, :], g_ref[...]).astype(BF16)
        o_ref[0, j] = _dot(f1_ref[...], h)


def _fft_stage2_kernel(br_ref, bi_ref, m_ref, o_ref, *, nk):
    n2 = FFT_N2
    for j in range(nk):
        rhs = jnp.concatenate([br_ref[0, :, j, :], bi_ref[0, :, j, :]], axis=0).astype(BF16)
        res = _dot(m_ref[j], rhs)
        o_ref[0, 0, :, j, :] = res[:n2]
        o_ref[0, 1, :, j, :] = res[n2:]


def _fourier_out_kernel(x_ref, ar_ref, ai_ref, cc_ref, sc_ref, wo_ref, gf_ref, wrt_ref,
                        xo_ref, h_ref, idx_ref, gate_ref, *, norm):
    d = x_ref.shape[2]
    gd = d // FOURIER_GROUPS
    fs = []
    for gi in range(FOURIER_GROUPS):
        sl = slice(gi * gd, (gi + 1) * gd)
        fs.append(_dot(ar_ref[0, 0, :, sl].astype(BF16), cc_ref[...])
                  + _dot(ai_ref[0, 0, :, sl].astype(BF16), sc_ref[...]))
    f = (jnp.concatenate(fs, axis=1) * norm).astype(BF16)
    x1 = x_ref[0] + _dot(f, wo_ref[...])
    _moe_prologue(x1, gf_ref, wrt_ref, xo_ref, h_ref, idx_ref, gate_ref)


def _dft_tables(seq, gd):
    n2 = FFT_N2
    n1 = seq // n2
    k = np.arange(n1)
    ang1 = 2.0 * np.pi * ((k[:, None] * k[None, :]) % n1) / n1
    f1 = np.concatenate([np.cos(ang1), -np.sin(ang1)], axis=0)
    c = np.arange(gd)
    angc = 2.0 * np.pi * ((c[:, None] * c[None, :]) % gd) / gd
    return (jnp.asarray(f1, BF16), jnp.asarray(np.cos(angc), BF16), jnp.asarray(np.sin(angc), BF16))


def _twiddled_dft(seq):
    n2 = FFT_N2
    n1 = seq // n2
    k1 = lax.broadcasted_iota(jnp.int32, (n1, n2, n2), 0)
    k2 = lax.broadcasted_iota(jnp.int32, (n1, n2, n2), 1)
    a = lax.broadcasted_iota(jnp.int32, (n1, n2, n2), 2)
    ang = ((a * (n1 * k2 + k1)) % seq).astype(F32) * (2.0 * math.pi / seq)
    mr = jnp.cos(ang)
    mi = -jnp.sin(ang)
    top = jnp.concatenate([mr, -mi], axis=2)
    bot = jnp.concatenate([mi, mr], axis=2)
    return jnp.concatenate([top, bot], axis=1).astype(BF16)


def _fourier_layer(x, g_mix, w_out, g_ffn, w_router, tm):
    bsz, seq, d = x.shape
    n2 = FFT_N2
    n1 = seq // n2
    gd = d // FOURIER_GROUPS
    f1, cc, sc = _dft_tables(seq, gd)
    mtab = _twiddled_dft(seq)
    nseg = 8
    b1 = pl.pallas_call(
        functools.partial(_fft_stage1_kernel, nseg=nseg),
        grid=(bsz, n2 // nseg),
        in_specs=[pl.BlockSpec((1, n1, nseg, d), lambda b, t: (b, 0, t, 0)), _full((1, d)), _full(f1.shape)],
        out_specs=pl.BlockSpec((1, nseg, 2 * n1, d), lambda b, t: (b, t, 0, 0)),
        out_shape=jax.ShapeDtypeStruct((bsz, n2, 2 * n1, d), F32),
        compiler_params=_cparams("parallel", "parallel"),
        name="fft_stage1",
    )(x.reshape(bsz, n1, n2, d), g_mix.reshape(1, d), f1)

    nk = 8
    a = pl.pallas_call(
        functools.partial(_fft_stage2_kernel, nk=nk),
        grid=(bsz, n1 // nk),
        in_specs=[pl.BlockSpec((1, n2, nk, d), lambda b, t: (b, 0, t, 0)),
                  pl.BlockSpec((1, n2, nk, d), lambda b, t: (b, 0, n1 // nk + t, 0)),
                  pl.BlockSpec((nk, 2 * n2, 2 * n2), lambda b, t: (t, 0, 0))],
        out_specs=pl.BlockSpec((1, 2, n2, nk, d), lambda b, t: (b, 0, 0, t, 0)),
        out_shape=jax.ShapeDtypeStruct((bsz, 2, n2, n1, d), F32),
        compiler_params=_cparams("parallel", "parallel"),
        name="fft_stage2",
    )(b1, b1, mtab)

    av = a.reshape(bsz, 2, seq, d)
    n_tok = bsz * seq
    nt = seq // tm
    outs = pl.pallas_call(
        functools.partial(_fourier_out_kernel, norm=1.0 / math.sqrt(seq * gd)),
        grid=(bsz, nt),
        in_specs=[pl.BlockSpec((1, tm, d), lambda b, i: (b, i, 0)),
                  pl.BlockSpec((1, 1, tm, d), lambda b, i: (b, 0, i, 0)),
                  pl.BlockSpec((1, 1, tm, d), lambda b, i: (b, 1, i, 0)),
                  _full((gd, gd)), _full((gd, gd)), _full((d, d)), _full((1, d)), _full((N_EXPERTS, d))],
        out_specs=[pl.BlockSpec((1, tm, d), lambda b, i: (b, i, 0)),
                   pl.BlockSpec((1, tm, d), lambda b, i: (b, i, 0)),
                   pl.BlockSpec((TOP_K, tm), lambda b, i: (0, b * nt + i)),
                   pl.BlockSpec((TOP_K, tm), lambda b, i: (0, b * nt + i))],
        out_shape=[jax.ShapeDtypeStruct(x.shape, F32), jax.ShapeDtypeStruct(x.shape, BF16),
                   jax.ShapeDtypeStruct((TOP_K, n_tok), jnp.int32),
                   jax.ShapeDtypeStruct((TOP_K, n_tok), F32)],
        compiler_params=_cparams("parallel", "parallel"),
        name="fourier_out",
    )(x, av, av, cc, sc, w_out.astype(BF16), g_ffn.reshape(1, d), w_router.T)
    return outs


def _moe_plan(idx, n_tok):
    chunk, rows = MOE_CHUNK, MOE_ROWS
    nc = n_tok // chunk
    t_iota = jnp.arange(chunk, dtype=jnp.int32)
    before = (t_iota[:, None] < t_iota[None, :]).astype(BF16)
    lrow_lanes, seg_len = pl.pallas_call(
        _moe_rank_kernel,
        grid=(nc,),
        in_specs=[pl.BlockSpec((TOP_K, chunk), lambda c: (0, c)), _full((chunk, chunk))],
        out_specs=[pl.BlockSpec((1, TOP_K, chunk), lambda c: (c, 0, 0)),
                   pl.BlockSpec((1, N_EXPERTS, LANES), lambda c: (c, 0, 0))],
        out_shape=[jax.ShapeDtypeStruct((nc, TOP_K, chunk), jnp.int32),
                   jax.ShapeDtypeStruct((nc, N_EXPERTS, LANES), jnp.int32)],
        compiler_params=_cparams("parallel"),
        name="moe_rank",
    )(idx, before)
    seg_len = seg_len[:, :, 0]
    loff = jnp.cumsum(seg_len, axis=1) - seg_len
    region = (jnp.sum(seg_len, axis=0) + rows - 1) // rows * rows
    pend = jnp.cumsum(region)
    goff = (pend - region)[None, :] + jnp.cumsum(seg_len, axis=0) - seg_len
    n_blocks = (n_tok * TOP_K + nc * N_EXPERTS * (SEG_ALIGN - 1)) // rows + N_EXPERTS
    blk = jnp.arange(n_blocks, dtype=jnp.int32)
    block_exp = jnp.minimum(jnp.sum((blk[:, None] * rows >= pend[None, :]).astype(jnp.int32), axis=1),
                            N_EXPERTS - 1)
    return dict(
        n_blocks=n_blocks,
        seg_len=seg_len.reshape(-1), loff=loff.reshape(-1).astype(jnp.int32), goff=goff.reshape(-1).astype(jnp.int32),
        lrow_lanes=lrow_lanes,
        lrow_cols=lrow_lanes.transpose(0, 2, 1).reshape(n_tok, TOP_K),
        n_used=(pend[-1] // rows).reshape(1).astype(jnp.int32), block_exp=block_exp)


def _moe_rank_kernel(idx_ref, before_ref, lrow_ref, len_ref):
    e_iota = lax.broadcasted_iota(jnp.int32, (N_EXPERTS, idx_ref.shape[1]), 0)
    picks = [jnp.where(idx_ref[k:k + 1, :] == e_iota, 1.0, 0.0) for k in range(TOP_K)]
    total = picks[0]
    for m in picks[1:]:
        total = total + m
    earlier = _dot(total.astype(BF16), before_ref[...])
    cnt = jnp.sum(total, axis=1, keepdims=True).astype(jnp.int32)
    seg = (cnt + (SEG_ALIGN - 1)) & (-SEG_ALIGN)
    e_col = lax.broadcasted_iota(jnp.int32, (N_EXPERTS, 1), 0)
    loff = jnp.zeros((N_EXPERTS, 1), jnp.int32)
    for e in range(N_EXPERTS - 1):
        loff = loff + jnp.where(e_col > e, seg[e:e + 1, :], 0)
    base = loff.astype(F32) + earlier
    for k in range(TOP_K):
        lrow_ref[0, k:k + 1, :] = jnp.sum(picks[k] * base, axis=0, keepdims=True).astype(jnp.int32)
        base = base + picks[k]
    len_ref[0] = jnp.broadcast_to(seg, (N_EXPERTS, LANES))


def _segment_copies(c, len_ref, loff_ref, goff_ref, make, act):
    for e in range(N_EXPERTS):
        ln = len_ref[c * N_EXPERTS + e]
        lo = loff_ref[c * N_EXPERTS + e]
        go = goff_ref[c * N_EXPERTS + e]
        for size in SEG_PIECES:
            done = ln & (-2 * size)

            @pl.when((ln & size) != 0)
            def _():
                act(make(pl.multiple_of(lo + done, SEG_ALIGN), pl.multiple_of(go + done, SEG_ALIGN), size))


def _dispatch_kernel(len_ref, loff_ref, goff_ref, h_ref, lrow_ref, xb_in_ref, xb_ref, stage_ref, sem):
    del xb_in_ref
    c = pl.program_id(0)
    slot = c % 2
    n_local, chunk = stage_ref.shape[1], h_ref.shape[0]

    def make(slot_):
        def _make(lr, gr, size):
            return pltpu.make_async_copy(stage_ref.at[slot_, pl.ds(lr, size)], xb_ref.at[pl.ds(gr, size)],
                                         sem.at[slot_])
        return _make

    r_iota = lax.broadcasted_iota(jnp.int32, (n_local, chunk), 0)
    sel = (lrow_ref[0, 0:1, :] == r_iota) | (lrow_ref[0, 1:2, :] == r_iota)
    stage_ref[slot] = _dot(jnp.where(sel, 1.0, 0.0).astype(BF16), h_ref[...]).astype(BF16)

    @pl.when(c > 0)
    def _():
        _segment_copies(c - 1, len_ref, loff_ref, goff_ref, make(1 - slot), lambda cp: cp.wait())

    _segment_copies(c, len_ref, loff_ref, goff_ref, make(slot), lambda cp: cp.start())

    @pl.when(c == pl.num_programs(0) - 1)
    def _():
        _segment_copies(c, len_ref, loff_ref, goff_ref, make(slot), lambda cp: cp.wait())


def _expert_kernel(be_ref, nu_ref, xb_ref, w1_ref, w3_ref, w2_ref, y_ref, acc_ref, hid_ref, *, nf, fc):
    b = pl.program_id(0)
    f = pl.program_id(1)

    @pl.when(b < nu_ref[0])
    def _():
        y = _swiglu(xb_ref[...], w1_ref.at[0, 0], w3_ref.at[0, 0], w2_ref.at[0, 0], hid_ref, fc)

        @pl.when(f == 0)
        def _():
            acc_ref[...] = y

        @pl.when(f > 0)
        def _():
            acc_ref[...] += y

        @pl.when(f == nf - 1)
        def _():
            y_ref[...] = acc_ref[...].astype(BF16)

    @pl.when((b >= nu_ref[0]) & (f == nf - 1))
    def _():
        y_ref[...] = jnp.zeros_like(y_ref)


def _combine_kernel(len_ref, loff_ref, goff_ref, y_ref, lcol_ref, gcol_ref, x_ref, g_ref, o_ref, ybuf_ref, sem,
                    *, final_norm):
    c = pl.program_id(0)
    nc = pl.num_programs(0)
    slot = c % 2
    chunk, n_local = x_ref.shape[0], ybuf_ref.shape[1]

    def make(slot_):
        def _make(lr, gr, size):
            return pltpu.make_async_copy(y_ref.at[pl.ds(gr, size)], ybuf_ref.at[slot_, pl.ds(lr, size)],
                                         sem.at[slot_])
        return _make

    @pl.when(c == 0)
    def _():
        ybuf_ref[...] = jnp.zeros_like(ybuf_ref)
        _segment_copies(c, len_ref, loff_ref, goff_ref, make(slot), lambda cp: cp.start())

    @pl.when(c + 1 < nc)
    def _():
        _segment_copies(c + 1, len_ref, loff_ref, goff_ref, make(1 - slot), lambda cp: cp.start())

    _segment_copies(c, len_ref, loff_ref, goff_ref, make(slot), lambda cp: cp.wait())

    yl = ybuf_ref[slot]
    r_iota = lax.broadcasted_iota(jnp.int32, (chunk, n_local), 1)
    xo = x_ref[...]
    for k in range(TOP_K):
        sel = lcol_ref[:, k:k + 1] == r_iota
        xo = xo + gcol_ref[:, k:k + 1] * _dot(jnp.where(sel, 1.0, 0.0).astype(BF16), yl)
    o_ref[...] = _rms(xo, g_ref[...]) if final_norm else xo


def _moe_layer(x, h, idx, gates, w1, w3, w2, layer, g_final=None):
    bsz, seq, d = x.shape
    n_tok = bsz * seq
    f = w1.shape[3]
    rows, chunk = MOE_ROWS, MOE_CHUNK
    nc = n_tok // chunk
    plan = _moe_plan(idx, n_tok)
    n_blocks = plan['n_blocks']
    n_rows = n_blocks * rows
    seg = (plan['seg_len'], plan['loff'], plan['goff'])

    xb = pl.pallas_call(
        _dispatch_kernel,
        grid_spec=pltpu.PrefetchScalarGridSpec(
            num_scalar_prefetch=3, grid=(nc,),
            in_specs=[pl.BlockSpec((chunk, d), lambda c, *_: (c, 0)),
                      pl.BlockSpec((1, TOP_K, chunk), lambda c, *_: (c, 0, 0)),
                      pl.BlockSpec(memory_space=pl.ANY)],
            out_specs=pl.BlockSpec(memory_space=pl.ANY),
            scratch_shapes=[pltpu.VMEM((2, LOCAL_ROWS, d), BF16), pltpu.SemaphoreType.DMA((2,))]),
        out_shape=jax.ShapeDtypeStruct((n_rows, d), BF16),
        input_output_aliases={5: 0},
        compiler_params=_cparams("arbitrary"),
        name="moe_dispatch",
    )(*seg, h.reshape(n_tok, d), plan['lrow_lanes'], jnp.zeros((n_rows, d), BF16))

    nf = 2
    fblk = f // nf

    def blk_idx(b, nu):
        return jnp.minimum(b, nu[0] - 1)

    def f_idx(b, fi, nu):
        return jnp.where(b < nu[0], fi, nf - 1)

    y = pl.pallas_call(
        functools.partial(_expert_kernel, nf=nf, fc=256),
        grid_spec=pltpu.PrefetchScalarGridSpec(
            num_scalar_prefetch=2, grid=(n_blocks, nf),
            in_specs=[pl.BlockSpec((rows, d), lambda b, fi, be, nu: (blk_idx(b, nu), 0)),
                      pl.BlockSpec((1, 1, d, fblk),
                                   lambda b, fi, be, nu: (layer, be[blk_idx(b, nu)], 0, f_idx(b, fi, nu))),
                      pl.BlockSpec((1, 1, d, fblk),
                                   lambda b, fi, be, nu: (layer, be[blk_idx(b, nu)], 0, f_idx(b, fi, nu))),
                      pl.BlockSpec((1, 1, fblk, d),
                                   lambda b, fi, be, nu: (layer, be[blk_idx(b, nu)], f_idx(b, fi, nu), 0))],
            out_specs=pl.BlockSpec((rows, d), lambda b, fi, be, nu: (b, 0)),
            scratch_shapes=[pltpu.VMEM((rows, d), F32), pltpu.VMEM((rows, fblk), BF16)]),
        out_shape=jax.ShapeDtypeStruct((n_rows, d), BF16),
        compiler_params=_cparams("arbitrary", "arbitrary"),
        name="moe_experts",
    )(plan['block_exp'], plan['n_used'], xb, w1, w3, w2)

    final_norm = g_final is not None
    g = (g_final if final_norm else jnp.ones((d,), F32)).reshape(1, d)
    out = pl.pallas_call(
        functools.partial(_combine_kernel, final_norm=final_norm),
        grid_spec=pltpu.PrefetchScalarGridSpec(
            num_scalar_prefetch=3, grid=(nc,),
            in_specs=[pl.BlockSpec(memory_space=pl.ANY),
                      pl.BlockSpec((chunk, TOP_K), lambda c, *_: (c, 0)),
                      pl.BlockSpec((chunk, TOP_K), lambda c, *_: (c, 0)),
                      pl.BlockSpec((chunk, d), lambda c, *_: (c, 0)),
                      pl.BlockSpec((1, d), lambda c, *_: (0, 0))],
            out_specs=pl.BlockSpec((chunk, d), lambda c, *_: (c, 0)),
            scratch_shapes=[pltpu.VMEM((2, LOCAL_ROWS, d), BF16), pltpu.SemaphoreType.DMA((2,))]),
        out_shape=jax.ShapeDtypeStruct((n_tok, d), F32),
        compiler_params=_cparams("arbitrary"),
        name="moe_combine",
    )(*seg, y, plan['lrow_cols'], gates.T, x.reshape(n_tok, d), g)
    return out.reshape(bsz, seq, d)


def _qkv_kernel(x_ref, g_ref, w_ref, *rest, nc):
    o_refs, h_ref = rest[:-1], rest[-1]
    tm = x_ref.shape[1]
    gw = 3 * ATTN_WIDTH
    h = _rms(x_ref[0], g_ref[...])
    nlc = h.shape[1] // LANES
    for c in range(nlc):
        h_ref[c] = h[:, c * LANES:(c + 1) * LANES]
    for gi, (_, dil) in enumerate(ATTN_GROUPS):
        n = tm // dil
        hb = h if dil == 1 else jnp.concatenate(
            [jnp.concatenate([h_ref[c, pl.ds(r, n, stride=dil), :] for c in range(nlc)], axis=1)
             for r in range(dil)], axis=0)
        hb = hb.astype(BF16)
        for c in range(gi * gw, (gi + 1) * gw, nc):
            res = _dot(hb, w_ref[:, c:c + nc]).astype(BF16)
            for r in range(dil):
                o_refs[gi][0, :, r * gw + c - gi * gw:r * gw + c - gi * gw + nc] = res[r * n:(r + 1) * n]


def _attn_kernel(q_ref, kp_ref, kc_ref, kn_ref, vp_ref, vc_ref, vn_ref, o_ref, lse_ref,
                 *, tq, seg, dil, radius, slopes):
    j = pl.program_id(2)
    hd = ATTN_HEAD_DIM
    tb = q_ref.shape[1]
    tk = tq + 2 * radius
    q_all = q_ref[0] * jnp.asarray(1.0 / math.sqrt(hd), BF16)
    k_all = jnp.concatenate([kp_ref[0], kc_ref[0], kn_ref[0]], axis=0)
    v_all = jnp.concatenate([vp_ref[0], vc_ref[0], vn_ref[0]], axis=0)
    qi = lax.broadcasted_iota(jnp.int32, (tq, tk), 0)
    kj = lax.broadcasted_iota(jnp.int32, (tq, tk), 1)
    arel = jnp.abs(kj - radius - qi)
    band = arel <= radius
    dist = (dil * arel).astype(F32)
    alibi = [-(slope * dist) for slope in slopes]
    low = lax.broadcasted_iota(jnp.int32, (1, 2 * hd), 1) < hd
    for t in range(tb // tq):
        rows = slice(t * tq, (t + 1) * tq)
        kpos = j * tb + t * tq - radius + kj
        valid = band & (kpos >= 0) & (kpos < seg)
        q, k, v = q_all[rows], k_all[t * tq:t * tq + tk], v_all[t * tq:t * tq + tk]
        for hp in range(ATTN_HEADS // 2):
            sl = slice(hp * 2 * hd, (hp + 1) * 2 * hd)
            qp, kp, vp = q[:, sl], k[:, sl], v[:, sl]
            o_pair, l_pair = None, None
            for sub in range(2):
                mine = low if sub == 0 else jnp.logical_not(low)
                qm = jnp.where(mine, qp, jnp.zeros_like(qp))
                s = lax.dot_general(qm, kp, (((1,), (1,)), ((), ())), preferred_element_type=F32)
                s = jnp.where(valid, s + alibi[2 * hp + sub], MASK_VALUE)
                m = jnp.max(s, axis=-1, keepdims=True)
                p = jnp.exp(s - m)
                den = jnp.sum(p, axis=-1, keepdims=True)
                o = _dot(p.astype(BF16), vp) / den
                l = jnp.broadcast_to(m + jnp.log(den), o.shape)
                o_pair = o if sub == 0 else jnp.where(low, o_pair, o)
                l_pair = l if sub == 0 else jnp.where(low, l_pair, l)
            o_ref[0, rows, sl] = o_pair.astype(BF16)
            lse_ref[0, rows, sl] = l_pair


def _attn_out_ffn_kernel(x_ref, o0_ref, o1_ref, o2_ref, l0_ref, l1_ref, l2_ref, wo_ref, gf_ref,
                         w1_ref, w3_ref, w2_ref, out_ref, so_ref, sl_ref, hid_ref, *, fc):
    aw = ATTN_WIDTH
    ls, os_ = [], []
    for gi, (o_ref, l_ref) in enumerate(((o0_ref, l0_ref), (o1_ref, l1_ref), (o2_ref, l2_ref))):
        dil = ATTN_GROUPS[gi][1]
        if dil == 1:
            os_.append(o_ref[0].astype(F32))
            ls.append(l_ref[0])
            continue
        n = o_ref.shape[1]
        nlc = aw // LANES
        for r in range(dil):
            ov = o_ref[0, :, r * aw:(r + 1) * aw].astype(F32)
            lv = l_ref[0, :, r * aw:(r + 1) * aw]
            for c in range(nlc):
                so_ref[gi - 1, c, pl.ds(r, n, stride=dil), :] = ov[:, c * LANES:(c + 1) * LANES]
                sl_ref[gi - 1, c, pl.ds(r, n, stride=dil), :] = lv[:, c * LANES:(c + 1) * LANES]
        os_.append(jnp.concatenate([so_ref[gi - 1, c] for c in range(nlc)], axis=1))
        ls.append(jnp.concatenate([sl_ref[gi - 1, c] for c in range(nlc)], axis=1))
    m = jnp.maximum(jnp.maximum(ls[0], ls[1]), ls[2])
    es = [jnp.exp(l - m) for l in ls]
    den = es[0] + es[1] + es[2]
    o = (es[0] * os_[0] + es[1] * os_[1] + es[2] * os_[2]) / den
    x1 = x_ref[0] + _dot(o.astype(BF16), wo_ref[...])
    hb = _rms(x1, gf_ref[...]).astype(BF16)
    out_ref[0] = x1 + _swiglu(hb, w1_ref.at[0], w3_ref.at[0], w2_ref.at[0], hid_ref, fc)


def _alibi_slopes():
    n = len(ATTN_GROUPS) * ATTN_HEADS
    s = np.float32(2.0) ** (np.float32(-8.0) * np.arange(1, n + 1, dtype=np.float32) / np.float32(n))
    return [float(v) for v in s]


def _attention_ffn_layer(x, g_mix, w_qkv, w_out, g_ffn, w1, w3, w2, layer, tm):
    bsz, seq, d = x.shape
    nqkv = w_qkv.shape[1]
    aw = ATTN_WIDTH
    gw = 3 * aw
    assert nqkv == len(ATTN_GROUPS) * gw
    qkvs = pl.pallas_call(
        functools.partial(_qkv_kernel, nc=aw),
        grid=(bsz, seq // tm),
        in_specs=[pl.BlockSpec((1, tm, d), lambda b, i: (b, i, 0)), _full((1, d)), _full((d, nqkv))],
        out_specs=[pl.BlockSpec((1, tm // dil, dil * gw), lambda b, i: (b, i, 0)) for _, dil in ATTN_GROUPS],
        out_shape=[jax.ShapeDtypeStruct((bsz, seq // dil, dil * gw), BF16) for _, dil in ATTN_GROUPS],
        scratch_shapes=[pltpu.VMEM((d // LANES, tm, LANES), F32)],
        compiler_params=_cparams("parallel", "parallel"),
        name="qkv_proj",
    )(x, g_mix.reshape(1, d), w_qkv.astype(BF16))

    slopes = _alibi_slopes()
    outs, lses = [], []
    for gi, (window, dil) in enumerate(ATTN_GROUPS):
        radius = window // (2 * dil)
        seg = seq // dil
        tb = min(512, seg)
        tq = min(128, seg)
        hr = tb // radius
        last = seg // radius - 1
        view = qkvs[gi]

        def col(part, r):
            return r * 3 + part

        def cur(part):
            return pl.BlockSpec((1, tb, aw), lambda b, r, j, part=part: (b, j, col(part, r)))

        def prev(part):
            return pl.BlockSpec((1, radius, aw),
                                lambda b, r, j, part=part: (b, jnp.maximum(j * hr - 1, 0), col(part, r)))

        def nxt(part):
            return pl.BlockSpec((1, radius, aw),
                                lambda b, r, j, part=part: (b, jnp.minimum((j + 1) * hr, last), col(part, r)))

        o, lse = pl.pallas_call(
            functools.partial(_attn_kernel, tq=tq, seg=seg, dil=dil, radius=radius,
                              slopes=slopes[gi * ATTN_HEADS:(gi + 1) * ATTN_HEADS]),
            grid=(bsz, dil, seg // tb),
            in_specs=[cur(0), prev(1), cur(1), nxt(1), prev(2), cur(2), nxt(2)],
            out_specs=[pl.BlockSpec((1, tb, aw), lambda b, r, j: (b, j, r)),
                       pl.BlockSpec((1, tb, aw), lambda b, r, j: (b, j, r))],
            out_shape=[jax.ShapeDtypeStruct((bsz, seg, dil * aw), BF16),
                       jax.ShapeDtypeStruct((bsz, seg, dil * aw), F32)],
            compiler_params=_cparams("parallel", "parallel", "parallel"),
            name=f"dilated_attn_{dil}",
        )(view, view, view, view, view, view, view)
        outs.append(o)
        lses.append(lse)

    f = w1.shape[2]
    row = lambda width: pl.BlockSpec((1, tm, width), lambda b, i: (b, i, 0))
    dilated = [pl.BlockSpec((1, tm // dil, dil * aw), lambda b, i: (b, i, 0)) for _, dil in ATTN_GROUPS]
    return pl.pallas_call(
        functools.partial(_attn_out_ffn_kernel, fc=256),
        grid=(bsz, seq // tm),
        in_specs=[row(d)] + dilated + dilated + [
            _full((aw, d)), _full((1, d)),
            _stacked((d, f), layer), _stacked((d, f), layer), _stacked((f, d), layer)],
        out_specs=row(d),
        out_shape=jax.ShapeDtypeStruct(x.shape, F32),
        scratch_shapes=[pltpu.VMEM((len(ATTN_GROUPS) - 1, aw // LANES, tm, LANES), F32),
                        pltpu.VMEM((len(ATTN_GROUPS) - 1, aw // LANES, tm, LANES), F32),
                        pltpu.VMEM((tm, f), BF16)],
        compiler_params=_cparams("parallel", "parallel"),
        name="attn_out_ffn",
    )(x, *outs, *lses, w_out.astype(BF16), g_ffn.reshape(1, d), w1, w3, w2)


def _conv_glu_kernel(x_ref, g_ref, w_ref, u_ref):
    d = x_ref.shape[2]
    hb = _rms(x_ref[0], g_ref[...]).astype(BF16)
    a = _dot(hb, w_ref[:, :d])
    b = _dot(hb, w_ref[:, d:])
    u_ref[0] = a * jax.nn.sigmoid(b)


def _conv_out_kernel(x_ref, up_ref, uc_ref, un_ref, wdw_ref, bdw_ref, lg_ref, lb_ref, w2_ref, gf_ref, wrt_ref,
                     xo_ref, h_ref, idx_ref, gate_ref, ext_ref, *, ts, seq):
    i = pl.program_id(1)
    n_ext = ts + 2 * HALO
    ue = jnp.concatenate([up_ref[0], uc_ref[0], un_ref[0]], axis=0)
    pos = i * ts - HALO + lax.broadcasted_iota(jnp.int32, (n_ext, 1), 0)
    ext_ref[0] = jnp.where((pos >= 0) & (pos < seq), ue, 0.0)
    for r in range(1, SUBLANES):
        ext_ref[r, 0:n_ext - SUBLANES, :] = ext_ref[0, r:r + n_ext - SUBLANES, :]
    half = CONV_WIDTH // 2
    acc = None
    for k in range(CONV_WIDTH):
        start = HALO + k - half
        r = start % SUBLANES
        term = ext_ref[r, start - r:start - r + ts, :] * wdw_ref[k:k + 1, :]
        acc = term if acc is None else acc + term
    u = acc + bdw_ref[...]
    mu = jnp.mean(u, axis=-1, keepdims=True)
    uc = u - mu
    var = jnp.mean(uc * uc, axis=-1, keepdims=True)
    z = uc * lax.rsqrt(var + LN_EPS) * lg_ref[...] + lb_ref[...]
    z = (z * jax.nn.sigmoid(z)).astype(BF16)
    x1 = x_ref[0] + _dot(z, w2_ref[...])
    _moe_prologue(x1, gf_ref, wrt_ref, xo_ref, h_ref, idx_ref, gate_ref)


def _conv_layer(x, g_mix, w_pw1, w_dw, b_dw, ln_g, ln_b, w_pw2, g_ffn, w_router, tm, ts):
    bsz, seq, d = x.shape
    n_tok = bsz * seq
    u = pl.pallas_call(
        _conv_glu_kernel,
        grid=(bsz, seq // tm),
        in_specs=[pl.BlockSpec((1, tm, d), lambda b, i: (b, i, 0)), _full((1, d)), _full((d, 2 * d))],
        out_specs=pl.BlockSpec((1, tm, d), lambda b, i: (b, i, 0)),
        out_shape=jax.ShapeDtypeStruct(x.shape, F32),
        compiler_params=_cparams("parallel", "parallel"),
        name="conv_glu",
    )(x, g_mix.reshape(1, d), w_pw1.astype(BF16))

    nt = seq // ts
    vec = lambda a: a.reshape(1, d)
    return pl.pallas_call(
        functools.partial(_conv_out_kernel, ts=ts, seq=seq),
        grid=(bsz, nt),
        in_specs=[pl.BlockSpec((1, ts, d), lambda b, i: (b, i, 0))] + _halo_specs(ts, seq, d) + [
            _full((CONV_WIDTH, d)), _full((1, d)), _full((1, d)), _full((1, d)), _full((d, d)),
            _full((1, d)), _full((N_EXPERTS, d))],
        out_specs=[pl.BlockSpec((1, ts, d), lambda b, i: (b, i, 0)),
                   pl.BlockSpec((1, ts, d), lambda b, i: (b, i, 0)),
                   pl.BlockSpec((TOP_K, ts), lambda b, i: (0, b * nt + i)),
                   pl.BlockSpec((TOP_K, ts), lambda b, i: (0, b * nt + i))],
        out_shape=[jax.ShapeDtypeStruct(x.shape, F32), jax.ShapeDtypeStruct(x.shape, BF16),
                   jax.ShapeDtypeStruct((TOP_K, n_tok), jnp.int32),
                   jax.ShapeDtypeStruct((TOP_K, n_tok), F32)],
        scratch_shapes=[pltpu.VMEM((SUBLANES, ts + 2 * HALO, d), F32)],
        compiler_params=_cparams("parallel", "parallel"),
        name="conv_out",
    )(x, u, u, u, w_dw, vec(b_dw), vec(ln_g), vec(ln_b), w_pw2.astype(BF16), vec(g_ffn), w_router.T)


def kernel(x, g_mix, g_ffn, g_final, a_w_grp, a_scale, b_w_out, c_w_qkv, c_w_out, d_w_pw1, d_w_dw, d_b_dw,
           d_ln_g, d_ln_b, d_w_pw2, ffn_w1, ffn_w3, ffn_w2, moe_router, moe_w1, moe_w3, moe_w2):
    assert g_mix.shape[0] == 4, "one layer of each mixer kind"
    tm = min(512, x.shape[1])
    ffn_w = [w.astype(BF16) for w in (ffn_w1, ffn_w3, ffn_w2)]
    moe_w = [w.astype(BF16) for w in (moe_w1, moe_w3, moe_w2)]
    x = _pool_ffn_layer(x, g_mix[0], a_w_grp[0], a_scale[0], g_ffn[0], *ffn_w, 0, tm)
    x, h, idx, gates = _fourier_layer(x, g_mix[1], b_w_out[0], g_ffn[1], moe_router[0], tm)
    x = _moe_layer(x, h, idx, gates, *moe_w, 0)
    x = _attention_ffn_layer(x, g_mix[2], c_w_qkv[0], c_w_out[0], g_ffn[2], *ffn_w, 1, tm)
    x, h, idx, gates = _conv_layer(x, g_mix[3], d_w_pw1[0], d_w_dw[0], d_b_dw[0], d_ln_g[0], d_ln_b[0],
                                   d_w_pw2[0], g_ffn[3], moe_router[1], tm, tm)
    return _moe_layer(x, h, idx, gates, *moe_w, 1, g_final=g_final)
```

```python
import functools
import math

import numpy as np
import jax
import jax.numpy as jnp
from jax import lax
from jax.experimental import pallas as pl
from jax.experimental.pallas import tpu as pltpu

F32 = jnp.float32
BF16 = jnp.bfloat16

RMS_EPS = 1e-6
LN_EPS = 1e-5
MASK_VALUE = -1e30
POOL_WINDOWS = (2, 4, 8, 16)
FOURIER_GROUPS = 4
ATTN_GROUPS = ((128, 1), (512, 4), (2048, 16))
ATTN_HEADS = 8
ATTN_HEAD_DIM = 64
ATTN_WIDTH = ATTN_HEADS * ATTN_HEAD_DIM
CONV_WIDTH = 31
N_EXPERTS = 8
TOP_K = 2

LANES = 128
SUBLANES = 8
HALO = 16
FFT_N2 = 128
MOE_ROWS = 512
MOE_CHUNK = 512
SEG_ALIGN = 16
SEG_PIECES = (512, 256, 128, 64, 32, 16)
LOCAL_ROWS = 1152
VMEM_LIMIT_BYTES = 56 * 1024 * 1024


def _cparams(*sem):
    return pltpu.CompilerParams(dimension_semantics=sem, vmem_limit_bytes=VMEM_LIMIT_BYTES)


def _dot(a, b):
    return jnp.dot(a, b, preferred_element_type=F32)


def _rms(x, g):
    ms = jnp.mean(x * x, axis=-1, keepdims=True)
    return x * lax.rsqrt(ms + RMS_EPS) * g


def _swiglu(hb, w1_ref, w3_ref, w2_ref, g_ref, fc):
    f = w1_ref.shape[-1]
    for c in range(0, f, fc):
        a = _dot(hb, w1_ref[:, c:c + fc])
        b = _dot(hb, w3_ref[:, c:c + fc])
        g_ref[:, c:c + fc] = (a * jax.nn.sigmoid(a) * b).astype(BF16)
    return _dot(g_ref[...], w2_ref[...])


def _route(hb, wrt_ref, idx_ref, gate_ref):
    logits = lax.dot_general(wrt_ref[...], hb, (((1,), (1,)), ((), ())), preferred_element_type=F32)
    e_iota = lax.broadcasted_iota(jnp.int32, logits.shape, 0)
    m1 = jnp.max(logits, axis=0, keepdims=True)
    i1 = jnp.min(jnp.where(logits == m1, e_iota, N_EXPERTS), axis=0, keepdims=True)
    rest = jnp.where(e_iota == i1, -jnp.inf, logits)
    m2 = jnp.max(rest, axis=0, keepdims=True)
    i2 = jnp.min(jnp.where(rest == m2, e_iota, N_EXPERTS), axis=0, keepdims=True)
    e2 = jnp.exp(m2 - m1)
    den = 1.0 + e2
    idx_ref[0:1, :] = i1
    idx_ref[1:2, :] = i2
    gate_ref[0:1, :] = 1.0 / den
    gate_ref[1:2, :] = e2 / den


def _moe_prologue(x1, gf_ref, wrt_ref, x_out_ref, h_ref, idx_ref, gate_ref):
    x_out_ref[0] = x1
    hb = _rms(x1, gf_ref[...]).astype(BF16)
    h_ref[0] = hb
    _route(hb, wrt_ref, idx_ref, gate_ref)


def _pool_ffn_kernel(xp_ref, xc_ref, xn_ref, gm_ref, wg_ref, sc_ref, gf_ref, w1_ref, w3_ref, w2_ref,
                     o_ref, hid_ref, *, ts, seq, fc):
    i = pl.program_id(1)
    xc = xc_ref[0]
    d = xc.shape[1]
    gd = d // len(POOL_WINDOWS)
    n_ext = ts + 2 * HALO
    xe = jnp.concatenate([xp_ref[0], xc, xn_ref[0]], axis=0)
    he = _rms(xe, gm_ref[...])
    pos = i * ts - HALO + lax.broadcasted_iota(jnp.int32, (n_ext, 1), 0)
    he = jnp.where((pos >= 0) & (pos < seq), he, 0.0)

    s = he + pltpu.roll(he, 1, 0)
    sums = [s]
    half = 1
    for _ in POOL_WINDOWS[1:]:
        s = s[:, gd:]
        s = pltpu.roll(s, half, 0) + pltpu.roll(s, n_ext - half, 0)
        sums.append(s)
        half *= 2

    tpos = i * ts + lax.broadcasted_iota(jnp.int32, (ts, 1), 0)
    ys = []
    for gi, w in enumerate(POOL_WINDOWS):
        cnt = jnp.minimum(tpos + w // 2, seq) - jnp.maximum(tpos - w // 2, 0)
        mean = sums[gi][HALO:HALO + ts, :gd] / cnt.astype(F32)
        pooled = mean - he[HALO:HALO + ts, gi * gd:(gi + 1) * gd]
        ys.append(_dot(pooled.astype(BF16), wg_ref[gi]))
    x1 = xc + jnp.concatenate(ys, axis=1) * sc_ref[...]
    hb = _rms(x1, gf_ref[...]).astype(BF16)
    o_ref[0] = x1 + _swiglu(hb, w1_ref.at[0], w3_ref.at[0], w2_ref.at[0], hid_ref, fc)


def _halo_specs(ts, seq, d):
    r = ts // HALO
    last = seq // HALO - 1
    return [
        pl.BlockSpec((1, HALO, d), lambda b, i: (b, jnp.maximum(i * r - 1, 0), 0)),
        pl.BlockSpec((1, ts, d), lambda b, i: (b, i, 0)),
        pl.BlockSpec((1, HALO, d), lambda b, i: (b, jnp.minimum((i + 1) * r, last), 0)),
    ]


def _full(shape):
    n = len(shape)
    return pl.BlockSpec(shape, lambda *_: (0,) * n)


def _stacked(shape, layer):
    n = len(shape)
    return pl.BlockSpec((1,) + tuple(shape), lambda *_: (layer,) + (0,) * n)


def _pool_ffn_layer(x, g_mix, w_grp, scale, g_ffn, w1, w3, w2, layer, ts):
    bsz, seq, d = x.shape
    f = w1.shape[2]
    kern = functools.partial(_pool_ffn_kernel, ts=ts, seq=seq, fc=256)
    return pl.pallas_call(
        kern,
        grid=(bsz, seq // ts),
        in_specs=_halo_specs(ts, seq, d) + [
            _full((1, d)), _full(w_grp.shape), _full((1, d)), _full((1, d)),
            _stacked((d, f), layer), _stacked((d, f), layer), _stacked((f, d), layer)],
        out_specs=pl.BlockSpec((1, ts, d), lambda b, i: (b, i, 0)),
        out_shape=jax.ShapeDtypeStruct(x.shape, F32),
        scratch_shapes=[pltpu.VMEM((ts, f), BF16)],
        compiler_params=_cparams("parallel", "parallel"),
        name="pool_ffn",
    )(x, x, x, g_mix.reshape(1, d), w_grp.astype(BF16), scale.reshape(1, d), g_ffn.reshape(1, d), w1, w3, w2)


def _fft_stage1_kernel(x_ref, g_ref, f1_ref, o_ref, s_ref, *, nseg):
    n1, d = x_ref.shape[1], x_ref.shape[3]
    nlc = d // LANES
    for c in range(nlc):
        s_ref[c] = x_ref[0, :, :, c * LANES:(c + 1) * LANES].reshape(n1 * nseg, LANES)
    for j in range(nseg):
        xj = jnp.concatenate([s_ref[c, pl.ds(j, n1, stride=nseg), :] for c in range(nlc)], axis=1)
        h = _rms(xj, g_ref[...]).astype(BF16)
        o_ref[0, j] = _dot(f1_ref[...], h).astype(BF16)


def _fft_stage2_kernel(br_ref, bi_ref, m_ref, o_ref, s_ref, *, nk):
    n2, d = FFT_N2, br_ref.shape[3]
    nlc = d // LANES
    for p, ref in enumerate((br_ref, bi_ref)):
        for c in range(nlc):
            s_ref[p, c] = ref[0, :, :, c * LANES:(c + 1) * LANES].reshape(n2 * nk, LANES).astype(F32)
    for j in range(nk):
        rhs = jnp.concatenate(
            [jnp.concatenate([s_ref[p, c, pl.ds(j, n2, stride=nk), :] for c in range(nlc)], axis=1)
             for p in range(2)], axis=0).astype(BF16)
        res = _dot(m_ref[j], rhs)
        for p in range(2):
            for c in range(nlc):
                s_ref[p, c, pl.ds(j, n2, stride=nk), :] = res[p * n2:(p + 1) * n2, c * LANES:(c + 1) * LANES]
    for p in range(2):
        for c in range(nlc):
            o_ref[0, p, :, :, c * LANES:(c + 1) * LANES] = s_ref[p, c].astype(BF16).reshape(n2, nk, LANES)


def _fourier_out_kernel(x_ref, ar_ref, ai_ref, cc_ref, sc_ref, wo_ref, gf_ref, wrt_ref,
                        xo_ref, h_ref, idx_ref, gate_ref, *, norm):
    d = x_ref.shape[2]
    gd = d // FOURIER_GROUPS
    fs = []
    for gi in range(FOURIER_GROUPS):
        sl = slice(gi * gd, (gi + 1) * gd)
        fs.append(_dot(ar_ref[0, 0, :, sl].astype(BF16), cc_ref[...])
                  + _dot(ai_ref[0, 0, :, sl].astype(BF16), sc_ref[...]))
    f = (jnp.concatenate(fs, axis=1) * norm).astype(BF16)
    x1 = x_ref[0] + _dot(f, wo_ref[...])
    _moe_prologue(x1, gf_ref, wrt_ref, xo_ref, h_ref, idx_ref, gate_ref)


def _dft_tables(seq, gd):
    n2 = FFT_N2
    n1 = seq // n2
    k = np.arange(n1)
    ang1 = 2.0 * np.pi * ((k[:, None] * k[None, :]) % n1) / n1
    f1 = np.concatenate([np.cos(ang1), -np.sin(ang1)], axis=0)
    c = np.arange(gd)
    angc = 2.0 * np.pi * ((c[:, None] * c[None, :]) % gd) / gd
    return (jnp.asarray(f1, BF16), jnp.asarray(np.cos(angc), BF16), jnp.asarray(np.sin(angc), BF16))


def _twiddled_dft(seq):
    n2 = FFT_N2
    n1 = seq // n2
    k1 = lax.broadcasted_iota(jnp.int32, (n1, n2, n2), 0)
    k2 = lax.broadcasted_iota(jnp.int32, (n1, n2, n2), 1)
    a = lax.broadcasted_iota(jnp.int32, (n1, n2, n2), 2)
    ang = ((a * (n1 * k2 + k1)) % seq).astype(F32) * (2.0 * math.pi / seq)
    mr = jnp.cos(ang)
    mi = -jnp.sin(ang)
    top = jnp.concatenate([mr, -mi], axis=2)
    bot = jnp.concatenate([mi, mr], axis=2)
    return jnp.concatenate([top, bot], axis=1).astype(BF16)


def _fourier_layer(x, g_mix, w_out, g_ffn, w_router, tm):
    bsz, seq, d = x.shape
    n2 = FFT_N2
    n1 = seq // n2
    gd = d // FOURIER_GROUPS
    f1, cc, sc = _dft_tables(seq, gd)
    mtab = _twiddled_dft(seq)
    nseg = 8
    b1 = pl.pallas_call(
        functools.partial(_fft_stage1_kernel, nseg=nseg),
        grid=(bsz, n2 // nseg),
        in_specs=[pl.BlockSpec((1, n1, nseg, d), lambda b, t: (b, 0, t, 0)), _full((1, d)), _full(f1.shape)],
        out_specs=pl.BlockSpec((1, nseg, 2 * n1, d), lambda b, t: (b, t, 0, 0)),
        out_shape=jax.ShapeDtypeStruct((bsz, n2, 2 * n1, d), BF16),
        scratch_shapes=[pltpu.VMEM((d // LANES, n1 * nseg, LANES), F32)],
        compiler_params=_cparams("parallel", "parallel"),
        name="fft_stage1",
    )(x.reshape(bsz, n1, n2, d), g_mix.reshape(1, d), f1)

    nk = 16
    a = pl.pallas_call(
        functools.partial(_fft_stage2_kernel, nk=nk),
        grid=(bsz, n1 // nk),
        in_specs=[pl.BlockSpec((1, n2, nk, d), lambda b, t: (b, 0, t, 0)),
                  pl.BlockSpec((1, n2, nk, d), lambda b, t: (b, 0, n1 // nk + t, 0)),
                  pl.BlockSpec((nk, 2 * n2, 2 * n2), lambda b, t: (t, 0, 0))],
        out_specs=pl.BlockSpec((1, 2, n2, nk, d), lambda b, t: (b, 0, 0, t, 0)),
        out_shape=jax.ShapeDtypeStruct((bsz, 2, n2, n1, d), BF16),
        scratch_shapes=[pltpu.VMEM((2, d // LANES, n2 * nk, LANES), F32)],
        compiler_params=_cparams("parallel", "parallel"),
        name="fft_stage2",
    )(b1, b1, mtab)

    av = a.reshape(bsz, 2, seq, d)
    n_tok = bsz * seq
    nt = seq // tm
    outs = pl.pallas_call(
        functools.partial(_fourier_out_kernel, norm=1.0 / math.sqrt(seq * gd)),
        grid=(bsz, nt),
        in_specs=[pl.BlockSpec((1, tm, d), lambda b, i: (b, i, 0)),
                  pl.BlockSpec((1, 1, tm, d), lambda b, i: (b, 0, i, 0)),
                  pl.BlockSpec((1, 1, tm, d), lambda b, i: (b, 1, i, 0)),
                  _full((gd, gd)), _full((gd, gd)), _full((d, d)), _full((1, d)), _full((N_EXPERTS, d))],
        out_specs=[pl.BlockSpec((1, tm, d), lambda b, i: (b, i, 0)),
                   pl.BlockSpec((1, tm, d), lambda b, i: (b, i, 0)),
                   pl.BlockSpec((TOP_K, tm), lambda b, i: (0, b * nt + i)),
                   pl.BlockSpec((TOP_K, tm), lambda b, i: (0, b * nt + i))],
        out_shape=[jax.ShapeDtypeStruct(x.shape, F32), jax.ShapeDtypeStruct(x.shape, BF16),
                   jax.ShapeDtypeStruct((TOP_K, n_tok), jnp.int32),
                   jax.ShapeDtypeStruct((TOP_K, n_tok), F32)],
        compiler_params=_cparams("parallel", "parallel"),
        name="fourier_out",
    )(x, av, av, cc, sc, w_out.astype(BF16), g_ffn.reshape(1, d), w_router.T.astype(BF16))
    return outs


def _moe_plan(idx, n_tok):
    chunk, rows = MOE_CHUNK, MOE_ROWS
    nc = n_tok // chunk
    t_iota = jnp.arange(chunk, dtype=jnp.int32)
    before = (t_iota[:, None] < t_iota[None, :]).astype(BF16)
    lrow_lanes, seg_len = pl.pallas_call(
        _moe_rank_kernel,
        grid=(nc,),
        in_specs=[pl.BlockSpec((TOP_K, chunk), lambda c: (0, c)), _full((chunk, chunk))],
        out_specs=[pl.BlockSpec((1, TOP_K, chunk), lambda c: (c, 0, 0)),
                   pl.BlockSpec((1, N_EXPERTS, LANES), lambda c: (c, 0, 0))],
        out_shape=[jax.ShapeDtypeStruct((nc, TOP_K, chunk), jnp.int32),
                   jax.ShapeDtypeStruct((nc, N_EXPERTS, LANES), jnp.int32)],
        compiler_params=_cparams("parallel"),
        name="moe_rank",
    )(idx, before)
    seg_len = seg_len[:, :, 0]
    loff = jnp.cumsum(seg_len, axis=1) - seg_len
    region = (jnp.sum(seg_len, axis=0) + rows - 1) // rows * rows
    pend = jnp.cumsum(region)
    goff = (pend - region)[None, :] + jnp.cumsum(seg_len, axis=0) - seg_len
    n_blocks = (n_tok * TOP_K + nc * N_EXPERTS * (SEG_ALIGN - 1)) // rows + N_EXPERTS
    blk = jnp.arange(n_blocks, dtype=jnp.int32)
    block_exp = jnp.minimum(jnp.sum((blk[:, None] * rows >= pend[None, :]).astype(jnp.int32), axis=1),
                            N_EXPERTS - 1)
    return dict(
        n_blocks=n_blocks,
        seg_len=seg_len.reshape(-1), loff=loff.reshape(-1).astype(jnp.int32), goff=goff.reshape(-1).astype(jnp.int32),
        gap_off=(pend - region + jnp.sum(seg_len, axis=0)).astype(jnp.int32),
        gap_len=(region - jnp.sum(seg_len, axis=0)).astype(jnp.int32),
        lrow_lanes=lrow_lanes,
        lrow_cols=lrow_lanes.transpose(0, 2, 1).reshape(n_tok, TOP_K),
        n_used=(pend[-1] // rows).reshape(1).astype(jnp.int32), block_exp=block_exp)


def _moe_rank_kernel(idx_ref, before_ref, lrow_ref, len_ref):
    e_iota = lax.broadcasted_iota(jnp.int32, (N_EXPERTS, idx_ref.shape[1]), 0)
    picks = [jnp.where(idx_ref[k:k + 1, :] == e_iota, 1.0, 0.0) for k in range(TOP_K)]
    total = picks[0]
    for m in picks[1:]:
        total = total + m
    earlier = _dot(total.astype(BF16), before_ref[...])
    cnt = jnp.sum(total, axis=1, keepdims=True).astype(jnp.int32)
    seg = (cnt + (SEG_ALIGN - 1)) & (-SEG_ALIGN)
    e_col = lax.broadcasted_iota(jnp.int32, (N_EXPERTS, 1), 0)
    loff = jnp.zeros((N_EXPERTS, 1), jnp.int32)
    for e in range(N_EXPERTS - 1):
        loff = loff + jnp.where(e_col > e, seg[e:e + 1, :], 0)
    base = loff.astype(F32) + earlier
    for k in range(TOP_K):
        lrow_ref[0, k:k + 1, :] = jnp.sum(picks[k] * base, axis=0, keepdims=True).astype(jnp.int32)
        base = base + picks[k]
    len_ref[0] = jnp.broadcast_to(seg, (N_EXPERTS, LANES))


def _segment_copies(c, len_ref, loff_ref, goff_ref, make, act):
    for e in range(N_EXPERTS):
        ln = len_ref[c * N_EXPERTS + e]
        lo = loff_ref[c * N_EXPERTS + e]
        go = goff_ref[c * N_EXPERTS + e]
        for size in SEG_PIECES:
            done = ln & (-2 * size)

            @pl.when((ln & size) != 0)
            def _():
                act(make(pl.multiple_of(lo + done, SEG_ALIGN), pl.multiple_of(go + done, SEG_ALIGN), size))


def _zero_fill_copies(gap_off_ref, gap_len_ref, nu_ref, zero_ref, xb_ref, sem, act):
    rows = zero_ref.shape[0]
    for e in range(N_EXPERTS):
        ln = gap_len_ref[e]
        off = gap_off_ref[e]
        for size in SEG_PIECES[1:]:
            done = ln & (-2 * size)

            @pl.when((ln & size) != 0)
            def _():
                act(pltpu.make_async_copy(zero_ref.at[pl.ds(0, size)],
                                          xb_ref.at[pl.ds(pl.multiple_of(off + done, SEG_ALIGN), size)], sem))

    def block(b, carry):
        act(pltpu.make_async_copy(zero_ref, xb_ref.at[pl.ds(pl.multiple_of(b * rows, rows), rows)], sem))
        return carry

    lax.fori_loop(nu_ref[0], xb_ref.shape[0] // rows, block, 0)


def _dispatch_kernel(len_ref, loff_ref, goff_ref, gap_off_ref, gap_len_ref, nu_ref, h_ref, lrow_ref, xb_ref,
                     stage_ref, zero_ref, sem, zsem):
    c = pl.program_id(0)
    slot = c % 2
    n_local, chunk = stage_ref.shape[1], h_ref.shape[0]

    @pl.when(c == 0)
    def _():
        zero_ref[...] = jnp.zeros_like(zero_ref)
        _zero_fill_copies(gap_off_ref, gap_len_ref, nu_ref, zero_ref, xb_ref, zsem, lambda cp: cp.start())

    def make(slot_):
        def _make(lr, gr, size):
            return pltpu.make_async_copy(stage_ref.at[slot_, pl.ds(lr, size)], xb_ref.at[pl.ds(gr, size)],
                                         sem.at[slot_])
        return _make

    r_iota = lax.broadcasted_iota(jnp.int32, (n_local, chunk), 0)
    sel = (lrow_ref[0, 0:1, :] == r_iota) | (lrow_ref[0, 1:2, :] == r_iota)
    stage_ref[slot] = _dot(jnp.where(sel, 1.0, 0.0).astype(BF16), h_ref[...]).astype(BF16)

    @pl.when(c > 0)
    def _():
        _segment_copies(c - 1, len_ref, loff_ref, goff_ref, make(1 - slot), lambda cp: cp.wait())

    _segment_copies(c, len_ref, loff_ref, goff_ref, make(slot), lambda cp: cp.start())

    @pl.when(c == pl.num_programs(0) - 1)
    def _():
        _segment_copies(c, len_ref, loff_ref, goff_ref, make(slot), lambda cp: cp.wait())
        _zero_fill_copies(gap_off_ref, gap_len_ref, nu_ref, zero_ref, xb_ref, zsem, lambda cp: cp.wait())


def _expert_kernel(be_ref, nu_ref, xb_ref, w1_ref, w3_ref, w2_ref, y_ref, acc_ref, hid_ref, *, nf, fc):
    b = pl.program_id(0)
    f = pl.program_id(1)

    @pl.when(b < nu_ref[0])
    def _():
        y = _swiglu(xb_ref[...], w1_ref.at[0, 0], w3_ref.at[0, 0], w2_ref.at[0, 0], hid_ref, fc)

        @pl.when(f == 0)
        def _():
            acc_ref[...] = y

        @pl.when(f > 0)
        def _():
            acc_ref[...] += y

        @pl.when(f == nf - 1)
        def _():
            y_ref[...] = acc_ref[...].astype(BF16)

    @pl.when((b >= nu_ref[0]) & (f == nf - 1))
    def _():
        y_ref[...] = jnp.zeros_like(y_ref)


def _combine_kernel(len_ref, loff_ref, goff_ref, y_ref, lcol_ref, gcol_ref, x_ref, g_ref, o_ref, ybuf_ref, sem,
                    *, final_norm):
    c = pl.program_id(0)
    nc = pl.num_programs(0)
    slot = c % 2
    chunk, n_local = x_ref.shape[0], ybuf_ref.shape[1]

    def make(slot_):
        def _make(lr, gr, size):
            return pltpu.make_async_copy(y_ref.at[pl.ds(gr, size)], ybuf_ref.at[slot_, pl.ds(lr, size)],
                                         sem.at[slot_])
        return _make

    @pl.when(c == 0)
    def _():
        ybuf_ref[...] = jnp.zeros_like(ybuf_ref)
        _segment_copies(c, len_ref, loff_ref, goff_ref, make(slot), lambda cp: cp.start())

    @pl.when(c + 1 < nc)
    def _():
        _segment_copies(c + 1, len_ref, loff_ref, goff_ref, make(1 - slot), lambda cp: cp.start())

    _segment_copies(c, len_ref, loff_ref, goff_ref, make(slot), lambda cp: cp.wait())

    yl = ybuf_ref[slot]
    r_iota = lax.broadcasted_iota(jnp.int32, (chunk, n_local), 1)
    xo = x_ref[...]
    for k in range(TOP_K):
        sel = lcol_ref[:, k:k + 1] == r_iota
        xo = xo + gcol_ref[:, k:k + 1] * _dot(jnp.where(sel, 1.0, 0.0).astype(BF16), yl)
    o_ref[...] = _rms(xo, g_ref[...]) if final_norm else xo


def _moe_layer(x, h, idx, gates, w1, w3, w2, layer, g_final=None):
    bsz, seq, d = x.shape
    n_tok = bsz * seq
    f = w1.shape[3]
    rows, chunk = MOE_ROWS, MOE_CHUNK
    nc = n_tok // chunk
    plan = _moe_plan(idx, n_tok)
    n_blocks = plan['n_blocks']
    n_rows = n_blocks * rows
    seg = (plan['seg_len'], plan['loff'], plan['goff'])

    xb = pl.pallas_call(
        _dispatch_kernel,
        grid_spec=pltpu.PrefetchScalarGridSpec(
            num_scalar_prefetch=6, grid=(nc,),
            in_specs=[pl.BlockSpec((chunk, d), lambda c, *_: (c, 0)),
                      pl.BlockSpec((1, TOP_K, chunk), lambda c, *_: (c, 0, 0))],
            out_specs=pl.BlockSpec(memory_space=pl.ANY),
            scratch_shapes=[pltpu.VMEM((2, LOCAL_ROWS, d), BF16), pltpu.VMEM((rows, d), BF16),
                            pltpu.SemaphoreType.DMA((2,)), pltpu.SemaphoreType.DMA(())]),
        out_shape=jax.ShapeDtypeStruct((n_rows, d), BF16),
        compiler_params=_cparams("arbitrary"),
        name="moe_dispatch",
    )(*seg, plan['gap_off'], plan['gap_len'], plan['n_used'], h.reshape(n_tok, d), plan['lrow_lanes'])

    nf = 2
    fblk = f // nf

    def blk_idx(b, nu):
        return jnp.minimum(b, nu[0] - 1)

    def f_idx(b, fi, nu):
        return jnp.where(b < nu[0], fi, nf - 1)

    y = pl.pallas_call(
        functools.partial(_expert_kernel, nf=nf, fc=256),
        grid_spec=pltpu.PrefetchScalarGridSpec(
            num_scalar_prefetch=2, grid=(n_blocks, nf),
            in_specs=[pl.BlockSpec((rows, d), lambda b, fi, be, nu: (blk_idx(b, nu), 0)),
                      pl.BlockSpec((1, 1, d, fblk),
                                   lambda b, fi, be, nu: (layer, be[blk_idx(b, nu)], 0, f_idx(b, fi, nu))),
                      pl.BlockSpec((1, 1, d, fblk),
                                   lambda b, fi, be, nu: (layer, be[blk_idx(b, nu)], 0, f_idx(b, fi, nu))),
                      pl.BlockSpec((1, 1, fblk, d),
                                   lambda b, fi, be, nu: (layer, be[blk_idx(b, nu)], f_idx(b, fi, nu), 0))],
            out_specs=pl.BlockSpec((rows, d), lambda b, fi, be, nu: (b, 0)),
            scratch_shapes=[pltpu.VMEM((rows, d), F32), pltpu.VMEM((rows, fblk), BF16)]),
        out_shape=jax.ShapeDtypeStruct((n_rows, d), BF16),
        compiler_params=_cparams("arbitrary", "arbitrary"),
        name="moe_experts",
    )(plan['block_exp'], plan['n_used'], xb, w1, w3, w2)

    final_norm = g_final is not None
    g = (g_final if final_norm else jnp.ones((d,), F32)).reshape(1, d)
    out = pl.pallas_call(
        functools.partial(_combine_kernel, final_norm=final_norm),
        grid_spec=pltpu.PrefetchScalarGridSpec(
            num_scalar_prefetch=3, grid=(nc,),
            in_specs=[pl.BlockSpec(memory_space=pl.ANY),
                      pl.BlockSpec((chunk, TOP_K), lambda c, *_: (c, 0)),
                      pl.BlockSpec((chunk, TOP_K), lambda c, *_: (c, 0)),
                      pl.BlockSpec((chunk, d), lambda c, *_: (c, 0)),
                      pl.BlockSpec((1, d), lambda c, *_: (0, 0))],
            out_specs=pl.BlockSpec((chunk, d), lambda c, *_: (c, 0)),
            scratch_shapes=[pltpu.VMEM((2, LOCAL_ROWS, d), BF16), pltpu.SemaphoreType.DMA((2,))]),
        out_shape=jax.ShapeDtypeStruct((n_tok, d), F32),
        compiler_params=_cparams("arbitrary"),
        name="moe_combine",
    )(*seg, y, plan['lrow_cols'], gates.T, x.reshape(n_tok, d), g)
    return out.reshape(bsz, seq, d)


def _qkv_kernel(x_ref, g_ref, w_ref, *rest, nc):
    o_refs, h_ref = rest[:-1], rest[-1]
    tm = x_ref.shape[1]
    gw = 3 * ATTN_WIDTH
    h = _rms(x_ref[0], g_ref[...])
    nlc = h.shape[1] // LANES
    for c in range(nlc):
        h_ref[c] = h[:, c * LANES:(c + 1) * LANES]
    for gi, (_, dil) in enumerate(ATTN_GROUPS):
        n = tm // dil
        hb = h if dil == 1 else jnp.concatenate(
            [jnp.concatenate([h_ref[c, pl.ds(r, n, stride=dil), :] for c in range(nlc)], axis=1)
             for r in range(dil)], axis=0)
        hb = hb.astype(BF16)
        for c in range(gi * gw, (gi + 1) * gw, nc):
            res = _dot(hb, w_ref[:, c:c + nc]).astype(BF16)
            for r in range(dil):
                o_refs[gi][0, :, r * gw + c - gi * gw:r * gw + c - gi * gw + nc] = res[r * n:(r + 1) * n]


def _attn_kernel(q_ref, kp_ref, kc_ref, kn_ref, vp_ref, vc_ref, vn_ref, o_ref, lse_ref,
                 *, tq, seg, dil, radius, slopes):
    j = pl.program_id(2)
    hd = ATTN_HEAD_DIM
    tb = q_ref.shape[1]
    tk = tq + 2 * radius
    q_all = q_ref[0] * jnp.asarray(1.0 / math.sqrt(hd), BF16)
    k_all = jnp.concatenate([kp_ref[0], kc_ref[0], kn_ref[0]], axis=0)
    v_all = jnp.concatenate([vp_ref[0], vc_ref[0], vn_ref[0]], axis=0)
    qi = lax.broadcasted_iota(jnp.int32, (tq, tk), 0)
    kj = lax.broadcasted_iota(jnp.int32, (tq, tk), 1)
    arel = jnp.abs(kj - radius - qi)
    band = arel <= radius
    dist = (dil * arel).astype(F32)
    alibi = [-(slope * dist) for slope in slopes]
    low = lax.broadcasted_iota(jnp.int32, (1, 2 * hd), 1) < hd
    for t in range(tb // tq):
        rows = slice(t * tq, (t + 1) * tq)
        kpos = j * tb + t * tq - radius + kj
        valid = band & (kpos >= 0) & (kpos < seg)
        q, k, v = q_all[rows], k_all[t * tq:t * tq + tk], v_all[t * tq:t * tq + tk]
        for hp in range(ATTN_HEADS // 2):
            sl = slice(hp * 2 * hd, (hp + 1) * 2 * hd)
            qp, kp, vp = q[:, sl], k[:, sl], v[:, sl]
            o_pair, l_pair = None, None
            for sub in range(2):
                mine = low if sub == 0 else jnp.logical_not(low)
                qm = jnp.where(mine, qp, jnp.zeros_like(qp))
                s = lax.dot_general(qm, kp, (((1,), (1,)), ((), ())), preferred_element_type=F32)
                s = jnp.where(valid, s + alibi[2 * hp + sub], MASK_VALUE)
                m = jnp.max(s, axis=-1, keepdims=True)
                p = jnp.exp(s - m)
                den = jnp.sum(p, axis=-1, keepdims=True)
                o = _dot(p.astype(BF16), vp) / den
                l = jnp.broadcast_to(m + jnp.log(den), o.shape)
                o_pair = o if sub == 0 else jnp.where(low, o_pair, o)
                l_pair = l if sub == 0 else jnp.where(low, l_pair, l)
            o_ref[0, rows, sl] = o_pair.astype(BF16)
            lse_ref[0, rows, sl] = l_pair


def _attn_out_ffn_kernel(x_ref, o0_ref, o1_ref, o2_ref, l0_ref, l1_ref, l2_ref, wo_ref, gf_ref,
                         w1_ref, w3_ref, w2_ref, out_ref, so_ref, sl_ref, hid_ref, *, fc):
    aw = ATTN_WIDTH
    ls, os_ = [], []
    for gi, (o_ref, l_ref) in enumerate(((o0_ref, l0_ref), (o1_ref, l1_ref), (o2_ref, l2_ref))):
        dil = ATTN_GROUPS[gi][1]
        if dil == 1:
            os_.append(o_ref[0].astype(F32))
            ls.append(l_ref[0])
            continue
        n = o_ref.shape[1]
        nlc = aw // LANES
        for r in range(dil):
            ov = o_ref[0, :, r * aw:(r + 1) * aw].astype(F32)
            lv = l_ref[0, :, r * aw:(r + 1) * aw]
            for c in range(nlc):
                so_ref[gi - 1, c, pl.ds(r, n, stride=dil), :] = ov[:, c * LANES:(c + 1) * LANES]
                sl_ref[gi - 1, c, pl.ds(r, n, stride=dil), :] = lv[:, c * LANES:(c + 1) * LANES]
        os_.append(jnp.concatenate([so_ref[gi - 1, c] for c in range(nlc)], axis=1))
        ls.append(jnp.concatenate([sl_ref[gi - 1, c] for c in range(nlc)], axis=1))
    m = jnp.maximum(jnp.maximum(ls[0], ls[1]), ls[2])
    es = [jnp.exp(l - m) for l in ls]
    den = es[0] + es[1] + es[2]
    o = (es[0] * os_[0] + es[1] * os_[1] + es[2] * os_[2]) / den
    x1 = x_ref[0] + _dot(o.astype(BF16), wo_ref[...])
    hb = _rms(x1, gf_ref[...]).astype(BF16)
    out_ref[0] = x1 + _swiglu(hb, w1_ref.at[0], w3_ref.at[0], w2_ref.at[0], hid_ref, fc)


def _alibi_slopes():
    n = len(ATTN_GROUPS) * ATTN_HEADS
    s = np.float32(2.0) ** (np.float32(-8.0) * np.arange(1, n + 1, dtype=np.float32) / np.float32(n))
    return [float(v) for v in s]


def _attention_ffn_layer(x, g_mix, w_qkv, w_out, g_ffn, w1, w3, w2, layer, tm):
    bsz, seq, d = x.shape
    nqkv = w_qkv.shape[1]
    aw = ATTN_WIDTH
    gw = 3 * aw
    assert nqkv == len(ATTN_GROUPS) * gw
    qkvs = pl.pallas_call(
        functools.partial(_qkv_kernel, nc=aw),
        grid=(bsz, seq // tm),
        in_specs=[pl.BlockSpec((1, tm, d), lambda b, i: (b, i, 0)), _full((1, d)), _full((d, nqkv))],
        out_specs=[pl.BlockSpec((1, tm // dil, dil * gw), lambda b, i: (b, i, 0)) for _, dil in ATTN_GROUPS],
        out_shape=[jax.ShapeDtypeStruct((bsz, seq // dil, dil * gw), BF16) for _, dil in ATTN_GROUPS],
        scratch_shapes=[pltpu.VMEM((d // LANES, tm, LANES), F32)],
        compiler_params=_cparams("parallel", "parallel"),
        name="qkv_proj",
    )(x, g_mix.reshape(1, d), w_qkv.astype(BF16))

    slopes = _alibi_slopes()
    outs, lses = [], []
    for gi, (window, dil) in enumerate(ATTN_GROUPS):
        radius = window // (2 * dil)
        seg = seq // dil
        tb = min(512, seg)
        tq = min(128, seg)
        hr = tb // radius
        last = seg // radius - 1
        view = qkvs[gi]

        def col(part, r):
            return r * 3 + part

        def cur(part):
            return pl.BlockSpec((1, tb, aw), lambda b, r, j, part=part: (b, j, col(part, r)))

        def prev(part):
            return pl.BlockSpec((1, radius, aw),
                                lambda b, r, j, part=part: (b, jnp.maximum(j * hr - 1, 0), col(part, r)))

        def nxt(part):
            return pl.BlockSpec((1, radius, aw),
                                lambda b, r, j, part=part: (b, jnp.minimum((j + 1) * hr, last), col(part, r)))

        o, lse = pl.pallas_call(
            functools.partial(_attn_kernel, tq=tq, seg=seg, dil=dil, radius=radius,
                              slopes=slopes[gi * ATTN_HEADS:(gi + 1) * ATTN_HEADS]),
            grid=(bsz, dil, seg // tb),
            in_specs=[cur(0), prev(1), cur(1), nxt(1), prev(2), cur(2), nxt(2)],
            out_specs=[pl.BlockSpec((1, tb, aw), lambda b, r, j: (b, j, r)),
                       pl.BlockSpec((1, tb, aw), lambda b, r, j: (b, j, r))],
            out_shape=[jax.ShapeDtypeStruct((bsz, seg, dil * aw), BF16),
                       jax.ShapeDtypeStruct((bsz, seg, dil * aw), F32)],
            compiler_params=_cparams("parallel", "parallel", "parallel"),
            name=f"dilated_attn_{dil}",
        )(view, view, view, view, view, view, view)
        outs.append(o)
        lses.append(lse)

    f = w1.shape[2]
    row = lambda width: pl.BlockSpec((1, tm, width), lambda b, i: (b, i, 0))
    dilated = [pl.BlockSpec((1, tm // dil, dil * aw), lambda b, i: (b, i, 0)) for _, dil in ATTN_GROUPS]
    return pl.pallas_call(
        functools.partial(_attn_out_ffn_kernel, fc=256),
        grid=(bsz, seq // tm),
        in_specs=[row(d)] + dilated + dilated + [
            _full((aw, d)), _full((1, d)),
            _stacked((d, f), layer), _stacked((d, f), layer), _stacked((f, d), layer)],
        out_specs=row(d),
        out_shape=jax.ShapeDtypeStruct(x.shape, F32),
        scratch_shapes=[pltpu.VMEM((len(ATTN_GROUPS) - 1, aw // LANES, tm, LANES), F32),
                        pltpu.VMEM((len(ATTN_GROUPS) - 1, aw // LANES, tm, LANES), F32),
                        pltpu.VMEM((tm, f), BF16)],
        compiler_params=_cparams("parallel", "parallel"),
        name="attn_out_ffn",
    )(x, *outs, *lses, w_out.astype(BF16), g_ffn.reshape(1, d), w1, w3, w2)


def _conv_glu_kernel(x_ref, g_ref, w_ref, u_ref):
    d = x_ref.shape[2]
    hb = _rms(x_ref[0], g_ref[...]).astype(BF16)
    a = _dot(hb, w_ref[:, :d])
    b = _dot(hb, w_ref[:, d:])
    u_ref[0] = a * jax.nn.sigmoid(b)


def _conv_out_kernel(x_ref, up_ref, uc_ref, un_ref, wdw_ref, bdw_ref, lg_ref, lb_ref, w2_ref, gf_ref, wrt_ref,
                     xo_ref, h_ref, idx_ref, gate_ref, ext_ref, z_ref, *, ts, seq, rows, lanes):
    i = pl.program_id(1)
    n_ext = ts + 2 * HALO
    ue = jnp.concatenate([up_ref[0], uc_ref[0], un_ref[0]], axis=0)
    pos = i * ts - HALO + lax.broadcasted_iota(jnp.int32, (n_ext, 1), 0)
    ext_ref[0] = jnp.where((pos >= 0) & (pos < seq), ue, 0.0)
    for r in range(1, SUBLANES):
        ext_ref[r, 0:n_ext - SUBLANES, :] = ext_ref[0, r:r + n_ext - SUBLANES, :]
    half = CONV_WIDTH // 2

    def row_chunk(c, carry):
        base = pl.multiple_of(c * rows, rows)
        for lo in range(0, z_ref.shape[1], lanes):
            acc = None
            for k in range(CONV_WIDTH):
                start = HALO + k - half
                r = start % SUBLANES
                first = pl.multiple_of(base + (start - r), SUBLANES)
                term = (ext_ref[r, pl.ds(first, rows), lo:lo + lanes].reshape(rows // SUBLANES, SUBLANES, lanes)
                        * wdw_ref[k, :, lo:lo + lanes])
                acc = term if acc is None else acc + term
            z_ref[pl.ds(base, rows), lo:lo + lanes] = acc.reshape(rows, lanes)
        return carry

    lax.fori_loop(0, ts // rows, row_chunk, 0)
    u = z_ref[...] + bdw_ref[...]
    mu = jnp.mean(u, axis=-1, keepdims=True)
    uc = u - mu
    var = jnp.mean(uc * uc, axis=-1, keepdims=True)
    z = uc * lax.rsqrt(var + LN_EPS) * lg_ref[...] + lb_ref[...]
    z = (z * jax.nn.sigmoid(z)).astype(BF16)
    x1 = x_ref[0] + _dot(z, w2_ref[...])
    _moe_prologue(x1, gf_ref, wrt_ref, xo_ref, h_ref, idx_ref, gate_ref)


def _conv_layer(x, g_mix, w_pw1, w_dw, b_dw, ln_g, ln_b, w_pw2, g_ffn, w_router, tm, ts):
    bsz, seq, d = x.shape
    n_tok = bsz * seq
    u = pl.pallas_call(
        _conv_glu_kernel,
        grid=(bsz, seq // tm),
        in_specs=[pl.BlockSpec((1, tm, d), lambda b, i: (b, i, 0)), _full((1, d)), _full((d, 2 * d))],
        out_specs=pl.BlockSpec((1, tm, d), lambda b, i: (b, i, 0)),
        out_shape=jax.ShapeDtypeStruct(x.shape, F32),
        compiler_params=_cparams("parallel", "parallel"),
        name="conv_glu",
    )(x, g_mix.reshape(1, d), w_pw1.astype(BF16))

    nt = seq // ts
    vec = lambda a: a.reshape(1, d)
    return pl.pallas_call(
        functools.partial(_conv_out_kernel, ts=ts, seq=seq, rows=min(32, ts), lanes=min(2 * LANES, d)),
        grid=(bsz, nt),
        in_specs=[pl.BlockSpec((1, ts, d), lambda b, i: (b, i, 0))] + _halo_specs(ts, seq, d) + [
            _full((CONV_WIDTH, SUBLANES, d)), _full((1, d)), _full((1, d)), _full((1, d)), _full((d, d)),
            _full((1, d)), _full((N_EXPERTS, d))],
        out_specs=[pl.BlockSpec((1, ts, d), lambda b, i: (b, i, 0)),
                   pl.BlockSpec((1, ts, d), lambda b, i: (b, i, 0)),
                   pl.BlockSpec((TOP_K, ts), lambda b, i: (0, b * nt + i)),
                   pl.BlockSpec((TOP_K, ts), lambda b, i: (0, b * nt + i))],
        out_shape=[jax.ShapeDtypeStruct(x.shape, F32), jax.ShapeDtypeStruct(x.shape, BF16),
                   jax.ShapeDtypeStruct((TOP_K, n_tok), jnp.int32),
                   jax.ShapeDtypeStruct((TOP_K, n_tok), F32)],
        scratch_shapes=[pltpu.VMEM((SUBLANES, ts + 2 * HALO, d), F32), pltpu.VMEM((ts, d), F32)],
        compiler_params=_cparams("parallel", "parallel"),
        name="conv_out",
    )(x, u, u, u, jnp.broadcast_to(w_dw[:, None, :], (CONV_WIDTH, SUBLANES, d)), vec(b_dw), vec(ln_g), vec(ln_b),
      w_pw2.astype(BF16), vec(g_ffn),
      w_router.T.astype(BF16))


def kernel(x, g_mix, g_ffn, g_final, a_w_grp, a_scale, b_w_out, c_w_qkv, c_w_out, d_w_pw1, d_w_dw, d_b_dw,
           d_ln_g, d_ln_b, d_w_pw2, ffn_w1, ffn_w3, ffn_w2, moe_router, moe_w1, moe_w3, moe_w2):
    assert g_mix.shape[0] == 4, "one layer of each mixer kind"
    tm = min(512, x.shape[1])
    ffn_w = [w.astype(BF16) for w in (ffn_w1, ffn_w3, ffn_w2)]
    moe_w = [w.astype(BF16) for w in (moe_w1, moe_w3, moe_w2)]
    x = _pool_ffn_layer(x, g_mix[0], a_w_grp[0], a_scale[0], g_ffn[0], *ffn_w, 0, tm)
    x, h, idx, gates = _fourier_layer(x, g_mix[1], b_w_out[0], g_ffn[1], moe_router[0], tm)
    x = _moe_layer(x, h, idx, gates, *moe_w, 0)
    x = _attention_ffn_layer(x, g_mix[2], c_w_qkv[0], c_w_out[0], g_ffn[2], *ffn_w, 1, tm)
    x, h, idx, gates = _conv_layer(x, g_mix[3], d_w_pw1[0], d_w_dw[0], d_b_dw[0], d_ln_g[0], d_ln_b[0],
                                   d_w_pw2[0], g_ffn[3], moe_router[1], tm, tm)
    return _moe_layer(x, h, idx, gates, *moe_w, 1, g_final=g_final)
```

```python
import functools
import math

import numpy as np
import jax
import jax.numpy as jnp
from jax import lax
from jax.experimental import pallas as pl
from jax.experimental.pallas import tpu as pltpu

F32 = jnp.float32
BF16 = jnp.bfloat16

RMS_EPS = 1e-6
LN_EPS = 1e-5
MASK_VALUE = -1e30
POOL_WINDOWS = (2, 4, 8, 16)
FOURIER_GROUPS = 4
ATTN_GROUPS = ((128, 1), (512, 4), (2048, 16))
ATTN_HEADS = 8
ATTN_HEAD_DIM = 64
ATTN_WIDTH = ATTN_HEADS * ATTN_HEAD_DIM
CONV_WIDTH = 31
N_EXPERTS = 8
TOP_K = 2

LANES = 128
SUBLANES = 8
HALO = 16
FFT_N2 = 128
MOE_ROWS = 512
MOE_CHUNK = 512
SEG_ALIGN = 16
SEG_PIECES = (512, 256, 128, 64, 32, 16)
LOCAL_ROWS = 1152
VMEM_LIMIT_BYTES = 56 * 1024 * 1024
EXPERT_VMEM_LIMIT_BYTES = 60 * 1024 * 1024


def _cparams(*sem):
    return pltpu.CompilerParams(dimension_semantics=sem, vmem_limit_bytes=VMEM_LIMIT_BYTES)


def _dot(a, b):
    return jnp.dot(a, b, preferred_element_type=F32)


def _rms(x, g):
    ms = jnp.mean(x * x, axis=-1, keepdims=True)
    return x * lax.rsqrt(ms + RMS_EPS) * g


def _swiglu(hb, w1_ref, w3_ref, w2_ref, g_ref, fc):
    f = w1_ref.shape[-1]
    for c in range(0, f, fc):
        a = _dot(hb, w1_ref[:, c:c + fc])
        b = _dot(hb, w3_ref[:, c:c + fc])
        g_ref[:, c:c + fc] = (a * jax.nn.sigmoid(a) * b).astype(BF16)
    return _dot(g_ref[...], w2_ref[...])


def _route(hb, wrt_ref, idx_ref, gate_ref):
    logits = lax.dot_general(wrt_ref[...], hb, (((1,), (1,)), ((), ())), preferred_element_type=F32)
    e_iota = lax.broadcasted_iota(jnp.int32, logits.shape, 0)
    m1 = jnp.max(logits, axis=0, keepdims=True)
    i1 = jnp.min(jnp.where(logits == m1, e_iota, N_EXPERTS), axis=0, keepdims=True)
    rest = jnp.where(e_iota == i1, -jnp.inf, logits)
    m2 = jnp.max(rest, axis=0, keepdims=True)
    i2 = jnp.min(jnp.where(rest == m2, e_iota, N_EXPERTS), axis=0, keepdims=True)
    e2 = jnp.exp(m2 - m1)
    den = 1.0 + e2
    idx_ref[0:1, :] = i1
    idx_ref[1:2, :] = i2
    gate_ref[0:1, :] = 1.0 / den
    gate_ref[1:2, :] = e2 / den


def _moe_prologue(x1, gf_ref, wrt_ref, x_out_ref, h_ref, idx_ref, gate_ref):
    x_out_ref[0] = x1
    hb = _rms(x1, gf_ref[...]).astype(BF16)
    h_ref[0] = hb
    _route(hb, wrt_ref, idx_ref, gate_ref)


def _pool_ffn_kernel(xp_ref, xc_ref, xn_ref, gm_ref, wg_ref, sc_ref, gf_ref, w1_ref, w3_ref, w2_ref,
                     o_ref, hid_ref, *, ts, seq, fc):
    i = pl.program_id(1)
    xc = xc_ref[0]
    d = xc.shape[1]
    gd = d // len(POOL_WINDOWS)
    n_ext = ts + 2 * HALO
    xe = jnp.concatenate([xp_ref[0], xc, xn_ref[0]], axis=0)
    he = _rms(xe, gm_ref[...])
    pos = i * ts - HALO + lax.broadcasted_iota(jnp.int32, (n_ext, 1), 0)
    he = jnp.where((pos >= 0) & (pos < seq), he, 0.0)

    s = he + pltpu.roll(he, 1, 0)
    sums = [s]
    half = 1
    for _ in POOL_WINDOWS[1:]:
        s = s[:, gd:]
        s = pltpu.roll(s, half, 0) + pltpu.roll(s, n_ext - half, 0)
        sums.append(s)
        half *= 2

    tpos = i * ts + lax.broadcasted_iota(jnp.int32, (ts, 1), 0)
    ys = []
    for gi, w in enumerate(POOL_WINDOWS):
        cnt = jnp.minimum(tpos + w // 2, seq) - jnp.maximum(tpos - w // 2, 0)
        mean = sums[gi][HALO:HALO + ts, :gd] / cnt.astype(F32)
        pooled = mean - he[HALO:HALO + ts, gi * gd:(gi + 1) * gd]
        ys.append(_dot(pooled.astype(BF16), wg_ref[gi]))
    x1 = xc + jnp.concatenate(ys, axis=1) * sc_ref[...]
    hb = _rms(x1, gf_ref[...]).astype(BF16)
    o_ref[0] = x1 + _swiglu(hb, w1_ref.at[0], w3_ref.at[0], w2_ref.at[0], hid_ref, fc)


def _halo_specs(ts, seq, d):
    r = ts // HALO
    last = seq // HALO - 1
    return [
        pl.BlockSpec((1, HALO, d), lambda b, i: (b, jnp.maximum(i * r - 1, 0), 0)),
        pl.BlockSpec((1, ts, d), lambda b, i: (b, i, 0)),
        pl.BlockSpec((1, HALO, d), lambda b, i: (b, jnp.minimum((i + 1) * r, last), 0)),
    ]


def _full(shape):
    n = len(shape)
    return pl.BlockSpec(shape, lambda *_: (0,) * n)


def _stacked(shape, layer):
    n = len(shape)
    return pl.BlockSpec((1,) + tuple(shape), lambda *_: (layer,) + (0,) * n, pipeline_mode=pl.Buffered(1))


def _pool_ffn_layer(x, g_mix, w_grp, scale, g_ffn, w1, w3, w2, layer, ts):
    bsz, seq, d = x.shape
    f = w1.shape[2]
    kern = functools.partial(_pool_ffn_kernel, ts=ts, seq=seq, fc=256)
    return pl.pallas_call(
        kern,
        grid=(bsz, seq // ts),
        in_specs=_halo_specs(ts, seq, d) + [
            _full((1, d)), _full(w_grp.shape), _full((1, d)), _full((1, d)),
            _stacked((d, f), layer), _stacked((d, f), layer), _stacked((f, d), layer)],
        out_specs=pl.BlockSpec((1, ts, d), lambda b, i: (b, i, 0)),
        out_shape=jax.ShapeDtypeStruct(x.shape, F32),
        scratch_shapes=[pltpu.VMEM((ts, f), BF16)],
        compiler_params=_cparams("parallel", "parallel"),
        name="pool_ffn",
    )(x, x, x, g_mix.reshape(1, d), w_grp.astype(BF16), scale.reshape(1, d), g_ffn.reshape(1, d), w1, w3, w2)


def _fft_stage1_kernel(x_ref, g_ref, f1_ref, o_ref, s_ref, *, nseg):
    n1, d = x_ref.shape[1], x_ref.shape[3]
    nlc = d // LANES
    for c in range(nlc):
        s_ref[c] = x_ref[0, :, :, c * LANES:(c + 1) * LANES].reshape(n1 * nseg, LANES)
    for j in range(nseg):
        xj = jnp.concatenate([s_ref[c, pl.ds(j, n1, stride=nseg), :] for c in range(nlc)], axis=1)
        h = _rms(xj, g_ref[...]).astype(BF16)
        o_ref[0, j] = _dot(f1_ref[...], h).astype(BF16)


def _fft_stage2_kernel(br_ref, bi_ref, m_ref, o_ref, s_ref, *, nk):
    n2, d = FFT_N2, br_ref.shape[3]
    nlc = d // LANES
    for p, ref in enumerate((br_ref, bi_ref)):
        for c in range(nlc):
            s_ref[p, c] = ref[0, :, :, c * LANES:(c + 1) * LANES].reshape(n2 * nk, LANES).astype(F32)
    for j in range(nk):
        rhs = jnp.concatenate(
            [jnp.concatenate([s_ref[p, c, pl.ds(j, n2, stride=nk), :] for c in range(nlc)], axis=1)
             for p in range(2)], axis=0).astype(BF16)
        res = _dot(m_ref[j], rhs)
        for p in range(2):
            for c in range(nlc):
                s_ref[p, c, pl.ds(j, n2, stride=nk), :] = res[p * n2:(p + 1) * n2, c * LANES:(c + 1) * LANES]
    for p in range(2):
        for c in range(nlc):
            o_ref[0, p, :, :, c * LANES:(c + 1) * LANES] = s_ref[p, c].astype(BF16).reshape(n2, nk, LANES)


def _fourier_out_kernel(x_ref, ar_ref, ai_ref, cc_ref, sc_ref, wo_ref, gf_ref, wrt_ref,
                        xo_ref, h_ref, idx_ref, gate_ref, *, norm):
    d = x_ref.shape[2]
    gd = d // FOURIER_GROUPS
    fs = []
    for gi in range(FOURIER_GROUPS):
        sl = slice(gi * gd, (gi + 1) * gd)
        fs.append(_dot(ar_ref[0, 0, :, sl].astype(BF16), cc_ref[...])
                  + _dot(ai_ref[0, 0, :, sl].astype(BF16), sc_ref[...]))
    f = (jnp.concatenate(fs, axis=1) * norm).astype(BF16)
    x1 = x_ref[0] + _dot(f, wo_ref[...])
    _moe_prologue(x1, gf_ref, wrt_ref, xo_ref, h_ref, idx_ref, gate_ref)


def _dft_tables(seq, gd):
    n2 = FFT_N2
    n1 = seq // n2
    k = np.arange(n1)
    ang1 = 2.0 * np.pi * ((k[:, None] * k[None, :]) % n1) / n1
    f1 = np.concatenate([np.cos(ang1), -np.sin(ang1)], axis=0)
    c = np.arange(gd)
    angc = 2.0 * np.pi * ((c[:, None] * c[None, :]) % gd) / gd
    return (jnp.asarray(f1, BF16), jnp.asarray(np.cos(angc), BF16), jnp.asarray(np.sin(angc), BF16))


def _twiddled_dft(seq):
    n2 = FFT_N2
    n1 = seq // n2
    k1 = lax.broadcasted_iota(jnp.int32, (n1, n2, n2), 0)
    k2 = lax.broadcasted_iota(jnp.int32, (n1, n2, n2), 1)
    a = lax.broadcasted_iota(jnp.int32, (n1, n2, n2), 2)
    ang = ((a * (n1 * k2 + k1)) % seq).astype(F32) * (2.0 * math.pi / seq)
    mr = jnp.cos(ang)
    mi = -jnp.sin(ang)
    top = jnp.concatenate([mr, -mi], axis=2)
    bot = jnp.concatenate([mi, mr], axis=2)
    return jnp.concatenate([top, bot], axis=1).astype(BF16)


def _fourier_layer(x, g_mix, w_out, g_ffn, w_router, tm):
    bsz, seq, d = x.shape
    n2 = FFT_N2
    n1 = seq // n2
    gd = d // FOURIER_GROUPS
    f1, cc, sc = _dft_tables(seq, gd)
    mtab = _twiddled_dft(seq)
    nseg = 8
    b1 = pl.pallas_call(
        functools.partial(_fft_stage1_kernel, nseg=nseg),
        grid=(bsz, n2 // nseg),
        in_specs=[pl.BlockSpec((1, n1, nseg, d), lambda b, t: (b, 0, t, 0)), _full((1, d)), _full(f1.shape)],
        out_specs=pl.BlockSpec((1, nseg, 2 * n1, d), lambda b, t: (b, t, 0, 0)),
        out_shape=jax.ShapeDtypeStruct((bsz, n2, 2 * n1, d), BF16),
        scratch_shapes=[pltpu.VMEM((d // LANES, n1 * nseg, LANES), F32)],
        compiler_params=_cparams("parallel", "parallel"),
        name="fft_stage1",
    )(x.reshape(bsz, n1, n2, d), g_mix.reshape(1, d), f1)

    nk = 16
    a = pl.pallas_call(
        functools.partial(_fft_stage2_kernel, nk=nk),
        grid=(bsz, n1 // nk),
        in_specs=[pl.BlockSpec((1, n2, nk, d), lambda b, t: (b, 0, t, 0)),
                  pl.BlockSpec((1, n2, nk, d), lambda b, t: (b, 0, n1 // nk + t, 0)),
                  pl.BlockSpec((nk, 2 * n2, 2 * n2), lambda b, t: (t, 0, 0))],
        out_specs=pl.BlockSpec((1, 2, n2, nk, d), lambda b, t: (b, 0, 0, t, 0)),
        out_shape=jax.ShapeDtypeStruct((bsz, 2, n2, n1, d), BF16),
        scratch_shapes=[pltpu.VMEM((2, d // LANES, n2 * nk, LANES), F32)],
        compiler_params=_cparams("parallel", "parallel"),
        name="fft_stage2",
    )(b1, b1, mtab)

    av = a.reshape(bsz, 2, seq, d)
    n_tok = bsz * seq
    nt = seq // tm
    outs = pl.pallas_call(
        functools.partial(_fourier_out_kernel, norm=1.0 / math.sqrt(seq * gd)),
        grid=(bsz, nt),
        in_specs=[pl.BlockSpec((1, tm, d), lambda b, i: (b, i, 0)),
                  pl.BlockSpec((1, 1, tm, d), lambda b, i: (b, 0, i, 0)),
                  pl.BlockSpec((1, 1, tm, d), lambda b, i: (b, 1, i, 0)),
                  _full((gd, gd)), _full((gd, gd)), _full((d, d)), _full((1, d)), _full((N_EXPERTS, d))],
        out_specs=[pl.BlockSpec((1, tm, d), lambda b, i: (b, i, 0)),
                   pl.BlockSpec((1, tm, d), lambda b, i: (b, i, 0)),
                   pl.BlockSpec((TOP_K, tm), lambda b, i: (0, b * nt + i)),
                   pl.BlockSpec((TOP_K, tm), lambda b, i: (0, b * nt + i))],
        out_shape=[jax.ShapeDtypeStruct(x.shape, F32), jax.ShapeDtypeStruct(x.shape, BF16),
                   jax.ShapeDtypeStruct((TOP_K, n_tok), jnp.int32),
                   jax.ShapeDtypeStruct((TOP_K, n_tok), F32)],
        compiler_params=_cparams("parallel", "parallel"),
        name="fourier_out",
    )(x, av, av, cc, sc, w_out.astype(BF16), g_ffn.reshape(1, d), w_router.T.astype(BF16))
    return outs


def _moe_plan(idx, n_tok):
    chunk, rows = MOE_CHUNK, MOE_ROWS
    nc = n_tok // chunk
    t_iota = jnp.arange(chunk, dtype=jnp.int32)
    before = (t_iota[:, None] < t_iota[None, :]).astype(BF16)
    lrow_lanes, seg_len = pl.pallas_call(
        _moe_rank_kernel,
        grid=(nc,),
        in_specs=[pl.BlockSpec((TOP_K, chunk), lambda c: (0, c)), _full((chunk, chunk))],
        out_specs=[pl.BlockSpec((1, TOP_K, chunk), lambda c: (c, 0, 0)),
                   pl.BlockSpec((1, N_EXPERTS, LANES), lambda c: (c, 0, 0))],
        out_shape=[jax.ShapeDtypeStruct((nc, TOP_K, chunk), jnp.int32),
                   jax.ShapeDtypeStruct((nc, N_EXPERTS, LANES), jnp.int32)],
        compiler_params=_cparams("parallel"),
        name="moe_rank",
    )(idx, before)
    seg_len = seg_len[:, :, 0]
    loff = jnp.cumsum(seg_len, axis=1) - seg_len
    region = (jnp.sum(seg_len, axis=0) + rows - 1) // rows * rows
    pend = jnp.cumsum(region)
    goff = (pend - region)[None, :] + jnp.cumsum(seg_len, axis=0) - seg_len
    n_blocks = (n_tok * TOP_K + nc * N_EXPERTS * (SEG_ALIGN - 1)) // rows + N_EXPERTS
    blk = jnp.arange(n_blocks, dtype=jnp.int32)
    block_exp = jnp.minimum(jnp.sum((blk[:, None] * rows >= pend[None, :]).astype(jnp.int32), axis=1),
                            N_EXPERTS - 1)
    return dict(
        n_blocks=n_blocks,
        seg_len=seg_len.reshape(-1), loff=loff.reshape(-1).astype(jnp.int32), goff=goff.reshape(-1).astype(jnp.int32),
        gap_off=(pend - region + jnp.sum(seg_len, axis=0)).astype(jnp.int32),
        gap_len=(region - jnp.sum(seg_len, axis=0)).astype(jnp.int32),
        lrow_lanes=lrow_lanes,
        lrow_cols=lrow_lanes.transpose(0, 2, 1).reshape(n_tok, TOP_K),
        n_used=(pend[-1] // rows).reshape(1).astype(jnp.int32), block_exp=block_exp)


def _moe_rank_kernel(idx_ref, before_ref, lrow_ref, len_ref):
    e_iota = lax.broadcasted_iota(jnp.int32, (N_EXPERTS, idx_ref.shape[1]), 0)
    picks = [jnp.where(idx_ref[k:k + 1, :] == e_iota, 1.0, 0.0) for k in range(TOP_K)]
    total = picks[0]
    for m in picks[1:]:
        total = total + m
    earlier = _dot(total.astype(BF16), before_ref[...])
    cnt = jnp.sum(total, axis=1, keepdims=True).astype(jnp.int32)
    seg = (cnt + (SEG_ALIGN - 1)) & (-SEG_ALIGN)
    e_col = lax.broadcasted_iota(jnp.int32, (N_EXPERTS, 1), 0)
    loff = jnp.zeros((N_EXPERTS, 1), jnp.int32)
    for e in range(N_EXPERTS - 1):
        loff = loff + jnp.where(e_col > e, seg[e:e + 1, :], 0)
    base = loff.astype(F32) + earlier
    for k in range(TOP_K):
        lrow_ref[0, k:k + 1, :] = jnp.sum(picks[k] * base, axis=0, keepdims=True).astype(jnp.int32)
        base = base + picks[k]
    len_ref[0] = jnp.broadcast_to(seg, (N_EXPERTS, LANES))


def _segment_copies(c, len_ref, loff_ref, goff_ref, make, act):
    for e in range(N_EXPERTS):
        ln = len_ref[c * N_EXPERTS + e]
        lo = loff_ref[c * N_EXPERTS + e]
        go = goff_ref[c * N_EXPERTS + e]
        for size in SEG_PIECES:
            done = ln & (-2 * size)

            @pl.when((ln & size) != 0)
            def _():
                act(make(pl.multiple_of(lo + done, SEG_ALIGN), pl.multiple_of(go + done, SEG_ALIGN), size))


def _zero_fill_copies(gap_off_ref, gap_len_ref, nu_ref, zero_ref, xb_ref, sem, act):
    rows = zero_ref.shape[0]
    for e in range(N_EXPERTS):
        ln = gap_len_ref[e]
        off = gap_off_ref[e]
        for size in SEG_PIECES[1:]:
            done = ln & (-2 * size)

            @pl.when((ln & size) != 0)
            def _():
                act(pltpu.make_async_copy(zero_ref.at[pl.ds(0, size)],
                                          xb_ref.at[pl.ds(pl.multiple_of(off + done, SEG_ALIGN), size)], sem))

    def block(b, carry):
        act(pltpu.make_async_copy(zero_ref, xb_ref.at[pl.ds(pl.multiple_of(b * rows, rows), rows)], sem))
        return carry

    lax.fori_loop(nu_ref[0], xb_ref.shape[0] // rows, block, 0)


def _dispatch_kernel(len_ref, loff_ref, goff_ref, gap_off_ref, gap_len_ref, nu_ref, h_ref, lrow_ref, xb_ref,
                     stage_ref, zero_ref, sem, zsem):
    c = pl.program_id(0)
    slot = c % 2
    n_local, chunk = stage_ref.shape[1], h_ref.shape[0]

    @pl.when(c == 0)
    def _():
        zero_ref[...] = jnp.zeros_like(zero_ref)
        _zero_fill_copies(gap_off_ref, gap_len_ref, nu_ref, zero_ref, xb_ref, zsem, lambda cp: cp.start())

    def make(slot_):
        def _make(lr, gr, size):
            return pltpu.make_async_copy(stage_ref.at[slot_, pl.ds(lr, size)], xb_ref.at[pl.ds(gr, size)],
                                         sem.at[slot_])
        return _make

    r_iota = lax.broadcasted_iota(jnp.int32, (n_local, chunk), 0)
    sel = (lrow_ref[0, 0:1, :] == r_iota) | (lrow_ref[0, 1:2, :] == r_iota)
    stage_ref[slot] = _dot(jnp.where(sel, 1.0, 0.0).astype(BF16), h_ref[...]).astype(BF16)

    @pl.when(c > 0)
    def _():
        _segment_copies(c - 1, len_ref, loff_ref, goff_ref, make(1 - slot), lambda cp: cp.wait())

    _segment_copies(c, len_ref, loff_ref, goff_ref, make(slot), lambda cp: cp.start())

    @pl.when(c == pl.num_programs(0) - 1)
    def _():
        _segment_copies(c, len_ref, loff_ref, goff_ref, make(slot), lambda cp: cp.wait())
        _zero_fill_copies(gap_off_ref, gap_len_ref, nu_ref, zero_ref, xb_ref, zsem, lambda cp: cp.wait())


def _expert_kernel(be_ref, nu_ref, xb_ref, w1_ref, w3_ref, w2_ref, y_ref, hid_ref, *, fc):
    b = pl.program_id(0)

    @pl.when(b < nu_ref[0])
    def _():
        y_ref[...] = _swiglu(xb_ref[...], w1_ref.at[0, 0], w3_ref.at[0, 0], w2_ref.at[0, 0], hid_ref,
                             fc).astype(BF16)

    @pl.when(b >= nu_ref[0])
    def _():
        y_ref[...] = jnp.zeros_like(y_ref)


def _combine_kernel(len_ref, loff_ref, goff_ref, y_ref, lcol_ref, gcol_ref, x_ref, g_ref, o_ref, ybuf_ref, sem,
                    *, final_norm):
    c = pl.program_id(0)
    nc = pl.num_programs(0)
    slot = c % 2
    chunk, n_local = x_ref.shape[0], ybuf_ref.shape[1]

    def make(slot_):
        def _make(lr, gr, size):
            return pltpu.make_async_copy(y_ref.at[pl.ds(gr, size)], ybuf_ref.at[slot_, pl.ds(lr, size)],
                                         sem.at[slot_])
        return _make

    @pl.when(c == 0)
    def _():
        ybuf_ref[...] = jnp.zeros_like(ybuf_ref)
        _segment_copies(c, len_ref, loff_ref, goff_ref, make(slot), lambda cp: cp.start())

    @pl.when(c + 1 < nc)
    def _():
        _segment_copies(c + 1, len_ref, loff_ref, goff_ref, make(1 - slot), lambda cp: cp.start())

    _segment_copies(c, len_ref, loff_ref, goff_ref, make(slot), lambda cp: cp.wait())

    yl = ybuf_ref[slot]
    r_iota = lax.broadcasted_iota(jnp.int32, (chunk, n_local), 1)
    q = None
    for k in range(TOP_K):
        qk = jnp.where(lcol_ref[:, k:k + 1] == r_iota, gcol_ref[:, k:k + 1], 0.0)
        q = qk if q is None else q + qk
    xo = x_ref[...] + _dot(q.astype(BF16), yl)
    o_ref[...] = _rms(xo, g_ref[...]) if final_norm else xo


def _moe_layer(x, h, idx, gates, w1, w3, w2, layer, g_final=None):
    bsz, seq, d = x.shape
    n_tok = bsz * seq
    f = w1.shape[3]
    rows, chunk = MOE_ROWS, MOE_CHUNK
    nc = n_tok // chunk
    plan = _moe_plan(idx, n_tok)
    n_blocks = plan['n_blocks']
    n_rows = n_blocks * rows
    seg = (plan['seg_len'], plan['loff'], plan['goff'])

    xb = pl.pallas_call(
        _dispatch_kernel,
        grid_spec=pltpu.PrefetchScalarGridSpec(
            num_scalar_prefetch=6, grid=(nc,),
            in_specs=[pl.BlockSpec((chunk, d), lambda c, *_: (c, 0)),
                      pl.BlockSpec((1, TOP_K, chunk), lambda c, *_: (c, 0, 0))],
            out_specs=pl.BlockSpec(memory_space=pl.ANY),
            scratch_shapes=[pltpu.VMEM((2, LOCAL_ROWS, d), BF16), pltpu.VMEM((rows, d), BF16),
                            pltpu.SemaphoreType.DMA((2,)), pltpu.SemaphoreType.DMA(())]),
        out_shape=jax.ShapeDtypeStruct((n_rows, d), BF16),
        compiler_params=_cparams("arbitrary"),
        name="moe_dispatch",
    )(*seg, plan['gap_off'], plan['gap_len'], plan['n_used'], h.reshape(n_tok, d), plan['lrow_lanes'])

    def blk_idx(b, nu):
        return jnp.minimum(b, nu[0] - 1)

    y = pl.pallas_call(
        functools.partial(_expert_kernel, fc=256),
        grid_spec=pltpu.PrefetchScalarGridSpec(
            num_scalar_prefetch=2, grid=(n_blocks,),
            in_specs=[pl.BlockSpec((rows, d), lambda b, be, nu: (blk_idx(b, nu), 0)),
                      pl.BlockSpec((1, 1, d, f), lambda b, be, nu: (layer, be[blk_idx(b, nu)], 0, 0)),
                      pl.BlockSpec((1, 1, d, f), lambda b, be, nu: (layer, be[blk_idx(b, nu)], 0, 0)),
                      pl.BlockSpec((1, 1, f, d), lambda b, be, nu: (layer, be[blk_idx(b, nu)], 0, 0))],
            out_specs=pl.BlockSpec((rows, d), lambda b, be, nu: (b, 0)),
            scratch_shapes=[pltpu.VMEM((rows, f), BF16)]),
        out_shape=jax.ShapeDtypeStruct((n_rows, d), BF16),
        compiler_params=pltpu.CompilerParams(dimension_semantics=("arbitrary",),
                                             vmem_limit_bytes=EXPERT_VMEM_LIMIT_BYTES),
        name="moe_experts",
    )(plan['block_exp'], plan['n_used'], xb, w1, w3, w2)

    final_norm = g_final is not None
    g = (g_final if final_norm else jnp.ones((d,), F32)).reshape(1, d)
    out = pl.pallas_call(
        functools.partial(_combine_kernel, final_norm=final_norm),
        grid_spec=pltpu.PrefetchScalarGridSpec(
            num_scalar_prefetch=3, grid=(nc,),
            in_specs=[pl.BlockSpec(memory_space=pl.ANY),
                      pl.BlockSpec((chunk, TOP_K), lambda c, *_: (c, 0)),
                      pl.BlockSpec((chunk, TOP_K), lambda c, *_: (c, 0)),
                      pl.BlockSpec((chunk, d), lambda c, *_: (c, 0)),
                      pl.BlockSpec((1, d), lambda c, *_: (0, 0))],
            out_specs=pl.BlockSpec((chunk, d), lambda c, *_: (c, 0)),
            scratch_shapes=[pltpu.VMEM((2, LOCAL_ROWS, d), BF16), pltpu.SemaphoreType.DMA((2,))]),
        out_shape=jax.ShapeDtypeStruct((n_tok, d), F32),
        compiler_params=_cparams("arbitrary"),
        name="moe_combine",
    )(*seg, y, plan['lrow_cols'], gates.T, x.reshape(n_tok, d), g)
    return out.reshape(bsz, seq, d)


def _qkv_kernel(x_ref, g_ref, w_ref, *rest, nc):
    o_refs, h_ref = rest[:-1], rest[-1]
    tm = x_ref.shape[1]
    gw = 3 * ATTN_WIDTH
    h = _rms(x_ref[0], g_ref[...])
    nlc = h.shape[1] // LANES
    for c in range(nlc):
        h_ref[c] = h[:, c * LANES:(c + 1) * LANES]
    for gi, (_, dil) in enumerate(ATTN_GROUPS):
        n = tm // dil
        hb = h if dil == 1 else jnp.concatenate(
            [jnp.concatenate([h_ref[c, pl.ds(r, n, stride=dil), :] for c in range(nlc)], axis=1)
             for r in range(dil)], axis=0)
        hb = hb.astype(BF16)
        for c in range(gi * gw, (gi + 1) * gw, nc):
            res = _dot(hb, w_ref[:, c:c + nc]).astype(BF16)
            for r in range(dil):
                o_refs[gi][0, :, r * gw + c - gi * gw:r * gw + c - gi * gw + nc] = res[r * n:(r + 1) * n]


def _attn_kernel(q_ref, kp_ref, kc_ref, kn_ref, vp_ref, vc_ref, vn_ref, o_ref, lse_ref,
                 *, tq, seg, dil, radius, slopes):
    j = pl.program_id(2)
    hd = ATTN_HEAD_DIM
    tb = q_ref.shape[1]
    tk = tq + 2 * radius
    q_all = q_ref[0] * jnp.asarray(1.0 / math.sqrt(hd), BF16)
    k_all = jnp.concatenate([kp_ref[0], kc_ref[0], kn_ref[0]], axis=0)
    v_all = jnp.concatenate([vp_ref[0], vc_ref[0], vn_ref[0]], axis=0)
    qi = lax.broadcasted_iota(jnp.int32, (tq, tk), 0)
    kj = lax.broadcasted_iota(jnp.int32, (tq, tk), 1)
    arel = jnp.abs(kj - radius - qi)
    band = arel <= radius
    dist = (dil * arel).astype(F32)
    alibi = [-(slope * dist) for slope in slopes]
    low = lax.broadcasted_iota(jnp.int32, (1, 2 * hd), 1) < hd
    for t in range(tb // tq):
        rows = slice(t * tq, (t + 1) * tq)
        kpos = j * tb + t * tq - radius + kj
        valid = band & (kpos >= 0) & (kpos < seg)
        q, k, v = q_all[rows], k_all[t * tq:t * tq + tk], v_all[t * tq:t * tq + tk]
        for hp in range(ATTN_HEADS // 2):
            sl = slice(hp * 2 * hd, (hp + 1) * 2 * hd)
            qp, kp, vp = q[:, sl], k[:, sl], v[:, sl]
            o_pair, l_pair = None, None
            for sub in range(2):
                mine = low if sub == 0 else jnp.logical_not(low)
                qm = jnp.where(mine, qp, jnp.zeros_like(qp))
                s = lax.dot_general(qm, kp, (((1,), (1,)), ((), ())), preferred_element_type=F32)
                s = jnp.where(valid, s + alibi[2 * hp + sub], MASK_VALUE)
                m = jnp.max(s, axis=-1, keepdims=True)
                p = jnp.exp(s - m)
                den = jnp.sum(p, axis=-1, keepdims=True)
                o = _dot(p.astype(BF16), vp) / den
                l = jnp.broadcast_to(m + jnp.log(den), o.shape)
                o_pair = o if sub == 0 else jnp.where(low, o_pair, o)
                l_pair = l if sub == 0 else jnp.where(low, l_pair, l)
            o_ref[0, rows, sl] = o_pair.astype(BF16)
            lse_ref[0, rows, sl] = l_pair


def _attn_out_ffn_kernel(x_ref, o0_ref, o1_ref, o2_ref, l0_ref, l1_ref, l2_ref, wo_ref, gf_ref,
                         w1_ref, w3_ref, w2_ref, out_ref, so_ref, sl_ref, hid_ref, *, fc):
    aw = ATTN_WIDTH
    ls, os_ = [], []
    for gi, (o_ref, l_ref) in enumerate(((o0_ref, l0_ref), (o1_ref, l1_ref), (o2_ref, l2_ref))):
        dil = ATTN_GROUPS[gi][1]
        if dil == 1:
            os_.append(o_ref[0].astype(F32))
            ls.append(l_ref[0])
            continue
        n = o_ref.shape[1]
        nlc = aw // LANES
        for r in range(dil):
            ov = o_ref[0, :, r * aw:(r + 1) * aw].astype(F32)
            lv = l_ref[0, :, r * aw:(r + 1) * aw]
            for c in range(nlc):
                so_ref[gi - 1, c, pl.ds(r, n, stride=dil), :] = ov[:, c * LANES:(c + 1) * LANES]
                sl_ref[gi - 1, c, pl.ds(r, n, stride=dil), :] = lv[:, c * LANES:(c + 1) * LANES]
        os_.append(jnp.concatenate([so_ref[gi - 1, c] for c in range(nlc)], axis=1))
        ls.append(jnp.concatenate([sl_ref[gi - 1, c] for c in range(nlc)], axis=1))
    m = jnp.maximum(jnp.maximum(ls[0], ls[1]), ls[2])
    es = [jnp.exp(l - m) for l in ls]
    den = es[0] + es[1] + es[2]
    o = (es[0] * os_[0] + es[1] * os_[1] + es[2] * os_[2]) / den
    x1 = x_ref[0] + _dot(o.astype(BF16), wo_ref[...])
    hb = _rms(x1, gf_ref[...]).astype(BF16)
    out_ref[0] = x1 + _swiglu(hb, w1_ref.at[0], w3_ref.at[0], w2_ref.at[0], hid_ref, fc)


def _alibi_slopes():
    n = len(ATTN_GROUPS) * ATTN_HEADS
    s = np.float32(2.0) ** (np.float32(-8.0) * np.arange(1, n + 1, dtype=np.float32) / np.float32(n))
    return [float(v) for v in s]


def _attention_ffn_layer(x, g_mix, w_qkv, w_out, g_ffn, w1, w3, w2, layer, tm):
    bsz, seq, d = x.shape
    nqkv = w_qkv.shape[1]
    aw = ATTN_WIDTH
    gw = 3 * aw
    assert nqkv == len(ATTN_GROUPS) * gw
    qkvs = pl.pallas_call(
        functools.partial(_qkv_kernel, nc=aw),
        grid=(bsz, seq // tm),
        in_specs=[pl.BlockSpec((1, tm, d), lambda b, i: (b, i, 0)), _full((1, d)), _full((d, nqkv))],
        out_specs=[pl.BlockSpec((1, tm // dil, dil * gw), lambda b, i: (b, i, 0)) for _, dil in ATTN_GROUPS],
        out_shape=[jax.ShapeDtypeStruct((bsz, seq // dil, dil * gw), BF16) for _, dil in ATTN_GROUPS],
        scratch_shapes=[pltpu.VMEM((d // LANES, tm, LANES), F32)],
        compiler_params=_cparams("parallel", "parallel"),
        name="qkv_proj",
    )(x, g_mix.reshape(1, d), w_qkv.astype(BF16))

    slopes = _alibi_slopes()
    outs, lses = [], []
    for gi, (window, dil) in enumerate(ATTN_GROUPS):
        radius = window // (2 * dil)
        seg = seq // dil
        tb = min(512, seg)
        tq = min(128, seg)
        hr = tb // radius
        last = seg // radius - 1
        view = qkvs[gi]

        def col(part, r):
            return r * 3 + part

        def cur(part):
            return pl.BlockSpec((1, tb, aw), lambda b, r, j, part=part: (b, j, col(part, r)))

        def prev(part):
            return pl.BlockSpec((1, radius, aw),
                                lambda b, r, j, part=part: (b, jnp.maximum(j * hr - 1, 0), col(part, r)))

        def nxt(part):
            return pl.BlockSpec((1, radius, aw),
                                lambda b, r, j, part=part: (b, jnp.minimum((j + 1) * hr, last), col(part, r)))

        o, lse = pl.pallas_call(
            functools.partial(_attn_kernel, tq=tq, seg=seg, dil=dil, radius=radius,
                              slopes=slopes[gi * ATTN_HEADS:(gi + 1) * ATTN_HEADS]),
            grid=(bsz, dil, seg // tb),
            in_specs=[cur(0), prev(1), cur(1), nxt(1), prev(2), cur(2), nxt(2)],
            out_specs=[pl.BlockSpec((1, tb, aw), lambda b, r, j: (b, j, r)),
                       pl.BlockSpec((1, tb, aw), lambda b, r, j: (b, j, r))],
            out_shape=[jax.ShapeDtypeStruct((bsz, seg, dil * aw), BF16),
                       jax.ShapeDtypeStruct((bsz, seg, dil * aw), F32)],
            compiler_params=_cparams("parallel", "parallel", "parallel"),
            name=f"dilated_attn_{dil}",
        )(view, view, view, view, view, view, view)
        outs.append(o)
        lses.append(lse)

    f = w1.shape[2]
    row = lambda width: pl.BlockSpec((1, tm, width), lambda b, i: (b, i, 0))
    dilated = [pl.BlockSpec((1, tm // dil, dil * aw), lambda b, i: (b, i, 0)) for _, dil in ATTN_GROUPS]
    return pl.pallas_call(
        functools.partial(_attn_out_ffn_kernel, fc=256),
        grid=(bsz, seq // tm),
        in_specs=[row(d)] + dilated + dilated + [
            _full((aw, d)), _full((1, d)),
            _stacked((d, f), layer), _stacked((d, f), layer), _stacked((f, d), layer)],
        out_specs=row(d),
        out_shape=jax.ShapeDtypeStruct(x.shape, F32),
        scratch_shapes=[pltpu.VMEM((len(ATTN_GROUPS) - 1, aw // LANES, tm, LANES), F32),
                        pltpu.VMEM((len(ATTN_GROUPS) - 1, aw // LANES, tm, LANES), F32),
                        pltpu.VMEM((tm, f), BF16)],
        compiler_params=_cparams("parallel", "parallel"),
        name="attn_out_ffn",
    )(x, *outs, *lses, w_out.astype(BF16), g_ffn.reshape(1, d), w1, w3, w2)


def _conv_glu_kernel(x_ref, g_ref, w_ref, u_ref):
    d = x_ref.shape[2]
    hb = _rms(x_ref[0], g_ref[...]).astype(BF16)
    a = _dot(hb, w_ref[:, :d])
    b = _dot(hb, w_ref[:, d:])
    u_ref[0] = a * jax.nn.sigmoid(b)


def _conv_out_kernel(x_ref, up_ref, uc_ref, un_ref, wdw_ref, bdw_ref, lg_ref, lb_ref, w2_ref, gf_ref, wrt_ref,
                     xo_ref, h_ref, idx_ref, gate_ref, ext_ref, z_ref, *, ts, seq, rows, lanes):
    i = pl.program_id(1)
    n_ext = ts + 2 * HALO
    ue = jnp.concatenate([up_ref[0], uc_ref[0], un_ref[0]], axis=0)
    pos = i * ts - HALO + lax.broadcasted_iota(jnp.int32, (n_ext, 1), 0)
    ext_ref[0] = jnp.where((pos >= 0) & (pos < seq), ue, 0.0)
    for r in range(1, SUBLANES):
        ext_ref[r, 0:n_ext - SUBLANES, :] = ext_ref[0, r:r + n_ext - SUBLANES, :]
    half = CONV_WIDTH // 2

    def row_chunk(c, carry):
        base = pl.multiple_of(c * rows, rows)
        for lo in range(0, z_ref.shape[1], lanes):
            acc = None
            for k in range(CONV_WIDTH):
                start = HALO + k - half
                r = start % SUBLANES
                first = pl.multiple_of(base + (start - r), SUBLANES)
                term = (ext_ref[r, pl.ds(first, rows), lo:lo + lanes].reshape(rows // SUBLANES, SUBLANES, lanes)
                        * wdw_ref[k, :, lo:lo + lanes])
                acc = term if acc is None else acc + term
            z_ref[pl.ds(base, rows), lo:lo + lanes] = acc.reshape(rows, lanes)
        return carry

    lax.fori_loop(0, ts // rows, row_chunk, 0)
    u = z_ref[...] + bdw_ref[...]
    mu = jnp.mean(u, axis=-1, keepdims=True)
    uc = u - mu
    var = jnp.mean(uc * uc, axis=-1, keepdims=True)
    z = uc * lax.rsqrt(var + LN_EPS) * lg_ref[...] + lb_ref[...]
    z = (z * jax.nn.sigmoid(z)).astype(BF16)
    x1 = x_ref[0] + _dot(z, w2_ref[...])
    _moe_prologue(x1, gf_ref, wrt_ref, xo_ref, h_ref, idx_ref, gate_ref)


def _conv_layer(x, g_mix, w_pw1, w_dw, b_dw, ln_g, ln_b, w_pw2, g_ffn, w_router, tm, ts):
    bsz, seq, d = x.shape
    n_tok = bsz * seq
    u = pl.pallas_call(
        _conv_glu_kernel,
        grid=(bsz, seq // tm),
        in_specs=[pl.BlockSpec((1, tm, d), lambda b, i: (b, i, 0)), _full((1, d)), _full((d, 2 * d))],
        out_specs=pl.BlockSpec((1, tm, d), lambda b, i: (b, i, 0)),
        out_shape=jax.ShapeDtypeStruct(x.shape, F32),
        compiler_params=_cparams("parallel", "parallel"),
        name="conv_glu",
    )(x, g_mix.reshape(1, d), w_pw1.astype(BF16))

    nt = seq // ts
    vec = lambda a: a.reshape(1, d)
    return pl.pallas_call(
        functools.partial(_conv_out_kernel, ts=ts, seq=seq, rows=min(32, ts), lanes=min(2 * LANES, d)),
        grid=(bsz, nt),
        in_specs=[pl.BlockSpec((1, ts, d), lambda b, i: (b, i, 0))] + _halo_specs(ts, seq, d) + [
            _full((CONV_WIDTH, SUBLANES, d)), _full((1, d)), _full((1, d)), _full((1, d)), _full((d, d)),
            _full((1, d)), _full((N_EXPERTS, d))],
        out_specs=[pl.BlockSpec((1, ts, d), lambda b, i: (b, i, 0)),
                   pl.BlockSpec((1, ts, d), lambda b, i: (b, i, 0)),
                   pl.BlockSpec((TOP_K, ts), lambda b, i: (0, b * nt + i)),
                   pl.BlockSpec((TOP_K, ts), lambda b, i: (0, b * nt + i))],
        out_shape=[jax.ShapeDtypeStruct(x.shape, F32), jax.ShapeDtypeStruct(x.shape, BF16),
                   jax.ShapeDtypeStruct((TOP_K, n_tok), jnp.int32),
                   jax.ShapeDtypeStruct((TOP_K, n_tok), F32)],
        scratch_shapes=[pltpu.VMEM((SUBLANES, ts + 2 * HALO, d), F32), pltpu.VMEM((ts, d), F32)],
        compiler_params=_cparams("parallel", "parallel"),
        name="conv_out",
    )(x, u, u, u, jnp.broadcast_to(w_dw[:, None, :], (CONV_WIDTH, SUBLANES, d)), vec(b_dw), vec(ln_g), vec(ln_b),
      w_pw2.astype(BF16), vec(g_ffn),
      w_router.T.astype(BF16))


def kernel(x, g_mix, g_ffn, g_final, a_w_grp, a_scale, b_w_out, c_w_qkv, c_w_out, d_w_pw1, d_w_dw, d_b_dw,
           d_ln_g, d_ln_b, d_w_pw2, ffn_w1, ffn_w3, ffn_w2, moe_router, moe_w1, moe_w3, moe_w2):
    assert g_mix.shape[0] == 4, "one layer of each mixer kind"
    tm = min(512, x.shape[1])
    ffn_w = [w.astype(BF16) for w in (ffn_w1, ffn_w3, ffn_w2)]
    moe_w = [w.astype(BF16) for w in (moe_w1, moe_w3, moe_w2)]
    x = _pool_ffn_layer(x, g_mix[0], a_w_grp[0], a_scale[0], g_ffn[0], *ffn_w, 0, tm)
    x, h, idx, gates = _fourier_layer(x, g_mix[1], b_w_out[0], g_ffn[1], moe_router[0], tm)
    x = _moe_layer(x, h, idx, gates, *moe_w, 0)
    x = _attention_ffn_layer(x, g_mix[2], c_w_qkv[0], c_w_out[0], g_ffn[2], *ffn_w, 1, tm)
    x, h, idx, gates = _conv_layer(x, g_mix[3], d_w_pw1[0], d_w_dw[0], d_b_dw[0], d_ln_g[0], d_ln_b[0],
                                   d_w_pw2[0], g_ffn[3], moe_router[1], tm, tm)
    return _moe_layer(x, h, idx, gates, *moe_w, 1, g_final=g_final)
```

```python
import functools
import math

import numpy as np
import jax
import jax.numpy as jnp
from jax import lax
from jax.experimental import pallas as pl
from jax.experimental.pallas import tpu as pltpu

F32 = jnp.float32
BF16 = jnp.bfloat16

RMS_EPS = 1e-6
LN_EPS = 1e-5
MASK_VALUE = -1e30
POOL_WINDOWS = (2, 4, 8, 16)
FOURIER_GROUPS = 4
ATTN_GROUPS = ((128, 1), (512, 4), (2048, 16))
ATTN_HEADS = 8
ATTN_HEAD_DIM = 64
ATTN_WIDTH = ATTN_HEADS * ATTN_HEAD_DIM
CONV_WIDTH = 31
N_EXPERTS = 8
TOP_K = 2

LANES = 128
SUBLANES = 8
HALO = 16
FFT_N2 = 128
MOE_ROWS = 512
MOE_CHUNK = 512
SEG_ALIGN = 16
SEG_PIECES = (512, 256, 128, 64, 32, 16)
LOCAL_ROWS = 1152
VMEM_LIMIT_BYTES = 56 * 1024 * 1024
EXPERT_VMEM_LIMIT_BYTES = 60 * 1024 * 1024


def _cparams(*sem):
    return pltpu.CompilerParams(dimension_semantics=sem, vmem_limit_bytes=VMEM_LIMIT_BYTES)


def _dot(a, b):
    return jnp.dot(a, b, preferred_element_type=F32)


def _rms(x, g):
    ms = jnp.mean(x * x, axis=-1, keepdims=True)
    return x * lax.rsqrt(ms + RMS_EPS) * g


def _swiglu(hb, w1_ref, w3_ref, w2_ref, g_ref, fc):
    f = w1_ref.shape[-1]
    for c in range(0, f, fc):
        a = _dot(hb, w1_ref[:, c:c + fc])
        b = _dot(hb, w3_ref[:, c:c + fc])
        g_ref[:, c:c + fc] = (a * jax.nn.sigmoid(a) * b).astype(BF16)
    return _dot(g_ref[...], w2_ref[...])


def _route(hb, wrt_ref, gate_ref):
    logits = lax.dot_general(wrt_ref[...], hb, (((1,), (1,)), ((), ())), preferred_element_type=F32)
    e_iota = lax.broadcasted_iota(jnp.int32, logits.shape, 0)
    m1 = jnp.max(logits, axis=0, keepdims=True)
    i1 = jnp.min(jnp.where(logits == m1, e_iota, N_EXPERTS), axis=0, keepdims=True)
    rest = jnp.where(e_iota == i1, -jnp.inf, logits)
    m2 = jnp.max(rest, axis=0, keepdims=True)
    i2 = jnp.min(jnp.where(rest == m2, e_iota, N_EXPERTS), axis=0, keepdims=True)
    e2 = jnp.exp(m2 - m1)
    den = 1.0 + e2
    gate_ref[0:1, :] = 1.0 / den
    gate_ref[1:2, :] = e2 / den
    return [i1, i2]


def _local_rows(idx_rows, before_ref, lrow_ref, len_ref):
    e_iota = lax.broadcasted_iota(jnp.int32, (N_EXPERTS, idx_rows[0].shape[1]), 0)
    picks = [jnp.where(row == e_iota, 1.0, 0.0) for row in idx_rows]
    total = picks[0]
    for m in picks[1:]:
        total = total + m
    earlier = _dot(total.astype(BF16), before_ref[...])
    cnt = jnp.sum(total, axis=1, keepdims=True).astype(jnp.int32)
    seg = (cnt + (SEG_ALIGN - 1)) & (-SEG_ALIGN)
    e_col = lax.broadcasted_iota(jnp.int32, (N_EXPERTS, 1), 0)
    loff = jnp.zeros((N_EXPERTS, 1), jnp.int32)
    for e in range(N_EXPERTS - 1):
        loff = loff + jnp.where(e_col > e, seg[e:e + 1, :], 0)
    base = loff.astype(F32) + earlier
    for k in range(TOP_K):
        lrow_ref[0, k:k + 1, :] = jnp.sum(picks[k] * base, axis=0, keepdims=True).astype(jnp.int32)
        base = base + picks[k]
    len_ref[0] = jnp.broadcast_to(seg, (N_EXPERTS, LANES))


def _moe_prologue(x1, gf_ref, wrt_ref, before_ref, x_out_ref, h_ref, gate_ref, lrow_ref, len_ref):
    x_out_ref[0] = x1
    hb = _rms(x1, gf_ref[...]).astype(BF16)
    h_ref[0] = hb
    _local_rows(_route(hb, wrt_ref, gate_ref), before_ref, lrow_ref, len_ref)


def _moe_prologue_specs(tm, nt, x_shape):
    assert tm == MOE_CHUNK, "the mixer tile is the MoE token chunk"
    bsz, seq, d = x_shape
    n_tok = bsz * seq
    chunk_of = lambda b, i: b * nt + i
    in_specs = [_full((1, d)), _full((N_EXPERTS, d)), _full((tm, tm))]
    out_specs = [pl.BlockSpec((1, tm, d), lambda b, i: (b, i, 0)),
                 pl.BlockSpec((1, tm, d), lambda b, i: (b, i, 0)),
                 pl.BlockSpec((TOP_K, tm), lambda b, i: (0, chunk_of(b, i))),
                 pl.BlockSpec((1, TOP_K, tm), lambda b, i: (chunk_of(b, i), 0, 0)),
                 pl.BlockSpec((1, N_EXPERTS, LANES), lambda b, i: (chunk_of(b, i), 0, 0))]
    out_shape = [jax.ShapeDtypeStruct(x_shape, F32), jax.ShapeDtypeStruct(x_shape, BF16),
                 jax.ShapeDtypeStruct((TOP_K, n_tok), F32),
                 jax.ShapeDtypeStruct((n_tok // tm, TOP_K, tm), jnp.int32),
                 jax.ShapeDtypeStruct((n_tok // tm, N_EXPERTS, LANES), jnp.int32)]
    return in_specs, out_specs, out_shape


def _moe_prologue_args(g_ffn, w_router, tm):
    t_iota = jnp.arange(tm, dtype=jnp.int32)
    before = (t_iota[:, None] < t_iota[None, :]).astype(BF16)
    return g_ffn.reshape(1, -1), w_router.T.astype(BF16), before


def _pool_ffn_kernel(xp_ref, xc_ref, xn_ref, gm_ref, wg_ref, sc_ref, gf_ref, w1_ref, w3_ref, w2_ref,
                     o_ref, hid_ref, *, ts, seq, fc):
    i = pl.program_id(1)
    xc = xc_ref[0]
    d = xc.shape[1]
    gd = d // len(POOL_WINDOWS)
    n_ext = ts + 2 * HALO
    xe = jnp.concatenate([xp_ref[0], xc, xn_ref[0]], axis=0)
    he = _rms(xe, gm_ref[...])
    pos = i * ts - HALO + lax.broadcasted_iota(jnp.int32, (n_ext, 1), 0)
    he = jnp.where((pos >= 0) & (pos < seq), he, 0.0)

    s = he + pltpu.roll(he, 1, 0)
    sums = [s]
    half = 1
    for _ in POOL_WINDOWS[1:]:
        s = s[:, gd:]
        s = pltpu.roll(s, half, 0) + pltpu.roll(s, n_ext - half, 0)
        sums.append(s)
        half *= 2

    tpos = i * ts + lax.broadcasted_iota(jnp.int32, (ts, 1), 0)
    ys = []
    for gi, w in enumerate(POOL_WINDOWS):
        cnt = jnp.minimum(tpos + w // 2, seq) - jnp.maximum(tpos - w // 2, 0)
        mean = sums[gi][HALO:HALO + ts, :gd] / cnt.astype(F32)
        pooled = mean - he[HALO:HALO + ts, gi * gd:(gi + 1) * gd]
        ys.append(_dot(pooled.astype(BF16), wg_ref[gi]))
    x1 = xc + jnp.concatenate(ys, axis=1) * sc_ref[...]
    hb = _rms(x1, gf_ref[...]).astype(BF16)
    o_ref[0] = x1 + _swiglu(hb, w1_ref.at[0], w3_ref.at[0], w2_ref.at[0], hid_ref, fc)


def _halo_specs(ts, seq, d):
    r = ts // HALO
    last = seq // HALO - 1
    return [
        pl.BlockSpec((1, HALO, d), lambda b, i: (b, jnp.maximum(i * r - 1, 0), 0)),
        pl.BlockSpec((1, ts, d), lambda b, i: (b, i, 0)),
        pl.BlockSpec((1, HALO, d), lambda b, i: (b, jnp.minimum((i + 1) * r, last), 0)),
    ]


def _full(shape):
    n = len(shape)
    return pl.BlockSpec(shape, lambda *_: (0,) * n)


def _stacked(shape, layer):
    n = len(shape)
    return pl.BlockSpec((1,) + tuple(shape), lambda *_: (layer,) + (0,) * n, pipeline_mode=pl.Buffered(1))


def _pool_ffn_layer(x, g_mix, w_grp, scale, g_ffn, w1, w3, w2, layer, ts):
    bsz, seq, d = x.shape
    f = w1.shape[2]
    kern = functools.partial(_pool_ffn_kernel, ts=ts, seq=seq, fc=256)
    return pl.pallas_call(
        kern,
        grid=(bsz, seq // ts),
        in_specs=_halo_specs(ts, seq, d) + [
            _full((1, d)), _full(w_grp.shape), _full((1, d)), _full((1, d)),
            _stacked((d, f), layer), _stacked((d, f), layer), _stacked((f, d), layer)],
        out_specs=pl.BlockSpec((1, ts, d), lambda b, i: (b, i, 0)),
        out_shape=jax.ShapeDtypeStruct(x.shape, F32),
        scratch_shapes=[pltpu.VMEM((ts, f), BF16)],
        compiler_params=_cparams("parallel", "parallel"),
        name="pool_ffn",
    )(x, x, x, g_mix.reshape(1, d), w_grp.astype(BF16), scale.reshape(1, d), g_ffn.reshape(1, d), w1, w3, w2)


def _fft_stage1_kernel(x_ref, g_ref, f1_ref, o_ref, s_ref, *, nseg):
    n1, d = x_ref.shape[1], x_ref.shape[3]
    nlc = d // LANES
    for c in range(nlc):
        s_ref[c] = x_ref[0, :, :, c * LANES:(c + 1) * LANES].reshape(n1 * nseg, LANES)
    for j in range(nseg):
        xj = jnp.concatenate([s_ref[c, pl.ds(j, n1, stride=nseg), :] for c in range(nlc)], axis=1)
        h = _rms(xj, g_ref[...]).astype(BF16)
        o_ref[0, j] = _dot(f1_ref[...], h).astype(BF16)


def _fft_stage2_kernel(br_ref, bi_ref, m_ref, o_ref, s_ref, *, nk):
    n2, d = FFT_N2, br_ref.shape[3]
    nlc = d // LANES
    for p, ref in enumerate((br_ref, bi_ref)):
        for c in range(nlc):
            s_ref[p, c] = ref[0, :, :, c * LANES:(c + 1) * LANES].reshape(n2 * nk, LANES).astype(F32)
    for j in range(nk):
        rhs = jnp.concatenate(
            [jnp.concatenate([s_ref[p, c, pl.ds(j, n2, stride=nk), :] for c in range(nlc)], axis=1)
             for p in range(2)], axis=0).astype(BF16)
        res = _dot(m_ref[j], rhs)
        for p in range(2):
            for c in range(nlc):
                s_ref[p, c, pl.ds(j, n2, stride=nk), :] = res[p * n2:(p + 1) * n2, c * LANES:(c + 1) * LANES]
    for p in range(2):
        for c in range(nlc):
            o_ref[0, p, :, :, c * LANES:(c + 1) * LANES] = s_ref[p, c].astype(BF16).reshape(n2, nk, LANES)


def _fourier_out_kernel(x_ref, ar_ref, ai_ref, cc_ref, sc_ref, wo_ref, gf_ref, wrt_ref, before_ref,
                        xo_ref, h_ref, gate_ref, lrow_ref, len_ref, *, norm):
    d = x_ref.shape[2]
    gd = d // FOURIER_GROUPS
    fs = []
    for gi in range(FOURIER_GROUPS):
        sl = slice(gi * gd, (gi + 1) * gd)
        fs.append(_dot(ar_ref[0, 0, :, sl].astype(BF16), cc_ref[...])
                  + _dot(ai_ref[0, 0, :, sl].astype(BF16), sc_ref[...]))
    f = (jnp.concatenate(fs, axis=1) * norm).astype(BF16)
    x1 = x_ref[0] + _dot(f, wo_ref[...])
    _moe_prologue(x1, gf_ref, wrt_ref, before_ref, xo_ref, h_ref, gate_ref, lrow_ref, len_ref)


def _dft_tables(seq, gd):
    n2 = FFT_N2
    n1 = seq // n2
    k = np.arange(n1)
    ang1 = 2.0 * np.pi * ((k[:, None] * k[None, :]) % n1) / n1
    f1 = np.concatenate([np.cos(ang1), -np.sin(ang1)], axis=0)
    c = np.arange(gd)
    angc = 2.0 * np.pi * ((c[:, None] * c[None, :]) % gd) / gd
    return (jnp.asarray(f1, BF16), jnp.asarray(np.cos(angc), BF16), jnp.asarray(np.sin(angc), BF16))


def _twiddled_dft(seq):
    n2 = FFT_N2
    n1 = seq // n2
    k1 = lax.broadcasted_iota(jnp.int32, (n1, n2, n2), 0)
    k2 = lax.broadcasted_iota(jnp.int32, (n1, n2, n2), 1)
    a = lax.broadcasted_iota(jnp.int32, (n1, n2, n2), 2)
    ang = ((a * (n1 * k2 + k1)) % seq).astype(F32) * (2.0 * math.pi / seq)
    mr = jnp.cos(ang)
    mi = -jnp.sin(ang)
    top = jnp.concatenate([mr, -mi], axis=2)
    bot = jnp.concatenate([mi, mr], axis=2)
    return jnp.concatenate([top, bot], axis=1).astype(BF16)


def _fourier_layer(x, g_mix, w_out, g_ffn, w_router, tm):
    bsz, seq, d = x.shape
    n2 = FFT_N2
    n1 = seq // n2
    gd = d // FOURIER_GROUPS
    f1, cc, sc = _dft_tables(seq, gd)
    mtab = _twiddled_dft(seq)
    nseg = 8
    b1 = pl.pallas_call(
        functools.partial(_fft_stage1_kernel, nseg=nseg),
        grid=(bsz, n2 // nseg),
        in_specs=[pl.BlockSpec((1, n1, nseg, d), lambda b, t: (b, 0, t, 0)), _full((1, d)), _full(f1.shape)],
        out_specs=pl.BlockSpec((1, nseg, 2 * n1, d), lambda b, t: (b, t, 0, 0)),
        out_shape=jax.ShapeDtypeStruct((bsz, n2, 2 * n1, d), BF16),
        scratch_shapes=[pltpu.VMEM((d // LANES, n1 * nseg, LANES), F32)],
        compiler_params=_cparams("parallel", "parallel"),
        name="fft_stage1",
    )(x.reshape(bsz, n1, n2, d), g_mix.reshape(1, d), f1)

    nk = 16
    a = pl.pallas_call(
        functools.partial(_fft_stage2_kernel, nk=nk),
        grid=(bsz, n1 // nk),
        in_specs=[pl.BlockSpec((1, n2, nk, d), lambda b, t: (b, 0, t, 0)),
                  pl.BlockSpec((1, n2, nk, d), lambda b, t: (b, 0, n1 // nk + t, 0)),
                  pl.BlockSpec((nk, 2 * n2, 2 * n2), lambda b, t: (t, 0, 0))],
        out_specs=pl.BlockSpec((1, 2, n2, nk, d), lambda b, t: (b, 0, 0, t, 0)),
        out_shape=jax.ShapeDtypeStruct((bsz, 2, n2, n1, d), BF16),
        scratch_shapes=[pltpu.VMEM((2, d // LANES, n2 * nk, LANES), F32)],
        compiler_params=_cparams("parallel", "parallel"),
        name="fft_stage2",
    )(b1, b1, mtab)

    av = a.reshape(bsz, 2, seq, d)
    nt = seq // tm
    moe_in, moe_out, moe_shape = _moe_prologue_specs(tm, nt, x.shape)
    return pl.pallas_call(
        functools.partial(_fourier_out_kernel, norm=1.0 / math.sqrt(seq * gd)),
        grid=(bsz, nt),
        in_specs=[pl.BlockSpec((1, tm, d), lambda b, i: (b, i, 0)),
                  pl.BlockSpec((1, 1, tm, d), lambda b, i: (b, 0, i, 0)),
                  pl.BlockSpec((1, 1, tm, d), lambda b, i: (b, 1, i, 0)),
                  _full((gd, gd)), _full((gd, gd)), _full((d, d))] + moe_in,
        out_specs=moe_out,
        out_shape=moe_shape,
        compiler_params=_cparams("parallel", "parallel"),
        name="fourier_out",
    )(x, av, av, cc, sc, w_out.astype(BF16), *_moe_prologue_args(g_ffn, w_router, tm))


def _moe_plan(lrow_lanes, seg_len, n_tok):
    chunk, rows = MOE_CHUNK, MOE_ROWS
    nc = n_tok // chunk
    seg_len = seg_len[:, :, 0]
    loff = jnp.cumsum(seg_len, axis=1) - seg_len
    region = (jnp.sum(seg_len, axis=0) + rows - 1) // rows * rows
    pend = jnp.cumsum(region)
    goff = (pend - region)[None, :] + jnp.cumsum(seg_len, axis=0) - seg_len
    n_blocks = (n_tok * TOP_K + nc * N_EXPERTS * (SEG_ALIGN - 1)) // rows + N_EXPERTS
    blk = jnp.arange(n_blocks, dtype=jnp.int32)
    block_exp = jnp.minimum(jnp.sum((blk[:, None] * rows >= pend[None, :]).astype(jnp.int32), axis=1),
                            N_EXPERTS - 1)
    return dict(
        n_blocks=n_blocks,
        seg_len=seg_len.reshape(-1), loff=loff.reshape(-1).astype(jnp.int32), goff=goff.reshape(-1).astype(jnp.int32),
        gap_off=(pend - region + jnp.sum(seg_len, axis=0)).astype(jnp.int32),
        gap_len=(region - jnp.sum(seg_len, axis=0)).astype(jnp.int32),
        lrow_lanes=lrow_lanes,
        lrow_cols=lrow_lanes.transpose(0, 2, 1).reshape(n_tok, TOP_K),
        n_used=(pend[-1] // rows).reshape(1).astype(jnp.int32), block_exp=block_exp)


def _segment_copies(c, len_ref, loff_ref, goff_ref, make, act):
    for e in range(N_EXPERTS):
        ln = len_ref[c * N_EXPERTS + e]
        lo = loff_ref[c * N_EXPERTS + e]
        go = goff_ref[c * N_EXPERTS + e]
        for size in SEG_PIECES:
            done = ln & (-2 * size)

            @pl.when((ln & size) != 0)
            def _():
                act(make(pl.multiple_of(lo + done, SEG_ALIGN), pl.multiple_of(go + done, SEG_ALIGN), size))


def _zero_fill_copies(gap_off_ref, gap_len_ref, nu_ref, zero_ref, xb_ref, sem, act):
    rows = zero_ref.shape[0]
    for e in range(N_EXPERTS):
        ln = gap_len_ref[e]
        off = gap_off_ref[e]
        for size in SEG_PIECES[1:]:
            done = ln & (-2 * size)

            @pl.when((ln & size) != 0)
            def _():
                act(pltpu.make_async_copy(zero_ref.at[pl.ds(0, size)],
                                          xb_ref.at[pl.ds(pl.multiple_of(off + done, SEG_ALIGN), size)], sem))

    def block(b, carry):
        act(pltpu.make_async_copy(zero_ref, xb_ref.at[pl.ds(pl.multiple_of(b * rows, rows), rows)], sem))
        return carry

    lax.fori_loop(nu_ref[0], xb_ref.shape[0] // rows, block, 0)


def _dispatch_kernel(len_ref, loff_ref, goff_ref, gap_off_ref, gap_len_ref, nu_ref, h_ref, lrow_ref, xb_ref,
                     stage_ref, zero_ref, sem, zsem):
    c = pl.program_id(0)
    slot = c % 2
    n_local, chunk = stage_ref.shape[1], h_ref.shape[0]

    @pl.when(c == 0)
    def _():
        zero_ref[...] = jnp.zeros_like(zero_ref)
        _zero_fill_copies(gap_off_ref, gap_len_ref, nu_ref, zero_ref, xb_ref, zsem, lambda cp: cp.start())

    def make(slot_):
        def _make(lr, gr, size):
            return pltpu.make_async_copy(stage_ref.at[slot_, pl.ds(lr, size)], xb_ref.at[pl.ds(gr, size)],
                                         sem.at[slot_])
        return _make

    r_iota = lax.broadcasted_iota(jnp.int32, (n_local, chunk), 0)
    sel = (lrow_ref[0, 0:1, :] == r_iota) | (lrow_ref[0, 1:2, :] == r_iota)
    stage_ref[slot] = _dot(jnp.where(sel, 1.0, 0.0).astype(BF16), h_ref[...]).astype(BF16)

    @pl.when(c > 0)
    def _():
        _segment_copies(c - 1, len_ref, loff_ref, goff_ref, make(1 - slot), lambda cp: cp.wait())

    _segment_copies(c, len_ref, loff_ref, goff_ref, make(slot), lambda cp: cp.start())

    @pl.when(c == pl.num_programs(0) - 1)
    def _():
        _segment_copies(c, len_ref, loff_ref, goff_ref, make(slot), lambda cp: cp.wait())
        _zero_fill_copies(gap_off_ref, gap_len_ref, nu_ref, zero_ref, xb_ref, zsem, lambda cp: cp.wait())


def _expert_kernel(be_ref, nu_ref, xb_ref, w1_ref, w3_ref, w2_ref, y_ref, hid_ref, *, fc):
    b = pl.program_id(0)

    @pl.when(b < nu_ref[0])
    def _():
        y_ref[...] = _swiglu(xb_ref[...], w1_ref.at[0, 0], w3_ref.at[0, 0], w2_ref.at[0, 0], hid_ref,
                             fc).astype(BF16)

    @pl.when(b >= nu_ref[0])
    def _():
        y_ref[...] = jnp.zeros_like(y_ref)


def _combine_kernel(len_ref, loff_ref, goff_ref, y_ref, lcol_ref, gcol_ref, x_ref, g_ref, o_ref, ybuf_ref, sem,
                    *, final_norm):
    c = pl.program_id(0)
    nc = pl.num_programs(0)
    slot = c % 2
    chunk, n_local = x_ref.shape[0], ybuf_ref.shape[1]

    def make(slot_):
        def _make(lr, gr, size):
            return pltpu.make_async_copy(y_ref.at[pl.ds(gr, size)], ybuf_ref.at[slot_, pl.ds(lr, size)],
                                         sem.at[slot_])
        return _make

    @pl.when(c == 0)
    def _():
        ybuf_ref[...] = jnp.zeros_like(ybuf_ref)
        _segment_copies(c, len_ref, loff_ref, goff_ref, make(slot), lambda cp: cp.start())

    @pl.when(c + 1 < nc)
    def _():
        _segment_copies(c + 1, len_ref, loff_ref, goff_ref, make(1 - slot), lambda cp: cp.start())

    _segment_copies(c, len_ref, loff_ref, goff_ref, make(slot), lambda cp: cp.wait())

    yl = ybuf_ref[slot]
    r_iota = lax.broadcasted_iota(jnp.int32, (chunk, n_local), 1)
    q = None
    for k in range(TOP_K):
        qk = jnp.where(lcol_ref[:, k:k + 1] == r_iota, gcol_ref[:, k:k + 1], 0.0)
        q = qk if q is None else q + qk
    xo = x_ref[...] + _dot(q.astype(BF16), yl)
    o_ref[...] = _rms(xo, g_ref[...]) if final_norm else xo


def _moe_layer(routed, w1, w3, w2, layer, g_final=None):
    x, h, gates, lrow_lanes, seg_len = routed
    bsz, seq, d = x.shape
    n_tok = bsz * seq
    f = w1.shape[3]
    rows, chunk = MOE_ROWS, MOE_CHUNK
    nc = n_tok // chunk
    plan = _moe_plan(lrow_lanes, seg_len, n_tok)
    n_blocks = plan['n_blocks']
    n_rows = n_blocks * rows
    seg = (plan['seg_len'], plan['loff'], plan['goff'])

    xb = pl.pallas_call(
        _dispatch_kernel,
        grid_spec=pltpu.PrefetchScalarGridSpec(
            num_scalar_prefetch=6, grid=(nc,),
            in_specs=[pl.BlockSpec((chunk, d), lambda c, *_: (c, 0)),
                      pl.BlockSpec((1, TOP_K, chunk), lambda c, *_: (c, 0, 0))],
            out_specs=pl.BlockSpec(memory_space=pl.ANY),
            scratch_shapes=[pltpu.VMEM((2, LOCAL_ROWS, d), BF16), pltpu.VMEM((rows, d), BF16),
                            pltpu.SemaphoreType.DMA((2,)), pltpu.SemaphoreType.DMA(())]),
        out_shape=jax.ShapeDtypeStruct((n_rows, d), BF16),
        compiler_params=_cparams("arbitrary"),
        name="moe_dispatch",
    )(*seg, plan['gap_off'], plan['gap_len'], plan['n_used'], h.reshape(n_tok, d), plan['lrow_lanes'])

    def blk_idx(b, nu):
        return jnp.minimum(b, nu[0] - 1)

    y = pl.pallas_call(
        functools.partial(_expert_kernel, fc=256),
        grid_spec=pltpu.PrefetchScalarGridSpec(
            num_scalar_prefetch=2, grid=(n_blocks,),
            in_specs=[pl.BlockSpec((rows, d), lambda b, be, nu: (blk_idx(b, nu), 0)),
                      pl.BlockSpec((1, 1, d, f), lambda b, be, nu: (layer, be[blk_idx(b, nu)], 0, 0)),
                      pl.BlockSpec((1, 1, d, f), lambda b, be, nu: (layer, be[blk_idx(b, nu)], 0, 0)),
                      pl.BlockSpec((1, 1, f, d), lambda b, be, nu: (layer, be[blk_idx(b, nu)], 0, 0))],
            out_specs=pl.BlockSpec((rows, d), lambda b, be, nu: (b, 0)),
            scratch_shapes=[pltpu.VMEM((rows, f), BF16)]),
        out_shape=jax.ShapeDtypeStruct((n_rows, d), BF16),
        compiler_params=pltpu.CompilerParams(dimension_semantics=("arbitrary",),
                                             vmem_limit_bytes=EXPERT_VMEM_LIMIT_BYTES),
        name="moe_experts",
    )(plan['block_exp'], plan['n_used'], xb, w1, w3, w2)

    final_norm = g_final is not None
    g = (g_final if final_norm else jnp.ones((d,), F32)).reshape(1, d)
    out = pl.pallas_call(
        functools.partial(_combine_kernel, final_norm=final_norm),
        grid_spec=pltpu.PrefetchScalarGridSpec(
            num_scalar_prefetch=3, grid=(nc,),
            in_specs=[pl.BlockSpec(memory_space=pl.ANY),
                      pl.BlockSpec((chunk, TOP_K), lambda c, *_: (c, 0)),
                      pl.BlockSpec((chunk, TOP_K), lambda c, *_: (c, 0)),
                      pl.BlockSpec((chunk, d), lambda c, *_: (c, 0)),
                      pl.BlockSpec((1, d), lambda c, *_: (0, 0))],
            out_specs=pl.BlockSpec((chunk, d), lambda c, *_: (c, 0)),
            scratch_shapes=[pltpu.VMEM((2, LOCAL_ROWS, d), BF16), pltpu.SemaphoreType.DMA((2,))]),
        out_shape=jax.ShapeDtypeStruct((n_tok, d), F32),
        compiler_params=_cparams("arbitrary"),
        name="moe_combine",
    )(*seg, y, plan['lrow_cols'], gates.T, x.reshape(n_tok, d), g)
    return out.reshape(bsz, seq, d)


def _qkv_kernel(x_ref, g_ref, w_ref, *rest, nc):
    o_refs, h_ref = rest[:-1], rest[-1]
    tm = x_ref.shape[1]
    gw = 3 * ATTN_WIDTH
    h = _rms(x_ref[0], g_ref[...])
    nlc = h.shape[1] // LANES
    for c in range(nlc):
        h_ref[c] = h[:, c * LANES:(c + 1) * LANES]
    for gi, (_, dil) in enumerate(ATTN_GROUPS):
        n = tm // dil
        hb = h if dil == 1 else jnp.concatenate(
            [jnp.concatenate([h_ref[c, pl.ds(r, n, stride=dil), :] for c in range(nlc)], axis=1)
             for r in range(dil)], axis=0)
        hb = hb.astype(BF16)
        for c in range(gi * gw, (gi + 1) * gw, nc):
            res = _dot(hb, w_ref[:, c:c + nc]).astype(BF16)
            for r in range(dil):
                o_refs[gi][0, :, r * gw + c - gi * gw:r * gw + c - gi * gw + nc] = res[r * n:(r + 1) * n]


def _attn_kernel(q_ref, kp_ref, kc_ref, kn_ref, vp_ref, vc_ref, vn_ref, o_ref, lse_ref,
                 *, tq, seg, dil, radius, slopes):
    j = pl.program_id(2)
    hd = ATTN_HEAD_DIM
    tb = q_ref.shape[1]
    tk = tq + 2 * radius
    q_all = q_ref[0] * jnp.asarray(1.0 / math.sqrt(hd), BF16)
    k_all = jnp.concatenate([kp_ref[0], kc_ref[0], kn_ref[0]], axis=0)
    v_all = jnp.concatenate([vp_ref[0], vc_ref[0], vn_ref[0]], axis=0)
    qi = lax.broadcasted_iota(jnp.int32, (tq, tk), 0)
    kj = lax.broadcasted_iota(jnp.int32, (tq, tk), 1)
    arel = jnp.abs(kj - radius - qi)
    band = arel <= radius
    dist = (dil * arel).astype(F32)
    alibi = [-(slope * dist) for slope in slopes]
    low = lax.broadcasted_iota(jnp.int32, (1, 2 * hd), 1) < hd
    for t in range(tb // tq):
        rows = slice(t * tq, (t + 1) * tq)
        kpos = j * tb + t * tq - radius + kj
        valid = band & (kpos >= 0) & (kpos < seg)
        q, k, v = q_all[rows], k_all[t * tq:t * tq + tk], v_all[t * tq:t * tq + tk]
        for hp in range(ATTN_HEADS // 2):
            sl = slice(hp * 2 * hd, (hp + 1) * 2 * hd)
            qp, kp, vp = q[:, sl], k[:, sl], v[:, sl]
            o_pair, l_pair = None, None
            for sub in range(2):
                mine = low if sub == 0 else jnp.logical_not(low)
                qm = jnp.where(mine, qp, jnp.zeros_like(qp))
                s = lax.dot_general(qm, kp, (((1,), (1,)), ((), ())), preferred_element_type=F32)
                s = jnp.where(valid, s + alibi[2 * hp + sub], MASK_VALUE)
                m = jnp.max(s, axis=-1, keepdims=True)
                p = jnp.exp(s - m)
                den = jnp.sum(p, axis=-1, keepdims=True)
                o = _dot(p.astype(BF16), vp) / den
                l = jnp.broadcast_to(m + jnp.log(den), o.shape)
                o_pair = o if sub == 0 else jnp.where(low, o_pair, o)
                l_pair = l if sub == 0 else jnp.where(low, l_pair, l)
            o_ref[0, rows, sl] = o_pair.astype(BF16)
            lse_ref[0, rows, sl] = l_pair


def _attn_out_ffn_kernel(x_ref, o0_ref, o1_ref, o2_ref, l0_ref, l1_ref, l2_ref, wo_ref, gf_ref,
                         w1_ref, w3_ref, w2_ref, out_ref, so_ref, sl_ref, hid_ref, *, fc):
    aw = ATTN_WIDTH
    ls, os_ = [], []
    for gi, (o_ref, l_ref) in enumerate(((o0_ref, l0_ref), (o1_ref, l1_ref), (o2_ref, l2_ref))):
        dil = ATTN_GROUPS[gi][1]
        if dil == 1:
            os_.append(o_ref[0].astype(F32))
            ls.append(l_ref[0])
            continue
        n = o_ref.shape[1]
        nlc = aw // LANES
        for r in range(dil):
            ov = o_ref[0, :, r * aw:(r + 1) * aw].astype(F32)
            lv = l_ref[0, :, r * aw:(r + 1) * aw]
            for c in range(nlc):
                so_ref[gi - 1, c, pl.ds(r, n, stride=dil), :] = ov[:, c * LANES:(c + 1) * LANES]
                sl_ref[gi - 1, c, pl.ds(r, n, stride=dil), :] = lv[:, c * LANES:(c + 1) * LANES]
        os_.append(jnp.concatenate([so_ref[gi - 1, c] for c in range(nlc)], axis=1))
        ls.append(jnp.concatenate([sl_ref[gi - 1, c] for c in range(nlc)], axis=1))
    m = jnp.maximum(jnp.maximum(ls[0], ls[1]), ls[2])
    es = [jnp.exp(l - m) for l in ls]
    den = es[0] + es[1] + es[2]
    o = (es[0] * os_[0] + es[1] * os_[1] + es[2] * os_[2]) / den
    x1 = x_ref[0] + _dot(o.astype(BF16), wo_ref[...])
    hb = _rms(x1, gf_ref[...]).astype(BF16)
    out_ref[0] = x1 + _swiglu(hb, w1_ref.at[0], w3_ref.at[0], w2_ref.at[0], hid_ref, fc)


def _alibi_slopes():
    n = len(ATTN_GROUPS) * ATTN_HEADS
    s = np.float32(2.0) ** (np.float32(-8.0) * np.arange(1, n + 1, dtype=np.float32) / np.float32(n))
    return [float(v) for v in s]


def _attention_ffn_layer(x, g_mix, w_qkv, w_out, g_ffn, w1, w3, w2, layer, tm):
    bsz, seq, d = x.shape
    nqkv = w_qkv.shape[1]
    aw = ATTN_WIDTH
    gw = 3 * aw
    assert nqkv == len(ATTN_GROUPS) * gw
    qkvs = pl.pallas_call(
        functools.partial(_qkv_kernel, nc=aw),
        grid=(bsz, seq // tm),
        in_specs=[pl.BlockSpec((1, tm, d), lambda b, i: (b, i, 0)), _full((1, d)), _full((d, nqkv))],
        out_specs=[pl.BlockSpec((1, tm // dil, dil * gw), lambda b, i: (b, i, 0)) for _, dil in ATTN_GROUPS],
        out_shape=[jax.ShapeDtypeStruct((bsz, seq // dil, dil * gw), BF16) for _, dil in ATTN_GROUPS],
        scratch_shapes=[pltpu.VMEM((d // LANES, tm, LANES), F32)],
        compiler_params=_cparams("parallel", "parallel"),
        name="qkv_proj",
    )(x, g_mix.reshape(1, d), w_qkv.astype(BF16))

    slopes = _alibi_slopes()
    outs, lses = [], []
    for gi, (window, dil) in enumerate(ATTN_GROUPS):
        radius = window // (2 * dil)
        seg = seq // dil
        tb = min(512, seg)
        tq = min(128, seg)
        hr = tb // radius
        last = seg // radius - 1
        view = qkvs[gi]

        def col(part, r):
            return r * 3 + part

        def cur(part):
            return pl.BlockSpec((1, tb, aw), lambda b, r, j, part=part: (b, j, col(part, r)))

        def prev(part):
            return pl.BlockSpec((1, radius, aw),
                                lambda b, r, j, part=part: (b, jnp.maximum(j * hr - 1, 0), col(part, r)))

        def nxt(part):
            return pl.BlockSpec((1, radius, aw),
                                lambda b, r, j, part=part: (b, jnp.minimum((j + 1) * hr, last), col(part, r)))

        o, lse = pl.pallas_call(
            functools.partial(_attn_kernel, tq=tq, seg=seg, dil=dil, radius=radius,
                              slopes=slopes[gi * ATTN_HEADS:(gi + 1) * ATTN_HEADS]),
            grid=(bsz, dil, seg // tb),
            in_specs=[cur(0), prev(1), cur(1), nxt(1), prev(2), cur(2), nxt(2)],
            out_specs=[pl.BlockSpec((1, tb, aw), lambda b, r, j: (b, j, r)),
                       pl.BlockSpec((1, tb, aw), lambda b, r, j: (b, j, r))],
            out_shape=[jax.ShapeDtypeStruct((bsz, seg, dil * aw), BF16),
                       jax.ShapeDtypeStruct((bsz, seg, dil * aw), F32)],
            compiler_params=_cparams("parallel", "parallel", "parallel"),
            name=f"dilated_attn_{dil}",
        )(view, view, view, view, view, view, view)
        outs.append(o)
        lses.append(lse)

    f = w1.shape[2]
    row = lambda width: pl.BlockSpec((1, tm, width), lambda b, i: (b, i, 0))
    dilated = [pl.BlockSpec((1, tm // dil, dil * aw), lambda b, i: (b, i, 0)) for _, dil in ATTN_GROUPS]
    return pl.pallas_call(
        functools.partial(_attn_out_ffn_kernel, fc=256),
        grid=(bsz, seq // tm),
        in_specs=[row(d)] + dilated + dilated + [
            _full((aw, d)), _full((1, d)),
            _stacked((d, f), layer), _stacked((d, f), layer), _stacked((f, d), layer)],
        out_specs=row(d),
        out_shape=jax.ShapeDtypeStruct(x.shape, F32),
        scratch_shapes=[pltpu.VMEM((len(ATTN_GROUPS) - 1, aw // LANES, tm, LANES), F32),
                        pltpu.VMEM((len(ATTN_GROUPS) - 1, aw // LANES, tm, LANES), F32),
                        pltpu.VMEM((tm, f), BF16)],
        compiler_params=_cparams("parallel", "parallel"),
        name="attn_out_ffn",
    )(x, *outs, *lses, w_out.astype(BF16), g_ffn.reshape(1, d), w1, w3, w2)


def _conv_glu_kernel(x_ref, g_ref, w_ref, u_ref):
    d = x_ref.shape[2]
    hb = _rms(x_ref[0], g_ref[...]).astype(BF16)
    a = _dot(hb, w_ref[:, :d])
    b = _dot(hb, w_ref[:, d:])
    u_ref[0] = a * jax.nn.sigmoid(b)


def _conv_out_kernel(x_ref, up_ref, uc_ref, un_ref, wdw_ref, bdw_ref, lg_ref, lb_ref, w2_ref,
                     gf_ref, wrt_ref, before_ref, xo_ref, h_ref, gate_ref, lrow_ref, len_ref, ext_ref, z_ref,
                     *, ts, seq, rows, lanes):
    i = pl.program_id(1)
    n_ext = ts + 2 * HALO
    ue = jnp.concatenate([up_ref[0], uc_ref[0], un_ref[0]], axis=0)
    pos = i * ts - HALO + lax.broadcasted_iota(jnp.int32, (n_ext, 1), 0)
    ext_ref[0] = jnp.where((pos >= 0) & (pos < seq), ue, 0.0)
    for r in range(1, SUBLANES):
        ext_ref[r, 0:n_ext - SUBLANES, :] = ext_ref[0, r:r + n_ext - SUBLANES, :]
    half = CONV_WIDTH // 2

    def row_chunk(c, carry):
        base = pl.multiple_of(c * rows, rows)
        for lo in range(0, z_ref.shape[1], lanes):
            acc = None
            for k in range(CONV_WIDTH):
                start = HALO + k - half
                r = start % SUBLANES
                first = pl.multiple_of(base + (start - r), SUBLANES)
                term = (ext_ref[r, pl.ds(first, rows), lo:lo + lanes].reshape(rows // SUBLANES, SUBLANES, lanes)
                        * wdw_ref[k, :, lo:lo + lanes])
                acc = term if acc is None else acc + term
            z_ref[pl.ds(base, rows), lo:lo + lanes] = acc.reshape(rows, lanes)
        return carry

    lax.fori_loop(0, ts // rows, row_chunk, 0)
    u = z_ref[...] + bdw_ref[...]
    mu = jnp.mean(u, axis=-1, keepdims=True)
    uc = u - mu
    var = jnp.mean(uc * uc, axis=-1, keepdims=True)
    z = uc * lax.rsqrt(var + LN_EPS) * lg_ref[...] + lb_ref[...]
    z = (z * jax.nn.sigmoid(z)).astype(BF16)
    x1 = x_ref[0] + _dot(z, w2_ref[...])
    _moe_prologue(x1, gf_ref, wrt_ref, before_ref, xo_ref, h_ref, gate_ref, lrow_ref, len_ref)


def _conv_layer(x, g_mix, w_pw1, w_dw, b_dw, ln_g, ln_b, w_pw2, g_ffn, w_router, ts):
    bsz, seq, d = x.shape
    nt = seq // ts
    vec = lambda a: a.reshape(1, d)
    u = pl.pallas_call(
        _conv_glu_kernel,
        grid=(bsz, nt),
        in_specs=[pl.BlockSpec((1, ts, d), lambda b, i: (b, i, 0)), _full((1, d)), _full((d, 2 * d))],
        out_specs=pl.BlockSpec((1, ts, d), lambda b, i: (b, i, 0)),
        out_shape=jax.ShapeDtypeStruct(x.shape, F32),
        compiler_params=_cparams("parallel", "parallel"),
        name="conv_glu",
    )(x, vec(g_mix), w_pw1.astype(BF16))

    moe_in, moe_out, moe_shape = _moe_prologue_specs(ts, nt, x.shape)
    return pl.pallas_call(
        functools.partial(_conv_out_kernel, ts=ts, seq=seq, rows=min(32, ts), lanes=min(2 * LANES, d)),
        grid=(bsz, nt),
        in_specs=[pl.BlockSpec((1, ts, d), lambda b, i: (b, i, 0))] + _halo_specs(ts, seq, d) + [
            _full((CONV_WIDTH, SUBLANES, d)), _full((1, d)), _full((1, d)), _full((1, d)), _full((d, d))] + moe_in,
        out_specs=moe_out,
        out_shape=moe_shape,
        scratch_shapes=[pltpu.VMEM((SUBLANES, ts + 2 * HALO, d), F32), pltpu.VMEM((ts, d), F32)],
        compiler_params=_cparams("parallel", "parallel"),
        name="conv_out",
    )(x, u, u, u, jnp.broadcast_to(w_dw[:, None, :], (CONV_WIDTH, SUBLANES, d)), vec(b_dw), vec(ln_g), vec(ln_b),
      w_pw2.astype(BF16), *_moe_prologue_args(g_ffn, w_router, ts))


def kernel(x, g_mix, g_ffn, g_final, a_w_grp, a_scale, b_w_out, c_w_qkv, c_w_out, d_w_pw1, d_w_dw, d_b_dw,
           d_ln_g, d_ln_b, d_w_pw2, ffn_w1, ffn_w3, ffn_w2, moe_router, moe_w1, moe_w3, moe_w2):
    assert g_mix.shape[0] == 4, "one layer of each mixer kind"
    tm = min(512, x.shape[1])
    ffn_w = [w.astype(BF16) for w in (ffn_w1, ffn_w3, ffn_w2)]
    moe_w = [w.astype(BF16) for w in (moe_w1, moe_w3, moe_w2)]
    x = _pool_ffn_layer(x, g_mix[0], a_w_grp[0], a_scale[0], g_ffn[0], *ffn_w, 0, tm)
    routed = _fourier_layer(x, g_mix[1], b_w_out[0], g_ffn[1], moe_router[0], tm)
    x = _moe_layer(routed, *moe_w, 0)
    x = _attention_ffn_layer(x, g_mix[2], c_w_qkv[0], c_w_out[0], g_ffn[2], *ffn_w, 1, tm)
    routed = _conv_layer(x, g_mix[3], d_w_pw1[0], d_w_dw[0], d_b_dw[0], d_ln_g[0], d_ln_b[0],
                         d_w_pw2[0], g_ffn[3], moe_router[1], tm)
    return _moe_layer(routed, *moe_w, 1, g_final=g_final)
```

```python
import functools
import math

import numpy as np
import jax
import jax.numpy as jnp
from jax import lax
from jax.experimental import pallas as pl
from jax.experimental.pallas import tpu as pltpu

F32 = jnp.float32
BF16 = jnp.bfloat16

RMS_EPS = 1e-6
LN_EPS = 1e-5
MASK_VALUE = -1e30
POOL_WINDOWS = (2, 4, 8, 16)
FOURIER_GROUPS = 4
ATTN_GROUPS = ((128, 1), (512, 4), (2048, 16))
ATTN_HEADS = 8
ATTN_HEAD_DIM = 64
ATTN_WIDTH = ATTN_HEADS * ATTN_HEAD_DIM
CONV_WIDTH = 31
N_EXPERTS = 8
TOP_K = 2

LANES = 128
SUBLANES = 8
HALO = 16
FFT_N2 = 128
MOE_ROWS = 512
MOE_CHUNK = 512
SEG_ALIGN = 16
SEG_PIECES = (512, 256, 128, 64, 32, 16)
LOCAL_ROWS = 1152
VMEM_LIMIT_BYTES = 56 * 1024 * 1024
EXPERT_VMEM_LIMIT_BYTES = 60 * 1024 * 1024


def _cparams(*sem):
    return pltpu.CompilerParams(dimension_semantics=sem, vmem_limit_bytes=VMEM_LIMIT_BYTES)


def _dot(a, b):
    return jnp.dot(a, b, preferred_element_type=F32)


def _rms(x, g):
    ms = jnp.mean(x * x, axis=-1, keepdims=True)
    return x * lax.rsqrt(ms + RMS_EPS) * g


def _swiglu(hb, w1_ref, w3_ref, w2_ref, g_ref, fc):
    f = w1_ref.shape[-1]
    for c in range(0, f, fc):
        a = _dot(hb, w1_ref[:, c:c + fc])
        b = _dot(hb, w3_ref[:, c:c + fc])
        g_ref[:, c:c + fc] = (a * jax.nn.sigmoid(a) * b).astype(BF16)
    return _dot(g_ref[...], w2_ref[...])


def _route(hb, wrt_ref, gate_ref):
    logits = lax.dot_general(wrt_ref[...], hb, (((1,), (1,)), ((), ())), preferred_element_type=F32)
    e_iota = lax.broadcasted_iota(jnp.int32, logits.shape, 0)
    m1 = jnp.max(logits, axis=0, keepdims=True)
    i1 = jnp.min(jnp.where(logits == m1, e_iota, N_EXPERTS), axis=0, keepdims=True)
    rest = jnp.where(e_iota == i1, -jnp.inf, logits)
    m2 = jnp.max(rest, axis=0, keepdims=True)
    i2 = jnp.min(jnp.where(rest == m2, e_iota, N_EXPERTS), axis=0, keepdims=True)
    e2 = jnp.exp(m2 - m1)
    den = 1.0 + e2
    gate_ref[0:1, :] = 1.0 / den
    gate_ref[1:2, :] = e2 / den
    return [i1, i2]


def _local_rows(idx_rows, before_ref, lrow_ref, len_ref):
    e_iota = lax.broadcasted_iota(jnp.int32, (N_EXPERTS, idx_rows[0].shape[1]), 0)
    picks = [jnp.where(row == e_iota, 1.0, 0.0) for row in idx_rows]
    total = picks[0]
    for m in picks[1:]:
        total = total + m
    earlier = _dot(total.astype(BF16), before_ref[...])
    cnt = jnp.sum(total, axis=1, keepdims=True).astype(jnp.int32)
    seg = (cnt + (SEG_ALIGN - 1)) & (-SEG_ALIGN)
    e_col = lax.broadcasted_iota(jnp.int32, (N_EXPERTS, 1), 0)
    loff = jnp.zeros((N_EXPERTS, 1), jnp.int32)
    for e in range(N_EXPERTS - 1):
        loff = loff + jnp.where(e_col > e, seg[e:e + 1, :], 0)
    base = loff.astype(F32) + earlier
    for k in range(TOP_K):
        lrow_ref[0, k:k + 1, :] = jnp.sum(picks[k] * base, axis=0, keepdims=True).astype(jnp.int32)
        base = base + picks[k]
    len_ref[0] = jnp.broadcast_to(seg, (N_EXPERTS, LANES))


def _moe_prologue(x1, gf_ref, wrt_ref, before_ref, x_out_ref, h_ref, gate_ref, lrow_ref, len_ref):
    x_out_ref[0] = x1
    hb = _rms(x1, gf_ref[...]).astype(BF16)
    h_ref[0] = hb
    _local_rows(_route(hb, wrt_ref, gate_ref), before_ref, lrow_ref, len_ref)


def _moe_prologue_specs(tm, nt, x_shape):
    assert tm == MOE_CHUNK, "the mixer tile is the MoE token chunk"
    bsz, seq, d = x_shape
    n_tok = bsz * seq
    chunk_of = lambda b, i: b * nt + i
    in_specs = [_full((1, d)), _full((N_EXPERTS, d)), _full((tm, tm))]
    out_specs = [pl.BlockSpec((1, tm, d), lambda b, i: (b, i, 0)),
                 pl.BlockSpec((1, tm, d), lambda b, i: (b, i, 0)),
                 pl.BlockSpec((TOP_K, tm), lambda b, i: (0, chunk_of(b, i))),
                 pl.BlockSpec((1, TOP_K, tm), lambda b, i: (chunk_of(b, i), 0, 0)),
                 pl.BlockSpec((1, N_EXPERTS, LANES), lambda b, i: (chunk_of(b, i), 0, 0))]
    out_shape = [jax.ShapeDtypeStruct(x_shape, F32), jax.ShapeDtypeStruct(x_shape, BF16),
                 jax.ShapeDtypeStruct((TOP_K, n_tok), F32),
                 jax.ShapeDtypeStruct((n_tok // tm, TOP_K, tm), jnp.int32),
                 jax.ShapeDtypeStruct((n_tok // tm, N_EXPERTS, LANES), jnp.int32)]
    return in_specs, out_specs, out_shape


def _moe_prologue_args(g_ffn, w_router, tm):
    t_iota = jnp.arange(tm, dtype=jnp.int32)
    before = (t_iota[:, None] < t_iota[None, :]).astype(BF16)
    return g_ffn.reshape(1, -1), w_router.T.astype(BF16), before


def _pool_ffn_kernel(xp_ref, xc_ref, xn_ref, gm_ref, wg_ref, sc_ref, gf_ref, w1_ref, w3_ref, w2_ref,
                     c1_ref, c3_ref, c2_ref, o_ref, b1_ref, b3_ref, b2_ref, hid_ref, *, ts, seq, fc):
    _copy_cast((c1_ref, c3_ref, c2_ref, b1_ref, b3_ref, b2_ref))
    i = pl.program_id(1)
    xc = xc_ref[0]
    d = xc.shape[1]
    gd = d // len(POOL_WINDOWS)
    n_ext = ts + 2 * HALO
    xe = jnp.concatenate([xp_ref[0], xc, xn_ref[0]], axis=0)
    he = _rms(xe, gm_ref[...])
    pos = i * ts - HALO + lax.broadcasted_iota(jnp.int32, (n_ext, 1), 0)
    he = jnp.where((pos >= 0) & (pos < seq), he, 0.0)

    s = he + pltpu.roll(he, 1, 0)
    sums = [s]
    half = 1
    for _ in POOL_WINDOWS[1:]:
        s = s[:, gd:]
        s = pltpu.roll(s, half, 0) + pltpu.roll(s, n_ext - half, 0)
        sums.append(s)
        half *= 2

    tpos = i * ts + lax.broadcasted_iota(jnp.int32, (ts, 1), 0)
    ys = []
    for gi, w in enumerate(POOL_WINDOWS):
        cnt = jnp.minimum(tpos + w // 2, seq) - jnp.maximum(tpos - w // 2, 0)
        mean = sums[gi][HALO:HALO + ts, :gd] / cnt.astype(F32)
        pooled = mean - he[HALO:HALO + ts, gi * gd:(gi + 1) * gd]
        ys.append(_dot(pooled.astype(BF16), wg_ref[gi]))
    x1 = xc + jnp.concatenate(ys, axis=1) * sc_ref[...]
    hb = _rms(x1, gf_ref[...]).astype(BF16)
    o_ref[0] = x1 + _swiglu(hb, w1_ref.at[0], w3_ref.at[0], w2_ref.at[0], hid_ref, fc)


def _halo_specs(ts, seq, d):
    r = ts // HALO
    last = seq // HALO - 1
    return [
        pl.BlockSpec((1, HALO, d), lambda b, i: (b, jnp.maximum(i * r - 1, 0), 0)),
        pl.BlockSpec((1, ts, d), lambda b, i: (b, i, 0)),
        pl.BlockSpec((1, HALO, d), lambda b, i: (b, jnp.minimum((i + 1) * r, last), 0)),
    ]


def _full(shape):
    n = len(shape)
    return pl.BlockSpec(shape, lambda *_: (0,) * n)


def _stacked(shape, layer):
    n = len(shape)
    return pl.BlockSpec((1,) + tuple(shape), lambda *_: (layer,) + (0,) * n, pipeline_mode=pl.Buffered(1))


def _side_cast(stacks, layer, bsz, nt):
    n_steps = bsz * nt
    views, ins, outs, shapes = [], [], [], []
    for w in stacks:
        layers, e, r, c = w.shape
        blk = e * r // n_steps
        assert blk * n_steps == e * r and blk % SEG_ALIGN == 0
        views.append(w.reshape(layers * e * r, c))
        ins.append(pl.BlockSpec((blk, c), lambda b, i: (layer * n_steps + b * nt + i, 0)))
        outs.append(pl.BlockSpec((blk, c), lambda b, i: (b * nt + i, 0)))
        shapes.append(jax.ShapeDtypeStruct((e * r, c), BF16))
    return views, ins, outs, shapes


def _side_cast_results(casted, stacks):
    return [c.reshape((1,) + w.shape[1:]) for c, w in zip(casted, stacks)]


def _copy_cast(refs):
    n = len(refs) // 2
    for src, dst in zip(refs[:n], refs[n:]):
        dst[...] = src[...].astype(BF16)


def _pool_ffn_layer(x, g_mix, w_grp, scale, g_ffn, w1, w3, w2, layer, ts, cast_stacks, cast_layer):
    bsz, seq, d = x.shape
    f = w1.shape[2]
    nt = seq // ts
    cviews, cin, cout, cshape = _side_cast(cast_stacks, cast_layer, bsz, nt)
    kern = functools.partial(_pool_ffn_kernel, ts=ts, seq=seq, fc=256)
    out = pl.pallas_call(
        kern,
        grid=(bsz, nt),
        in_specs=_halo_specs(ts, seq, d) + [
            _full((1, d)), _full(w_grp.shape), _full((1, d)), _full((1, d)),
            _stacked((d, f), layer), _stacked((d, f), layer), _stacked((f, d), layer)] + cin,
        out_specs=[pl.BlockSpec((1, ts, d), lambda b, i: (b, i, 0))] + cout,
        out_shape=[jax.ShapeDtypeStruct(x.shape, F32)] + cshape,
        scratch_shapes=[pltpu.VMEM((ts, f), BF16)],
        compiler_params=_cparams("parallel", "parallel"),
        name="pool_ffn",
    )(x, x, x, g_mix.reshape(1, d), w_grp.astype(BF16), scale.reshape(1, d), g_ffn.reshape(1, d), w1, w3, w2,
      *cviews)
    return out[0], _side_cast_results(out[1:], cast_stacks)


def _fft_stage1_kernel(x_ref, g_ref, f1_ref, o_ref, s_ref, *, nseg):
    n1, d = x_ref.shape[1], x_ref.shape[3]
    nlc = d // LANES
    for c in range(nlc):
        s_ref[c] = x_ref[0, :, :, c * LANES:(c + 1) * LANES].reshape(n1 * nseg, LANES)
    for j in range(nseg):
        xj = jnp.concatenate([s_ref[c, pl.ds(j, n1, stride=nseg), :] for c in range(nlc)], axis=1)
        h = _rms(xj, g_ref[...]).astype(BF16)
        o_ref[0, j] = _dot(f1_ref[...], h).astype(BF16)


def _fft_stage2_kernel(br_ref, bi_ref, m_ref, o_ref, s_ref, *, nk):
    n2, d = FFT_N2, br_ref.shape[3]
    nlc = d // LANES
    for p, ref in enumerate((br_ref, bi_ref)):
        for c in range(nlc):
            s_ref[p, c] = ref[0, :, :, c * LANES:(c + 1) * LANES].reshape(n2 * nk, LANES).astype(F32)
    for j in range(nk):
        rhs = jnp.concatenate(
            [jnp.concatenate([s_ref[p, c, pl.ds(j, n2, stride=nk), :] for c in range(nlc)], axis=1)
             for p in range(2)], axis=0).astype(BF16)
        res = _dot(m_ref[j], rhs)
        for p in range(2):
            for c in range(nlc):
                s_ref[p, c, pl.ds(j, n2, stride=nk), :] = res[p * n2:(p + 1) * n2, c * LANES:(c + 1) * LANES]
    for p in range(2):
        for c in range(nlc):
            o_ref[0, p, :, :, c * LANES:(c + 1) * LANES] = s_ref[p, c].astype(BF16).reshape(n2, nk, LANES)


def _fourier_out_kernel(x_ref, ar_ref, ai_ref, cc_ref, sc_ref, wo_ref, gf_ref, wrt_ref, before_ref,
                        xo_ref, h_ref, gate_ref, lrow_ref, len_ref, *, norm):
    d = x_ref.shape[2]
    gd = d // FOURIER_GROUPS
    fs = []
    for gi in range(FOURIER_GROUPS):
        sl = slice(gi * gd, (gi + 1) * gd)
        fs.append(_dot(ar_ref[0, 0, :, sl].astype(BF16), cc_ref[...])
                  + _dot(ai_ref[0, 0, :, sl].astype(BF16), sc_ref[...]))
    f = (jnp.concatenate(fs, axis=1) * norm).astype(BF16)
    x1 = x_ref[0] + _dot(f, wo_ref[...])
    _moe_prologue(x1, gf_ref, wrt_ref, before_ref, xo_ref, h_ref, gate_ref, lrow_ref, len_ref)


def _dft_tables(seq, gd):
    n2 = FFT_N2
    n1 = seq // n2
    k = np.arange(n1)
    ang1 = 2.0 * np.pi * ((k[:, None] * k[None, :]) % n1) / n1
    f1 = np.concatenate([np.cos(ang1), -np.sin(ang1)], axis=0)
    c = np.arange(gd)
    angc = 2.0 * np.pi * ((c[:, None] * c[None, :]) % gd) / gd
    return (jnp.asarray(f1, BF16), jnp.asarray(np.cos(angc), BF16), jnp.asarray(np.sin(angc), BF16))


def _twiddled_dft(seq):
    n2 = FFT_N2
    n1 = seq // n2
    k1 = lax.broadcasted_iota(jnp.int32, (n1, n2, n2), 0)
    k2 = lax.broadcasted_iota(jnp.int32, (n1, n2, n2), 1)
    a = lax.broadcasted_iota(jnp.int32, (n1, n2, n2), 2)
    ang = ((a * (n1 * k2 + k1)) % seq).astype(F32) * (2.0 * math.pi / seq)
    mr = jnp.cos(ang)
    mi = -jnp.sin(ang)
    top = jnp.concatenate([mr, -mi], axis=2)
    bot = jnp.concatenate([mi, mr], axis=2)
    return jnp.concatenate([top, bot], axis=1).astype(BF16)


def _fourier_layer(x, g_mix, w_out, g_ffn, w_router, tm):
    bsz, seq, d = x.shape
    n2 = FFT_N2
    n1 = seq // n2
    gd = d // FOURIER_GROUPS
    f1, cc, sc = _dft_tables(seq, gd)
    mtab = _twiddled_dft(seq)
    nseg = 8
    b1 = pl.pallas_call(
        functools.partial(_fft_stage1_kernel, nseg=nseg),
        grid=(bsz, n2 // nseg),
        in_specs=[pl.BlockSpec((1, n1, nseg, d), lambda b, t: (b, 0, t, 0)), _full((1, d)), _full(f1.shape)],
        out_specs=pl.BlockSpec((1, nseg, 2 * n1, d), lambda b, t: (b, t, 0, 0)),
        out_shape=jax.ShapeDtypeStruct((bsz, n2, 2 * n1, d), BF16),
        scratch_shapes=[pltpu.VMEM((d // LANES, n1 * nseg, LANES), F32)],
        compiler_params=_cparams("parallel", "parallel"),
        name="fft_stage1",
    )(x.reshape(bsz, n1, n2, d), g_mix.reshape(1, d), f1)

    nk = 16
    a = pl.pallas_call(
        functools.partial(_fft_stage2_kernel, nk=nk),
        grid=(bsz, n1 // nk),
        in_specs=[pl.BlockSpec((1, n2, nk, d), lambda b, t: (b, 0, t, 0)),
                  pl.BlockSpec((1, n2, nk, d), lambda b, t: (b, 0, n1 // nk + t, 0)),
                  pl.BlockSpec((nk, 2 * n2, 2 * n2), lambda b, t: (t, 0, 0))],
        out_specs=pl.BlockSpec((1, 2, n2, nk, d), lambda b, t: (b, 0, 0, t, 0)),
        out_shape=jax.ShapeDtypeStruct((bsz, 2, n2, n1, d), BF16),
        scratch_shapes=[pltpu.VMEM((2, d // LANES, n2 * nk, LANES), F32)],
        compiler_params=_cparams("parallel", "parallel"),
        name="fft_stage2",
    )(b1, b1, mtab)

    av = a.reshape(bsz, 2, seq, d)
    nt = seq // tm
    moe_in, moe_out, moe_shape = _moe_prologue_specs(tm, nt, x.shape)
    return pl.pallas_call(
        functools.partial(_fourier_out_kernel, norm=1.0 / math.sqrt(seq * gd)),
        grid=(bsz, nt),
        in_specs=[pl.BlockSpec((1, tm, d), lambda b, i: (b, i, 0)),
                  pl.BlockSpec((1, 1, tm, d), lambda b, i: (b, 0, i, 0)),
                  pl.BlockSpec((1, 1, tm, d), lambda b, i: (b, 1, i, 0)),
                  _full((gd, gd)), _full((gd, gd)), _full((d, d))] + moe_in,
        out_specs=moe_out,
        out_shape=moe_shape,
        compiler_params=_cparams("parallel", "parallel"),
        name="fourier_out",
    )(x, av, av, cc, sc, w_out.astype(BF16), *_moe_prologue_args(g_ffn, w_router, tm))


def _moe_plan(lrow_lanes, seg_len, n_tok):
    chunk, rows = MOE_CHUNK, MOE_ROWS
    nc = n_tok // chunk
    seg_len = seg_len[:, :, 0]
    loff = jnp.cumsum(seg_len, axis=1) - seg_len
    region = (jnp.sum(seg_len, axis=0) + rows - 1) // rows * rows
    pend = jnp.cumsum(region)
    goff = (pend - region)[None, :] + jnp.cumsum(seg_len, axis=0) - seg_len
    n_blocks = (n_tok * TOP_K + nc * N_EXPERTS * (SEG_ALIGN - 1)) // rows + N_EXPERTS
    blk = jnp.arange(n_blocks, dtype=jnp.int32)
    block_exp = jnp.minimum(jnp.sum((blk[:, None] * rows >= pend[None, :]).astype(jnp.int32), axis=1),
                            N_EXPERTS - 1)
    return dict(
        n_blocks=n_blocks,
        seg_len=seg_len.reshape(-1), loff=loff.reshape(-1).astype(jnp.int32), goff=goff.reshape(-1).astype(jnp.int32),
        gap_off=(pend - region + jnp.sum(seg_len, axis=0)).astype(jnp.int32),
        gap_len=(region - jnp.sum(seg_len, axis=0)).astype(jnp.int32),
        lrow_lanes=lrow_lanes,
        lrow_cols=lrow_lanes.transpose(0, 2, 1).reshape(n_tok, TOP_K),
        n_used=(pend[-1] // rows).reshape(1).astype(jnp.int32), block_exp=block_exp)


def _segment_copies(c, len_ref, loff_ref, goff_ref, make, act):
    for e in range(N_EXPERTS):
        ln = len_ref[c * N_EXPERTS + e]
        lo = loff_ref[c * N_EXPERTS + e]
        go = goff_ref[c * N_EXPERTS + e]
        for size in SEG_PIECES:
            done = ln & (-2 * size)

            @pl.when((ln & size) != 0)
            def _():
                act(make(pl.multiple_of(lo + done, SEG_ALIGN), pl.multiple_of(go + done, SEG_ALIGN), size))


def _zero_fill_copies(gap_off_ref, gap_len_ref, nu_ref, zero_ref, xb_ref, sem, act):
    rows = zero_ref.shape[0]
    for e in range(N_EXPERTS):
        ln = gap_len_ref[e]
        off = gap_off_ref[e]
        for size in SEG_PIECES[1:]:
            done = ln & (-2 * size)

            @pl.when((ln & size) != 0)
            def _():
                act(pltpu.make_async_copy(zero_ref.at[pl.ds(0, size)],
                                          xb_ref.at[pl.ds(pl.multiple_of(off + done, SEG_ALIGN), size)], sem))

    def block(b, carry):
        act(pltpu.make_async_copy(zero_ref, xb_ref.at[pl.ds(pl.multiple_of(b * rows, rows), rows)], sem))
        return carry

    lax.fori_loop(nu_ref[0], xb_ref.shape[0] // rows, block, 0)


def _dispatch_kernel(len_ref, loff_ref, goff_ref, gap_off_ref, gap_len_ref, nu_ref, h_ref, lrow_ref, xb_ref,
                     stage_ref, zero_ref, sem, zsem):
    c = pl.program_id(0)
    slot = c % 2
    n_local, chunk = stage_ref.shape[1], h_ref.shape[0]

    @pl.when(c == 0)
    def _():
        zero_ref[...] = jnp.zeros_like(zero_ref)
        _zero_fill_copies(gap_off_ref, gap_len_ref, nu_ref, zero_ref, xb_ref, zsem, lambda cp: cp.start())

    def make(slot_):
        def _make(lr, gr, size):
            return pltpu.make_async_copy(stage_ref.at[slot_, pl.ds(lr, size)], xb_ref.at[pl.ds(gr, size)],
                                         sem.at[slot_])
        return _make

    r_iota = lax.broadcasted_iota(jnp.int32, (n_local, chunk), 0)
    sel = (lrow_ref[0, 0:1, :] == r_iota) | (lrow_ref[0, 1:2, :] == r_iota)
    stage_ref[slot] = _dot(jnp.where(sel, 1.0, 0.0).astype(BF16), h_ref[...]).astype(BF16)

    @pl.when(c > 0)
    def _():
        _segment_copies(c - 1, len_ref, loff_ref, goff_ref, make(1 - slot), lambda cp: cp.wait())

    _segment_copies(c, len_ref, loff_ref, goff_ref, make(slot), lambda cp: cp.start())

    @pl.when(c == pl.num_programs(0) - 1)
    def _():
        _segment_copies(c, len_ref, loff_ref, goff_ref, make(slot), lambda cp: cp.wait())
        _zero_fill_copies(gap_off_ref, gap_len_ref, nu_ref, zero_ref, xb_ref, zsem, lambda cp: cp.wait())


def _expert_kernel(be_ref, nu_ref, xb_ref, w1_ref, w3_ref, w2_ref, y_ref, hid_ref, *, fc):
    b = pl.program_id(0)

    @pl.when(b < nu_ref[0])
    def _():
        y_ref[...] = _swiglu(xb_ref[...], w1_ref.at[0, 0], w3_ref.at[0, 0], w2_ref.at[0, 0], hid_ref,
                             fc).astype(BF16)

    @pl.when(b >= nu_ref[0])
    def _():
        y_ref[...] = jnp.zeros_like(y_ref)


def _combine_kernel(len_ref, loff_ref, goff_ref, y_ref, lcol_ref, gcol_ref, x_ref, g_ref, o_ref, ybuf_ref, sem,
                    *, final_norm):
    c = pl.program_id(0)
    nc = pl.num_programs(0)
    slot = c % 2
    chunk, n_local = x_ref.shape[0], ybuf_ref.shape[1]

    def make(slot_):
        def _make(lr, gr, size):
            return pltpu.make_async_copy(y_ref.at[pl.ds(gr, size)], ybuf_ref.at[slot_, pl.ds(lr, size)],
                                         sem.at[slot_])
        return _make

    @pl.when(c == 0)
    def _():
        ybuf_ref[...] = jnp.zeros_like(ybuf_ref)
        _segment_copies(c, len_ref, loff_ref, goff_ref, make(slot), lambda cp: cp.start())

    @pl.when(c + 1 < nc)
    def _():
        _segment_copies(c + 1, len_ref, loff_ref, goff_ref, make(1 - slot), lambda cp: cp.start())

    _segment_copies(c, len_ref, loff_ref, goff_ref, make(slot), lambda cp: cp.wait())

    yl = ybuf_ref[slot]
    r_iota = lax.broadcasted_iota(jnp.int32, (chunk, n_local), 1)
    q = None
    for k in range(TOP_K):
        qk = jnp.where(lcol_ref[:, k:k + 1] == r_iota, gcol_ref[:, k:k + 1], 0.0)
        q = qk if q is None else q + qk
    xo = x_ref[...] + _dot(q.astype(BF16), yl)
    o_ref[...] = _rms(xo, g_ref[...]) if final_norm else xo


def _moe_layer(routed, w1, w3, w2, layer, g_final=None):
    x, h, gates, lrow_lanes, seg_len = routed
    bsz, seq, d = x.shape
    n_tok = bsz * seq
    f = w1.shape[3]
    rows, chunk = MOE_ROWS, MOE_CHUNK
    nc = n_tok // chunk
    plan = _moe_plan(lrow_lanes, seg_len, n_tok)
    n_blocks = plan['n_blocks']
    n_rows = n_blocks * rows
    seg = (plan['seg_len'], plan['loff'], plan['goff'])

    xb = pl.pallas_call(
        _dispatch_kernel,
        grid_spec=pltpu.PrefetchScalarGridSpec(
            num_scalar_prefetch=6, grid=(nc,),
            in_specs=[pl.BlockSpec((chunk, d), lambda c, *_: (c, 0)),
                      pl.BlockSpec((1, TOP_K, chunk), lambda c, *_: (c, 0, 0))],
            out_specs=pl.BlockSpec(memory_space=pl.ANY),
            scratch_shapes=[pltpu.VMEM((2, LOCAL_ROWS, d), BF16), pltpu.VMEM((rows, d), BF16),
                            pltpu.SemaphoreType.DMA((2,)), pltpu.SemaphoreType.DMA(())]),
        out_shape=jax.ShapeDtypeStruct((n_rows, d), BF16),
        compiler_params=_cparams("arbitrary"),
        name="moe_dispatch",
    )(*seg, plan['gap_off'], plan['gap_len'], plan['n_used'], h.reshape(n_tok, d), plan['lrow_lanes'])

    def blk_idx(b, nu):
        return jnp.minimum(b, nu[0] - 1)

    y = pl.pallas_call(
        functools.partial(_expert_kernel, fc=256),
        grid_spec=pltpu.PrefetchScalarGridSpec(
            num_scalar_prefetch=2, grid=(n_blocks,),
            in_specs=[pl.BlockSpec((rows, d), lambda b, be, nu: (blk_idx(b, nu), 0)),
                      pl.BlockSpec((1, 1, d, f), lambda b, be, nu: (layer, be[blk_idx(b, nu)], 0, 0)),
                      pl.BlockSpec((1, 1, d, f), lambda b, be, nu: (layer, be[blk_idx(b, nu)], 0, 0)),
                      pl.BlockSpec((1, 1, f, d), lambda b, be, nu: (layer, be[blk_idx(b, nu)], 0, 0))],
            out_specs=pl.BlockSpec((rows, d), lambda b, be, nu: (b, 0)),
            scratch_shapes=[pltpu.VMEM((rows, f), BF16)]),
        out_shape=jax.ShapeDtypeStruct((n_rows, d), BF16),
        compiler_params=pltpu.CompilerParams(dimension_semantics=("arbitrary",),
                                             vmem_limit_bytes=EXPERT_VMEM_LIMIT_BYTES),
        name="moe_experts",
    )(plan['block_exp'], plan['n_used'], xb, w1, w3, w2)

    final_norm = g_final is not None
    g = (g_final if final_norm else jnp.ones((d,), F32)).reshape(1, d)
    out = pl.pallas_call(
        functools.partial(_combine_kernel, final_norm=final_norm),
        grid_spec=pltpu.PrefetchScalarGridSpec(
            num_scalar_prefetch=3, grid=(nc,),
            in_specs=[pl.BlockSpec(memory_space=pl.ANY),
                      pl.BlockSpec((chunk, TOP_K), lambda c, *_: (c, 0)),
                      pl.BlockSpec((chunk, TOP_K), lambda c, *_: (c, 0)),
                      pl.BlockSpec((chunk, d), lambda c, *_: (c, 0)),
                      pl.BlockSpec((1, d), lambda c, *_: (0, 0))],
            out_specs=pl.BlockSpec((chunk, d), lambda c, *_: (c, 0)),
            scratch_shapes=[pltpu.VMEM((2, LOCAL_ROWS, d), BF16), pltpu.SemaphoreType.DMA((2,))]),
        out_shape=jax.ShapeDtypeStruct((n_tok, d), F32),
        compiler_params=_cparams("arbitrary"),
        name="moe_combine",
    )(*seg, y, plan['lrow_cols'], gates.T, x.reshape(n_tok, d), g)
    return out.reshape(bsz, seq, d)


def _qkv_kernel(x_ref, g_ref, w_ref, c_ref, o0_ref, o1_ref, o2_ref, b_ref, h_ref, *, nc):
    _copy_cast((c_ref, b_ref))
    o_refs = (o0_ref, o1_ref, o2_ref)
    tm = x_ref.shape[1]
    gw = 3 * ATTN_WIDTH
    h = _rms(x_ref[0], g_ref[...])
    nlc = h.shape[1] // LANES
    for c in range(nlc):
        h_ref[c] = h[:, c * LANES:(c + 1) * LANES]
    for gi, (_, dil) in enumerate(ATTN_GROUPS):
        n = tm // dil
        hb = h if dil == 1 else jnp.concatenate(
            [jnp.concatenate([h_ref[c, pl.ds(r, n, stride=dil), :] for c in range(nlc)], axis=1)
             for r in range(dil)], axis=0)
        hb = hb.astype(BF16)
        for c in range(gi * gw, (gi + 1) * gw, nc):
            res = _dot(hb, w_ref[:, c:c + nc]).astype(BF16)
            for r in range(dil):
                o_refs[gi][0, :, r * gw + c - gi * gw:r * gw + c - gi * gw + nc] = res[r * n:(r + 1) * n]


def _attn_kernel(q_ref, kp_ref, kc_ref, kn_ref, vp_ref, vc_ref, vn_ref, o_ref, lse_ref,
                 *, tq, seg, dil, radius, slopes):
    j = pl.program_id(2)
    hd = ATTN_HEAD_DIM
    tb = q_ref.shape[1]
    tk = tq + 2 * radius
    q_all = q_ref[0] * jnp.asarray(1.0 / math.sqrt(hd), BF16)
    k_all = jnp.concatenate([kp_ref[0], kc_ref[0], kn_ref[0]], axis=0)
    v_all = jnp.concatenate([vp_ref[0], vc_ref[0], vn_ref[0]], axis=0)
    qi = lax.broadcasted_iota(jnp.int32, (tq, tk), 0)
    kj = lax.broadcasted_iota(jnp.int32, (tq, tk), 1)
    arel = jnp.abs(kj - radius - qi)
    band = arel <= radius
    dist = (dil * arel).astype(F32)
    alibi = [-(slope * dist) for slope in slopes]
    low = lax.broadcasted_iota(jnp.int32, (1, 2 * hd), 1) < hd
    for t in range(tb // tq):
        rows = slice(t * tq, (t + 1) * tq)
        kpos = j * tb + t * tq - radius + kj
        valid = band & (kpos >= 0) & (kpos < seg)
        q, k, v = q_all[rows], k_all[t * tq:t * tq + tk], v_all[t * tq:t * tq + tk]
        for hp in range(ATTN_HEADS // 2):
            sl = slice(hp * 2 * hd, (hp + 1) * 2 * hd)
            qp, kp, vp = q[:, sl], k[:, sl], v[:, sl]
            o_pair, l_pair = None, None
            for sub in range(2):
                mine = low if sub == 0 else jnp.logical_not(low)
                qm = jnp.where(mine, qp, jnp.zeros_like(qp))
                s = lax.dot_general(qm, kp, (((1,), (1,)), ((), ())), preferred_element_type=F32)
                s = jnp.where(valid, s + alibi[2 * hp + sub], MASK_VALUE)
                m = jnp.max(s, axis=-1, keepdims=True)
                p = jnp.exp(s - m)
                den = jnp.sum(p, axis=-1, keepdims=True)
                o = _dot(p.astype(BF16), vp) / den
                l = jnp.broadcast_to(m + jnp.log(den), o.shape)
                o_pair = o if sub == 0 else jnp.where(low, o_pair, o)
                l_pair = l if sub == 0 else jnp.where(low, l_pair, l)
            o_ref[0, rows, sl] = o_pair.astype(BF16)
            lse_ref[0, rows, sl] = l_pair


def _attn_out_ffn_kernel(x_ref, o0_ref, o1_ref, o2_ref, l0_ref, l1_ref, l2_ref, wo_ref, gf_ref,
                         w1_ref, w3_ref, w2_ref, c1_ref, c3_ref, out_ref, b1_ref, b3_ref,
                         so_ref, sl_ref, hid_ref, *, fc):
    _copy_cast((c1_ref, c3_ref, b1_ref, b3_ref))
    aw = ATTN_WIDTH
    ls, os_ = [], []
    for gi, (o_ref, l_ref) in enumerate(((o0_ref, l0_ref), (o1_ref, l1_ref), (o2_ref, l2_ref))):
        dil = ATTN_GROUPS[gi][1]
        if dil == 1:
            os_.append(o_ref[0].astype(F32))
            ls.append(l_ref[0])
            continue
        n = o_ref.shape[1]
        nlc = aw // LANES
        for r in range(dil):
            ov = o_ref[0, :, r * aw:(r + 1) * aw].astype(F32)
            lv = l_ref[0, :, r * aw:(r + 1) * aw]
            for c in range(nlc):
                so_ref[gi - 1, c, pl.ds(r, n, stride=dil), :] = ov[:, c * LANES:(c + 1) * LANES]
                sl_ref[gi - 1, c, pl.ds(r, n, stride=dil), :] = lv[:, c * LANES:(c + 1) * LANES]
        os_.append(jnp.concatenate([so_ref[gi - 1, c] for c in range(nlc)], axis=1))
        ls.append(jnp.concatenate([sl_ref[gi - 1, c] for c in range(nlc)], axis=1))
    m = jnp.maximum(jnp.maximum(ls[0], ls[1]), ls[2])
    es = [jnp.exp(l - m) for l in ls]
    den = es[0] + es[1] + es[2]
    o = (es[0] * os_[0] + es[1] * os_[1] + es[2] * os_[2]) / den
    x1 = x_ref[0] + _dot(o.astype(BF16), wo_ref[...])
    hb = _rms(x1, gf_ref[...]).astype(BF16)
    out_ref[0] = x1 + _swiglu(hb, w1_ref.at[0], w3_ref.at[0], w2_ref.at[0], hid_ref, fc)


def _alibi_slopes():
    n = len(ATTN_GROUPS) * ATTN_HEADS
    s = np.float32(2.0) ** (np.float32(-8.0) * np.arange(1, n + 1, dtype=np.float32) / np.float32(n))
    return [float(v) for v in s]


def _attention_ffn_layer(x, g_mix, w_qkv, w_out, g_ffn, w1, w3, w2, layer, tm, cast_stacks, cast_layer):
    bsz, seq, d = x.shape
    nqkv = w_qkv.shape[1]
    aw = ATTN_WIDTH
    gw = 3 * aw
    assert nqkv == len(ATTN_GROUPS) * gw
    qviews, qin, qout, qshape = _side_cast(cast_stacks[2:], cast_layer, bsz, seq // tm)
    qkvs = pl.pallas_call(
        functools.partial(_qkv_kernel, nc=aw),
        grid=(bsz, seq // tm),
        in_specs=[pl.BlockSpec((1, tm, d), lambda b, i: (b, i, 0)), _full((1, d)), _full((d, nqkv))] + qin,
        out_specs=[pl.BlockSpec((1, tm // dil, dil * gw), lambda b, i: (b, i, 0)) for _, dil in ATTN_GROUPS] + qout,
        out_shape=[jax.ShapeDtypeStruct((bsz, seq // dil, dil * gw), BF16) for _, dil in ATTN_GROUPS] + qshape,
        scratch_shapes=[pltpu.VMEM((d // LANES, tm, LANES), F32)],
        compiler_params=_cparams("parallel", "parallel"),
        name="qkv_proj",
    )(x, g_mix.reshape(1, d), w_qkv.astype(BF16), *qviews)
    cast_w2 = _side_cast_results(qkvs[3:], cast_stacks[2:])
    qkvs = qkvs[:3]

    slopes = _alibi_slopes()
    outs, lses = [], []
    for gi, (window, dil) in enumerate(ATTN_GROUPS):
        radius = window // (2 * dil)
        seg = seq // dil
        tb = min(512, seg)
        tq = min(128, seg)
        hr = tb // radius
        last = seg // radius - 1
        view = qkvs[gi]

        def col(part, r):
            return r * 3 + part

        def cur(part):
            return pl.BlockSpec((1, tb, aw), lambda b, r, j, part=part: (b, j, col(part, r)))

        def prev(part):
            return pl.BlockSpec((1, radius, aw),
                                lambda b, r, j, part=part: (b, jnp.maximum(j * hr - 1, 0), col(part, r)))

        def nxt(part):
            return pl.BlockSpec((1, radius, aw),
                                lambda b, r, j, part=part: (b, jnp.minimum((j + 1) * hr, last), col(part, r)))

        o, lse = pl.pallas_call(
            functools.partial(_attn_kernel, tq=tq, seg=seg, dil=dil, radius=radius,
                              slopes=slopes[gi * ATTN_HEADS:(gi + 1) * ATTN_HEADS]),
            grid=(bsz, dil, seg // tb),
            in_specs=[cur(0), prev(1), cur(1), nxt(1), prev(2), cur(2), nxt(2)],
            out_specs=[pl.BlockSpec((1, tb, aw), lambda b, r, j: (b, j, r)),
                       pl.BlockSpec((1, tb, aw), lambda b, r, j: (b, j, r))],
            out_shape=[jax.ShapeDtypeStruct((bsz, seg, dil * aw), BF16),
                       jax.ShapeDtypeStruct((bsz, seg, dil * aw), F32)],
            compiler_params=_cparams("parallel", "parallel", "parallel"),
            name=f"dilated_attn_{dil}",
        )(view, view, view, view, view, view, view)
        outs.append(o)
        lses.append(lse)

    f = w1.shape[2]
    row = lambda width: pl.BlockSpec((1, tm, width), lambda b, i: (b, i, 0))
    dilated = [pl.BlockSpec((1, tm // dil, dil * aw), lambda b, i: (b, i, 0)) for _, dil in ATTN_GROUPS]
    cviews, cin, cout, cshape = _side_cast(cast_stacks[:2], cast_layer, bsz, seq // tm)
    out = pl.pallas_call(
        functools.partial(_attn_out_ffn_kernel, fc=256),
        grid=(bsz, seq // tm),
        in_specs=[row(d)] + dilated + dilated + [
            _full((aw, d)), _full((1, d)),
            _stacked((d, f), layer), _stacked((d, f), layer), _stacked((f, d), layer)] + cin,
        out_specs=[row(d)] + cout,
        out_shape=[jax.ShapeDtypeStruct(x.shape, F32)] + cshape,
        scratch_shapes=[pltpu.VMEM((len(ATTN_GROUPS) - 1, aw // LANES, tm, LANES), F32),
                        pltpu.VMEM((len(ATTN_GROUPS) - 1, aw // LANES, tm, LANES), F32),
                        pltpu.VMEM((tm, f), BF16)],
        compiler_params=_cparams("parallel", "parallel"),
        name="attn_out_ffn",
    )(x, *outs, *lses, w_out.astype(BF16), g_ffn.reshape(1, d), w1, w3, w2, *cviews)
    return out[0], _side_cast_results(out[1:], cast_stacks[:2]) + cast_w2


def _conv_glu_kernel(x_ref, g_ref, w_ref, u_ref):
    d = x_ref.shape[2]
    hb = _rms(x_ref[0], g_ref[...]).astype(BF16)
    a = _dot(hb, w_ref[:, :d])
    b = _dot(hb, w_ref[:, d:])
    u_ref[0] = a * jax.nn.sigmoid(b)


def _conv_out_kernel(x_ref, up_ref, uc_ref, un_ref, wdw_ref, bdw_ref, lg_ref, lb_ref, w2_ref,
                     gf_ref, wrt_ref, before_ref, xo_ref, h_ref, gate_ref, lrow_ref, len_ref, ext_ref, z_ref,
                     *, ts, seq, rows, lanes):
    i = pl.program_id(1)
    n_ext = ts + 2 * HALO
    ue = jnp.concatenate([up_ref[0], uc_ref[0], un_ref[0]], axis=0)
    pos = i * ts - HALO + lax.broadcasted_iota(jnp.int32, (n_ext, 1), 0)
    ext_ref[0] = jnp.where((pos >= 0) & (pos < seq), ue, 0.0)
    for r in range(1, SUBLANES):
        ext_ref[r, 0:n_ext - SUBLANES, :] = ext_ref[0, r:r + n_ext - SUBLANES, :]
    half = CONV_WIDTH // 2

    def row_chunk(c, carry):
        base = pl.multiple_of(c * rows, rows)
        for lo in range(0, z_ref.shape[1], lanes):
            acc = None
            for k in range(CONV_WIDTH):
                start = HALO + k - half
                r = start % SUBLANES
                first = pl.multiple_of(base + (start - r), SUBLANES)
                term = (ext_ref[r, pl.ds(first, rows), lo:lo + lanes].reshape(rows // SUBLANES, SUBLANES, lanes)
                        * wdw_ref[k, :, lo:lo + lanes])
                acc = term if acc is None else acc + term
            z_ref[pl.ds(base, rows), lo:lo + lanes] = acc.reshape(rows, lanes)
        return carry

    lax.fori_loop(0, ts // rows, row_chunk, 0)
    u = z_ref[...] + bdw_ref[...]
    mu = jnp.mean(u, axis=-1, keepdims=True)
    uc = u - mu
    var = jnp.mean(uc * uc, axis=-1, keepdims=True)
    z = uc * lax.rsqrt(var + LN_EPS) * lg_ref[...] + lb_ref[...]
    z = (z * jax.nn.sigmoid(z)).astype(BF16)
    x1 = x_ref[0] + _dot(z, w2_ref[...])
    _moe_prologue(x1, gf_ref, wrt_ref, before_ref, xo_ref, h_ref, gate_ref, lrow_ref, len_ref)


def _conv_layer(x, g_mix, w_pw1, w_dw, b_dw, ln_g, ln_b, w_pw2, g_ffn, w_router, ts):
    bsz, seq, d = x.shape
    nt = seq // ts
    vec = lambda a: a.reshape(1, d)
    u = pl.pallas_call(
        _conv_glu_kernel,
        grid=(bsz, nt),
        in_specs=[pl.BlockSpec((1, ts, d), lambda b, i: (b, i, 0)), _full((1, d)), _full((d, 2 * d))],
        out_specs=pl.BlockSpec((1, ts, d), lambda b, i: (b, i, 0)),
        out_shape=jax.ShapeDtypeStruct(x.shape, F32),
        compiler_params=_cparams("parallel", "parallel"),
        name="conv_glu",
    )(x, vec(g_mix), w_pw1.astype(BF16))

    moe_in, moe_out, moe_shape = _moe_prologue_specs(ts, nt, x.shape)
    return pl.pallas_call(
        functools.partial(_conv_out_kernel, ts=ts, seq=seq, rows=min(32, ts), lanes=min(2 * LANES, d)),
        grid=(bsz, nt),
        in_specs=[pl.BlockSpec((1, ts, d), lambda b, i: (b, i, 0))] + _halo_specs(ts, seq, d) + [
            _full((CONV_WIDTH, SUBLANES, d)), _full((1, d)), _full((1, d)), _full((1, d)), _full((d, d))] + moe_in,
        out_specs=moe_out,
        out_shape=moe_shape,
        scratch_shapes=[pltpu.VMEM((SUBLANES, ts + 2 * HALO, d), F32), pltpu.VMEM((ts, d), F32)],
        compiler_params=_cparams("parallel", "parallel"),
        name="conv_out",
    )(x, u, u, u, jnp.broadcast_to(w_dw[:, None, :], (CONV_WIDTH, SUBLANES, d)), vec(b_dw), vec(ln_g), vec(ln_b),
      w_pw2.astype(BF16), *_moe_prologue_args(g_ffn, w_router, ts))


def kernel(x, g_mix, g_ffn, g_final, a_w_grp, a_scale, b_w_out, c_w_qkv, c_w_out, d_w_pw1, d_w_dw, d_b_dw,
           d_ln_g, d_ln_b, d_w_pw2, ffn_w1, ffn_w3, ffn_w2, moe_router, moe_w1, moe_w3, moe_w2):
    assert g_mix.shape[0] == 4, "one layer of each mixer kind"
    tm = min(512, x.shape[1])
    ffn_w = [w.astype(BF16) for w in (ffn_w1, ffn_w3, ffn_w2)]
    moe_f32 = (moe_w1, moe_w3, moe_w2)
    x, moe_w = _pool_ffn_layer(x, g_mix[0], a_w_grp[0], a_scale[0], g_ffn[0], *ffn_w, 0, tm, moe_f32, 0)
    routed = _fourier_layer(x, g_mix[1], b_w_out[0], g_ffn[1], moe_router[0], tm)
    x = _moe_layer(routed, *moe_w, 0)
    x, moe_w = _attention_ffn_layer(x, g_mix[2], c_w_qkv[0], c_w_out[0], g_ffn[2], *ffn_w, 1, tm, moe_f32, 1)
    routed = _conv_layer(x, g_mix[3], d_w_pw1[0], d_w_dw[0], d_b_dw[0], d_ln_g[0], d_ln_b[0],
                         d_w_pw2[0], g_ffn[3], moe_router[1], tm)
    return _moe_layer(routed, *moe_w, 0, g_final=g_final)
```

```python
import functools
import math

import numpy as np
import jax
import jax.numpy as jnp
from jax import lax
from jax.experimental import pallas as pl
from jax.experimental.pallas import tpu as pltpu

F32 = jnp.float32
BF16 = jnp.bfloat16

RMS_EPS = 1e-6
LN_EPS = 1e-5
MASK_VALUE = -1e30
POOL_WINDOWS = (2, 4, 8, 16)
FOURIER_GROUPS = 4
ATTN_GROUPS = ((128, 1), (512, 4), (2048, 16))
ATTN_HEADS = 8
ATTN_HEAD_DIM = 64
ATTN_WIDTH = ATTN_HEADS * ATTN_HEAD_DIM
CONV_WIDTH = 31
N_EXPERTS = 8
TOP_K = 2

LANES = 128
SUBLANES = 8
HALO = 16
FFT_N2 = 128
MOE_ROWS = 512
MOE_CHUNK = 512
SEG_ALIGN = 16
SEG_PIECES = (512, 256, 128, 64, 32, 16)
LOCAL_ROWS = 1152
VMEM_LIMIT_BYTES = 56 * 1024 * 1024
EXPERT_VMEM_LIMIT_BYTES = 60 * 1024 * 1024


def _cparams(*sem):
    return pltpu.CompilerParams(dimension_semantics=sem, vmem_limit_bytes=VMEM_LIMIT_BYTES)


def _dot(a, b):
    return jnp.dot(a, b, preferred_element_type=F32)


def _rms(x, g):
    ms = jnp.mean(x * x, axis=-1, keepdims=True)
    return x * lax.rsqrt(ms + RMS_EPS) * g


def _swiglu(hb, w1_ref, w3_ref, w2_ref, g_ref, fc):
    f = w1_ref.shape[-1]
    for c in range(0, f, fc):
        a = _dot(hb, w1_ref[:, c:c + fc])
        b = _dot(hb, w3_ref[:, c:c + fc])
        g_ref[:, c:c + fc] = (a * jax.nn.sigmoid(a) * b).astype(BF16)
    return _dot(g_ref[...], w2_ref[...])


def _route(hb, wrt_ref, gate_ref):
    logits = lax.dot_general(wrt_ref[...], hb, (((1,), (1,)), ((), ())), preferred_element_type=F32)
    e_iota = lax.broadcasted_iota(jnp.int32, logits.shape, 0)
    m1 = jnp.max(logits, axis=0, keepdims=True)
    i1 = jnp.min(jnp.where(logits == m1, e_iota, N_EXPERTS), axis=0, keepdims=True)
    rest = jnp.where(e_iota == i1, -jnp.inf, logits)
    m2 = jnp.max(rest, axis=0, keepdims=True)
    i2 = jnp.min(jnp.where(rest == m2, e_iota, N_EXPERTS), axis=0, keepdims=True)
    e2 = jnp.exp(m2 - m1)
    den = 1.0 + e2
    gate_ref[0:1, :] = 1.0 / den
    gate_ref[1:2, :] = e2 / den
    return [i1, i2]


def _local_rows(idx_rows, before_ref, lrow_ref, len_ref):
    e_iota = lax.broadcasted_iota(jnp.int32, (N_EXPERTS, idx_rows[0].shape[1]), 0)
    picks = [jnp.where(row == e_iota, 1.0, 0.0) for row in idx_rows]
    total = picks[0]
    for m in picks[1:]:
        total = total + m
    earlier = _dot(total.astype(BF16), before_ref[...])
    cnt = jnp.sum(total, axis=1, keepdims=True).astype(jnp.int32)
    seg = (cnt + (SEG_ALIGN - 1)) & (-SEG_ALIGN)
    e_col = lax.broadcasted_iota(jnp.int32, (N_EXPERTS, 1), 0)
    loff = jnp.zeros((N_EXPERTS, 1), jnp.int32)
    for e in range(N_EXPERTS - 1):
        loff = loff + jnp.where(e_col > e, seg[e:e + 1, :], 0)
    base = loff.astype(F32) + earlier
    for k in range(TOP_K):
        lrow_ref[0, k:k + 1, :] = jnp.sum(picks[k] * base, axis=0, keepdims=True).astype(jnp.int32)
        base = base + picks[k]
    len_ref[0] = jnp.broadcast_to(seg, (N_EXPERTS, LANES))


def _moe_prologue(x1, gf_ref, wrt_ref, before_ref, x_out_ref, h_ref, gate_ref, lrow_ref, len_ref):
    x_out_ref[0] = x1
    hb = _rms(x1, gf_ref[...]).astype(BF16)
    h_ref[0] = hb
    _local_rows(_route(hb, wrt_ref, gate_ref), before_ref, lrow_ref, len_ref)


def _moe_prologue_specs(tm, nt, x_shape):
    assert tm == MOE_CHUNK, "the mixer tile is the MoE token chunk"
    bsz, seq, d = x_shape
    n_tok = bsz * seq
    chunk_of = lambda b, i: b * nt + i
    in_specs = [_full((1, d)), _full((N_EXPERTS, d)), _full((tm, tm))]
    out_specs = [pl.BlockSpec((1, tm, d), lambda b, i: (b, i, 0)),
                 pl.BlockSpec((1, tm, d), lambda b, i: (b, i, 0)),
                 pl.BlockSpec((TOP_K, tm), lambda b, i: (0, chunk_of(b, i))),
                 pl.BlockSpec((1, TOP_K, tm), lambda b, i: (chunk_of(b, i), 0, 0)),
                 pl.BlockSpec((1, N_EXPERTS, LANES), lambda b, i: (chunk_of(b, i), 0, 0))]
    out_shape = [jax.ShapeDtypeStruct(x_shape, F32), jax.ShapeDtypeStruct(x_shape, BF16),
                 jax.ShapeDtypeStruct((TOP_K, n_tok), F32),
                 jax.ShapeDtypeStruct((n_tok // tm, TOP_K, tm), jnp.int32),
                 jax.ShapeDtypeStruct((n_tok // tm, N_EXPERTS, LANES), jnp.int32)]
    return in_specs, out_specs, out_shape


def _moe_prologue_args(g_ffn, w_router, tm):
    t_iota = jnp.arange(tm, dtype=jnp.int32)
    before = (t_iota[:, None] < t_iota[None, :]).astype(BF16)
    return g_ffn.reshape(1, -1), w_router.T.astype(BF16), before


def _pool_ffn_kernel(xp_ref, xc_ref, xn_ref, gm_ref, wg_ref, sc_ref, gf_ref, w1_ref, w3_ref, w2_ref,
                     c1_ref, c3_ref, c2_ref, o_ref, b1_ref, b3_ref, b2_ref, hid_ref, *, ts, seq, fc):
    _copy_cast((c1_ref, c3_ref, c2_ref, b1_ref, b3_ref, b2_ref))
    i = pl.program_id(1)
    xc = xc_ref[0]
    d = xc.shape[1]
    gd = d // len(POOL_WINDOWS)
    n_ext = ts + 2 * HALO
    xe = jnp.concatenate([xp_ref[0], xc, xn_ref[0]], axis=0)
    he = _rms(xe, gm_ref[...])
    pos = i * ts - HALO + lax.broadcasted_iota(jnp.int32, (n_ext, 1), 0)
    he = jnp.where((pos >= 0) & (pos < seq), he, 0.0)

    s = he + pltpu.roll(he, 1, 0)
    sums = [s]
    half = 1
    for _ in POOL_WINDOWS[1:]:
        s = s[:, gd:]
        s = pltpu.roll(s, half, 0) + pltpu.roll(s, n_ext - half, 0)
        sums.append(s)
        half *= 2

    tpos = i * ts + lax.broadcasted_iota(jnp.int32, (ts, 1), 0)
    ys = []
    for gi, w in enumerate(POOL_WINDOWS):
        cnt = jnp.minimum(tpos + w // 2, seq) - jnp.maximum(tpos - w // 2, 0)
        mean = sums[gi][HALO:HALO + ts, :gd] / cnt.astype(F32)
        pooled = mean - he[HALO:HALO + ts, gi * gd:(gi + 1) * gd]
        ys.append(_dot(pooled.astype(BF16), wg_ref[gi]))
    x1 = xc + jnp.concatenate(ys, axis=1) * sc_ref[...]
    hb = _rms(x1, gf_ref[...]).astype(BF16)
    o_ref[0] = x1 + _swiglu(hb, w1_ref.at[0], w3_ref.at[0], w2_ref.at[0], hid_ref, fc)


def _halo_specs(ts, seq, d):
    r = ts // HALO
    last = seq // HALO - 1
    return [
        pl.BlockSpec((1, HALO, d), lambda b, i: (b, jnp.maximum(i * r - 1, 0), 0)),
        pl.BlockSpec((1, ts, d), lambda b, i: (b, i, 0)),
        pl.BlockSpec((1, HALO, d), lambda b, i: (b, jnp.minimum((i + 1) * r, last), 0)),
    ]


def _full(shape):
    n = len(shape)
    return pl.BlockSpec(shape, lambda *_: (0,) * n)


def _stacked(shape, layer):
    n = len(shape)
    return pl.BlockSpec((1,) + tuple(shape), lambda *_: (layer,) + (0,) * n, pipeline_mode=pl.Buffered(1))


def _side_cast(stacks, layer, bsz, nt):
    n_steps = bsz * nt
    views, ins, outs, shapes = [], [], [], []
    for w in stacks:
        layers, e, r, c = w.shape
        blk = e * r // n_steps
        assert blk * n_steps == e * r and blk % SEG_ALIGN == 0
        views.append(w.reshape(layers * e * r, c))
        ins.append(pl.BlockSpec((blk, c), lambda b, i: (layer * n_steps + b * nt + i, 0)))
        outs.append(pl.BlockSpec((blk, c), lambda b, i: (b * nt + i, 0)))
        shapes.append(jax.ShapeDtypeStruct((e * r, c), BF16))
    return views, ins, outs, shapes


def _side_cast_results(casted, stacks):
    return [c.reshape((1,) + w.shape[1:]) for c, w in zip(casted, stacks)]


def _copy_cast(refs):
    n = len(refs) // 2
    for src, dst in zip(refs[:n], refs[n:]):
        dst[...] = src[...].astype(BF16)


def _pool_ffn_layer(x, g_mix, w_grp, scale, g_ffn, w1, w3, w2, layer, ts, cast_stacks, cast_layer):
    bsz, seq, d = x.shape
    f = w1.shape[2]
    nt = seq // ts
    cviews, cin, cout, cshape = _side_cast(cast_stacks, cast_layer, bsz, nt)
    kern = functools.partial(_pool_ffn_kernel, ts=ts, seq=seq, fc=256)
    out = pl.pallas_call(
        kern,
        grid=(bsz, nt),
        in_specs=_halo_specs(ts, seq, d) + [
            _full((1, d)), _full(w_grp.shape), _full((1, d)), _full((1, d)),
            _stacked((d, f), layer), _stacked((d, f), layer), _stacked((f, d), layer)] + cin,
        out_specs=[pl.BlockSpec((1, ts, d), lambda b, i: (b, i, 0))] + cout,
        out_shape=[jax.ShapeDtypeStruct(x.shape, F32)] + cshape,
        scratch_shapes=[pltpu.VMEM((ts, f), BF16)],
        compiler_params=_cparams("parallel", "parallel"),
        name="pool_ffn",
    )(x, x, x, g_mix.reshape(1, d), w_grp.astype(BF16), scale.reshape(1, d), g_ffn.reshape(1, d), w1, w3, w2,
      *cviews)
    return out[0], _side_cast_results(out[1:], cast_stacks)


def _fft_stage1_kernel(x_ref, g_ref, f1_ref, o_ref, s_ref, *, nseg):
    n1, d = x_ref.shape[1], x_ref.shape[3]
    nlc = d // LANES
    for c in range(nlc):
        s_ref[c] = x_ref[0, :, :, c * LANES:(c + 1) * LANES].reshape(n1 * nseg, LANES)
    for j in range(nseg):
        xj = jnp.concatenate([s_ref[c, pl.ds(j, n1, stride=nseg), :] for c in range(nlc)], axis=1)
        h = _rms(xj, g_ref[...]).astype(BF16)
        o_ref[0, j] = _dot(f1_ref[...], h).astype(BF16)


def _fft_stage2_kernel(br_ref, bi_ref, m_ref, o_ref, s_ref, *, nk):
    n2, d = FFT_N2, br_ref.shape[3]
    nlc = d // LANES
    for p, ref in enumerate((br_ref, bi_ref)):
        for c in range(nlc):
            s_ref[p, c] = ref[0, :, :, c * LANES:(c + 1) * LANES].reshape(n2 * nk, LANES).astype(F32)
    for j in range(nk):
        rhs = jnp.concatenate(
            [jnp.concatenate([s_ref[p, c, pl.ds(j, n2, stride=nk), :] for c in range(nlc)], axis=1)
             for p in range(2)], axis=0).astype(BF16)
        res = _dot(m_ref[j], rhs)
        for p in range(2):
            for c in range(nlc):
                s_ref[p, c, pl.ds(j, n2, stride=nk), :] = res[p * n2:(p + 1) * n2, c * LANES:(c + 1) * LANES]
    for p in range(2):
        for c in range(nlc):
            o_ref[0, p, :, :, c * LANES:(c + 1) * LANES] = s_ref[p, c].astype(BF16).reshape(n2, nk, LANES)


def _fourier_out_kernel(x_ref, ar_ref, ai_ref, cc_ref, sc_ref, wo_ref, gf_ref, wrt_ref, before_ref,
                        xo_ref, h_ref, gate_ref, lrow_ref, len_ref, *, norm):
    d = x_ref.shape[2]
    gd = d // FOURIER_GROUPS
    fs = []
    for gi in range(FOURIER_GROUPS):
        sl = slice(gi * gd, (gi + 1) * gd)
        fs.append(_dot(ar_ref[0, 0, :, sl].astype(BF16), cc_ref[...])
                  + _dot(ai_ref[0, 0, :, sl].astype(BF16), sc_ref[...]))
    f = (jnp.concatenate(fs, axis=1) * norm).astype(BF16)
    x1 = x_ref[0] + _dot(f, wo_ref[...])
    _moe_prologue(x1, gf_ref, wrt_ref, before_ref, xo_ref, h_ref, gate_ref, lrow_ref, len_ref)


def _dft_tables(seq, gd):
    n2 = FFT_N2
    n1 = seq // n2
    k = np.arange(n1)
    ang1 = 2.0 * np.pi * ((k[:, None] * k[None, :]) % n1) / n1
    f1 = np.concatenate([np.cos(ang1), -np.sin(ang1)], axis=0)
    c = np.arange(gd)
    angc = 2.0 * np.pi * ((c[:, None] * c[None, :]) % gd) / gd
    return (jnp.asarray(f1, BF16), jnp.asarray(np.cos(angc), BF16), jnp.asarray(np.sin(angc), BF16))


def _twiddled_dft(seq):
    n2 = FFT_N2
    n1 = seq // n2
    a = np.arange(n2)
    base = 2.0 * np.pi * ((a[:, None] * a[None, :]) % n2) / n2
    twid = 2.0 * np.pi * (np.arange(n1)[:, None] * a[None, :]) / seq
    c, s = (jnp.asarray(f(base), F32)[None] for f in (np.cos, np.sin))
    tc, ts = (jnp.asarray(f(twid), F32)[:, None, :] for f in (np.cos, np.sin))
    mr = c * tc - s * ts
    mi = -(s * tc + c * ts)
    top = jnp.concatenate([mr, -mi], axis=2)
    bot = jnp.concatenate([mi, mr], axis=2)
    return jnp.concatenate([top, bot], axis=1).astype(BF16)


def _fourier_layer(x, g_mix, w_out, g_ffn, w_router, tm):
    bsz, seq, d = x.shape
    n2 = FFT_N2
    n1 = seq // n2
    gd = d // FOURIER_GROUPS
    f1, cc, sc = _dft_tables(seq, gd)
    mtab = _twiddled_dft(seq)
    nseg = 8
    b1 = pl.pallas_call(
        functools.partial(_fft_stage1_kernel, nseg=nseg),
        grid=(bsz, n2 // nseg),
        in_specs=[pl.BlockSpec((1, n1, nseg, d), lambda b, t: (b, 0, t, 0)), _full((1, d)), _full(f1.shape)],
        out_specs=pl.BlockSpec((1, nseg, 2 * n1, d), lambda b, t: (b, t, 0, 0)),
        out_shape=jax.ShapeDtypeStruct((bsz, n2, 2 * n1, d), BF16),
        scratch_shapes=[pltpu.VMEM((d // LANES, n1 * nseg, LANES), F32)],
        compiler_params=_cparams("parallel", "parallel"),
        name="fft_stage1",
    )(x.reshape(bsz, n1, n2, d), g_mix.reshape(1, d), f1)

    nk = 16
    a = pl.pallas_call(
        functools.partial(_fft_stage2_kernel, nk=nk),
        grid=(bsz, n1 // nk),
        in_specs=[pl.BlockSpec((1, n2, nk, d), lambda b, t: (b, 0, t, 0)),
                  pl.BlockSpec((1, n2, nk, d), lambda b, t: (b, 0, n1 // nk + t, 0)),
                  pl.BlockSpec((nk, 2 * n2, 2 * n2), lambda b, t: (t, 0, 0))],
        out_specs=pl.BlockSpec((1, 2, n2, nk, d), lambda b, t: (b, 0, 0, t, 0)),
        out_shape=jax.ShapeDtypeStruct((bsz, 2, n2, n1, d), BF16),
        scratch_shapes=[pltpu.VMEM((2, d // LANES, n2 * nk, LANES), F32)],
        compiler_params=_cparams("parallel", "parallel"),
        name="fft_stage2",
    )(b1, b1, mtab)

    av = a.reshape(bsz, 2, seq, d)
    nt = seq // tm
    moe_in, moe_out, moe_shape = _moe_prologue_specs(tm, nt, x.shape)
    return pl.pallas_call(
        functools.partial(_fourier_out_kernel, norm=1.0 / math.sqrt(seq * gd)),
        grid=(bsz, nt),
        in_specs=[pl.BlockSpec((1, tm, d), lambda b, i: (b, i, 0)),
                  pl.BlockSpec((1, 1, tm, d), lambda b, i: (b, 0, i, 0)),
                  pl.BlockSpec((1, 1, tm, d), lambda b, i: (b, 1, i, 0)),
                  _full((gd, gd)), _full((gd, gd)), _full((d, d))] + moe_in,
        out_specs=moe_out,
        out_shape=moe_shape,
        compiler_params=_cparams("parallel", "parallel"),
        name="fourier_out",
    )(x, av, av, cc, sc, w_out.astype(BF16), *_moe_prologue_args(g_ffn, w_router, tm))


def _moe_plan(lrow_lanes, seg_len, n_tok):
    chunk, rows = MOE_CHUNK, MOE_ROWS
    nc = n_tok // chunk
    seg_len = seg_len[:, :, 0]
    loff = jnp.cumsum(seg_len, axis=1) - seg_len
    region = (jnp.sum(seg_len, axis=0) + rows - 1) // rows * rows
    pend = jnp.cumsum(region)
    goff = (pend - region)[None, :] + jnp.cumsum(seg_len, axis=0) - seg_len
    n_blocks = (n_tok * TOP_K + nc * N_EXPERTS * (SEG_ALIGN - 1)) // rows + N_EXPERTS
    blk = jnp.arange(n_blocks, dtype=jnp.int32)
    block_exp = jnp.minimum(jnp.sum((blk[:, None] * rows >= pend[None, :]).astype(jnp.int32), axis=1),
                            N_EXPERTS - 1)
    gap_off = (pend - region + jnp.sum(seg_len, axis=0)).astype(jnp.int32)
    return dict(
        n_blocks=n_blocks,
        seg_len=seg_len.reshape(-1), loff=loff.reshape(-1).astype(jnp.int32), goff=goff.reshape(-1).astype(jnp.int32),
        gap_off=gap_off,
        gap_len=(region - jnp.sum(seg_len, axis=0)).astype(jnp.int32),
        block_rows=jnp.clip(gap_off[block_exp] - blk * rows, 0, rows).astype(jnp.int32),
        lrow_lanes=lrow_lanes,
        lrow_cols=lrow_lanes.transpose(0, 2, 1).reshape(n_tok, TOP_K),
        n_used=(pend[-1] // rows).reshape(1).astype(jnp.int32), block_exp=block_exp)


def _segment_copies(c, len_ref, loff_ref, goff_ref, make, act):
    for e in range(N_EXPERTS):
        ln = len_ref[c * N_EXPERTS + e]
        lo = loff_ref[c * N_EXPERTS + e]
        go = goff_ref[c * N_EXPERTS + e]
        for size in SEG_PIECES:
            done = ln & (-2 * size)

            @pl.when((ln & size) != 0)
            def _():
                act(make(pl.multiple_of(lo + done, SEG_ALIGN), pl.multiple_of(go + done, SEG_ALIGN), size))


def _zero_fill_copies(gap_off_ref, gap_len_ref, nu_ref, zero_ref, xb_ref, sem, act):
    rows = zero_ref.shape[0]
    for e in range(N_EXPERTS):
        ln = gap_len_ref[e]
        off = gap_off_ref[e]
        for size in SEG_PIECES[1:]:
            done = ln & (-2 * size)

            @pl.when((ln & size) != 0)
            def _():
                act(pltpu.make_async_copy(zero_ref.at[pl.ds(0, size)],
                                          xb_ref.at[pl.ds(pl.multiple_of(off + done, SEG_ALIGN), size)], sem))

    def block(b, carry):
        act(pltpu.make_async_copy(zero_ref, xb_ref.at[pl.ds(pl.multiple_of(b * rows, rows), rows)], sem))
        return carry

    lax.fori_loop(nu_ref[0], xb_ref.shape[0] // rows, block, 0)


def _dispatch_kernel(len_ref, loff_ref, goff_ref, gap_off_ref, gap_len_ref, nu_ref, h_ref, lrow_ref, xb_ref,
                     stage_ref, zero_ref, sem, zsem):
    c = pl.program_id(0)
    slot = c % 2
    n_local, chunk = stage_ref.shape[1], h_ref.shape[0]

    @pl.when(c == 0)
    def _():
        zero_ref[...] = jnp.zeros_like(zero_ref)
        _zero_fill_copies(gap_off_ref, gap_len_ref, nu_ref, zero_ref, xb_ref, zsem, lambda cp: cp.start())

    def make(slot_):
        def _make(lr, gr, size):
            return pltpu.make_async_copy(stage_ref.at[slot_, pl.ds(lr, size)], xb_ref.at[pl.ds(gr, size)],
                                         sem.at[slot_])
        return _make

    r_iota = lax.broadcasted_iota(jnp.int32, (n_local, chunk), 0)
    sel = (lrow_ref[0, 0:1, :] == r_iota) | (lrow_ref[0, 1:2, :] == r_iota)
    stage_ref[slot] = _dot(jnp.where(sel, 1.0, 0.0).astype(BF16), h_ref[...]).astype(BF16)

    @pl.when(c > 0)
    def _():
        _segment_copies(c - 1, len_ref, loff_ref, goff_ref, make(1 - slot), lambda cp: cp.wait())

    _segment_copies(c, len_ref, loff_ref, goff_ref, make(slot), lambda cp: cp.start())

    @pl.when(c == pl.num_programs(0) - 1)
    def _():
        _segment_copies(c, len_ref, loff_ref, goff_ref, make(slot), lambda cp: cp.wait())
        _zero_fill_copies(gap_off_ref, gap_len_ref, nu_ref, zero_ref, xb_ref, zsem, lambda cp: cp.wait())


def _expert_kernel(be_ref, nu_ref, br_ref, xb_ref, w1_ref, w3_ref, w2_ref, y_ref, hid_ref, *, fc):
    b = pl.program_id(0)
    half = y_ref.shape[0] // 2
    used = b < nu_ref[0]
    short = br_ref[b] <= half
    w = (w1_ref.at[0, 0], w3_ref.at[0, 0], w2_ref.at[0, 0])

    @pl.when(used & jnp.logical_not(short))
    def _():
        y_ref[...] = _swiglu(xb_ref[...], *w, hid_ref, fc).astype(BF16)

    @pl.when(used & short)
    def _():
        y_ref[:half] = _swiglu(xb_ref[:half], *w, hid_ref.at[pl.ds(0, half)], fc).astype(BF16)
        y_ref[half:] = jnp.zeros((half, y_ref.shape[1]), BF16)

    @pl.when(jnp.logical_not(used))
    def _():
        y_ref[...] = jnp.zeros_like(y_ref)


def _combine_kernel(len_ref, loff_ref, goff_ref, y_ref, lcol_ref, gcol_ref, x_ref, g_ref, o_ref, ybuf_ref, sem,
                    *, final_norm):
    c = pl.program_id(0)
    nc = pl.num_programs(0)
    slot = c % 2
    chunk, n_local = x_ref.shape[0], ybuf_ref.shape[1]

    def make(slot_):
        def _make(lr, gr, size):
            return pltpu.make_async_copy(y_ref.at[pl.ds(gr, size)], ybuf_ref.at[slot_, pl.ds(lr, size)],
                                         sem.at[slot_])
        return _make

    @pl.when(c == 0)
    def _():
        ybuf_ref[...] = jnp.zeros_like(ybuf_ref)
        _segment_copies(c, len_ref, loff_ref, goff_ref, make(slot), lambda cp: cp.start())

    @pl.when(c + 1 < nc)
    def _():
        _segment_copies(c + 1, len_ref, loff_ref, goff_ref, make(1 - slot), lambda cp: cp.start())

    _segment_copies(c, len_ref, loff_ref, goff_ref, make(slot), lambda cp: cp.wait())

    yl = ybuf_ref[slot]
    r_iota = lax.broadcasted_iota(jnp.int32, (chunk, n_local), 1)
    q = None
    for k in range(TOP_K):
        qk = jnp.where(lcol_ref[:, k:k + 1] == r_iota, gcol_ref[:, k:k + 1], 0.0)
        q = qk if q is None else q + qk
    xo = x_ref[...] + _dot(q.astype(BF16), yl)
    o_ref[...] = _rms(xo, g_ref[...]) if final_norm else xo


def _moe_layer(routed, w1, w3, w2, layer, g_final=None):
    x, h, gates, lrow_lanes, seg_len = routed
    bsz, seq, d = x.shape
    n_tok = bsz * seq
    f = w1.shape[3]
    rows, chunk = MOE_ROWS, MOE_CHUNK
    nc = n_tok // chunk
    plan = _moe_plan(lrow_lanes, seg_len, n_tok)
    n_blocks = plan['n_blocks']
    n_rows = n_blocks * rows
    seg = (plan['seg_len'], plan['loff'], plan['goff'])

    xb = pl.pallas_call(
        _dispatch_kernel,
        grid_spec=pltpu.PrefetchScalarGridSpec(
            num_scalar_prefetch=6, grid=(nc,),
            in_specs=[pl.BlockSpec((chunk, d), lambda c, *_: (c, 0)),
                      pl.BlockSpec((1, TOP_K, chunk), lambda c, *_: (c, 0, 0))],
            out_specs=pl.BlockSpec(memory_space=pl.ANY),
            scratch_shapes=[pltpu.VMEM((2, LOCAL_ROWS, d), BF16), pltpu.VMEM((rows, d), BF16),
                            pltpu.SemaphoreType.DMA((2,)), pltpu.SemaphoreType.DMA(())]),
        out_shape=jax.ShapeDtypeStruct((n_rows, d), BF16),
        compiler_params=_cparams("arbitrary"),
        name="moe_dispatch",
    )(*seg, plan['gap_off'], plan['gap_len'], plan['n_used'], h.reshape(n_tok, d), plan['lrow_lanes'])

    def blk_idx(b, nu):
        return jnp.minimum(b, nu[0] - 1)

    y = pl.pallas_call(
        functools.partial(_expert_kernel, fc=256),
        grid_spec=pltpu.PrefetchScalarGridSpec(
            num_scalar_prefetch=3, grid=(n_blocks,),
            in_specs=[pl.BlockSpec((rows, d), lambda b, be, nu, br: (blk_idx(b, nu), 0)),
                      pl.BlockSpec((1, 1, d, f), lambda b, be, nu, br: (layer, be[blk_idx(b, nu)], 0, 0)),
                      pl.BlockSpec((1, 1, d, f), lambda b, be, nu, br: (layer, be[blk_idx(b, nu)], 0, 0)),
                      pl.BlockSpec((1, 1, f, d), lambda b, be, nu, br: (layer, be[blk_idx(b, nu)], 0, 0))],
            out_specs=pl.BlockSpec((rows, d), lambda b, be, nu, br: (b, 0)),
            scratch_shapes=[pltpu.VMEM((rows, f), BF16)]),
        out_shape=jax.ShapeDtypeStruct((n_rows, d), BF16),
        compiler_params=pltpu.CompilerParams(dimension_semantics=("arbitrary",),
                                             vmem_limit_bytes=EXPERT_VMEM_LIMIT_BYTES),
        name="moe_experts",
    )(plan['block_exp'], plan['n_used'], plan['block_rows'], xb, w1, w3, w2)

    final_norm = g_final is not None
    g = (g_final if final_norm else jnp.ones((d,), F32)).reshape(1, d)
    out = pl.pallas_call(
        functools.partial(_combine_kernel, final_norm=final_norm),
        grid_spec=pltpu.PrefetchScalarGridSpec(
            num_scalar_prefetch=3, grid=(nc,),
            in_specs=[pl.BlockSpec(memory_space=pl.ANY),
                      pl.BlockSpec((chunk, TOP_K), lambda c, *_: (c, 0)),
                      pl.BlockSpec((chunk, TOP_K), lambda c, *_: (c, 0)),
                      pl.BlockSpec((chunk, d), lambda c, *_: (c, 0)),
                      pl.BlockSpec((1, d), lambda c, *_: (0, 0))],
            out_specs=pl.BlockSpec((chunk, d), lambda c, *_: (c, 0)),
            scratch_shapes=[pltpu.VMEM((2, LOCAL_ROWS, d), BF16), pltpu.SemaphoreType.DMA((2,))]),
        out_shape=jax.ShapeDtypeStruct((n_tok, d), F32),
        compiler_params=_cparams("arbitrary"),
        name="moe_combine",
    )(*seg, y, plan['lrow_cols'], gates.T, x.reshape(n_tok, d), g)
    return out.reshape(bsz, seq, d)


def _qkv_kernel(x_ref, g_ref, w_ref, c_ref, o0_ref, o1_ref, o2_ref, b_ref, h_ref, *, nc):
    _copy_cast((c_ref, b_ref))
    o_refs = (o0_ref, o1_ref, o2_ref)
    tm = x_ref.shape[1]
    gw = 3 * ATTN_WIDTH
    h = _rms(x_ref[0], g_ref[...])
    nlc = h.shape[1] // LANES
    for c in range(nlc):
        h_ref[c] = h[:, c * LANES:(c + 1) * LANES]
    for gi, (_, dil) in enumerate(ATTN_GROUPS):
        n = tm // dil
        hb = h if dil == 1 else jnp.concatenate(
            [jnp.concatenate([h_ref[c, pl.ds(r, n, stride=dil), :] for c in range(nlc)], axis=1)
             for r in range(dil)], axis=0)
        hb = hb.astype(BF16)
        for c in range(gi * gw, (gi + 1) * gw, nc):
            res = _dot(hb, w_ref[:, c:c + nc]).astype(BF16)
            for r in range(dil):
                o_refs[gi][0, :, r * gw + c - gi * gw:r * gw + c - gi * gw + nc] = res[r * n:(r + 1) * n]


def _attn_kernel(q_ref, kp_ref, kc_ref, kn_ref, vp_ref, vc_ref, vn_ref, o_ref, lse_ref,
                 *, tq, seg, dil, radius, slopes):
    j = pl.program_id(2)
    hd = ATTN_HEAD_DIM
    tb = q_ref.shape[1]
    tk = tq + 2 * radius
    q_all = q_ref[0] * jnp.asarray(1.0 / math.sqrt(hd), BF16)
    k_all = jnp.concatenate([kp_ref[0], kc_ref[0], kn_ref[0]], axis=0)
    v_all = jnp.concatenate([vp_ref[0], vc_ref[0], vn_ref[0]], axis=0)
    qi = lax.broadcasted_iota(jnp.int32, (tq, tk), 0)
    kj = lax.broadcasted_iota(jnp.int32, (tq, tk), 1)
    arel = jnp.abs(kj - radius - qi)
    band = arel <= radius
    dist = (dil * arel).astype(F32)
    alibi = [-(slope * dist) for slope in slopes]
    low = lax.broadcasted_iota(jnp.int32, (1, 2 * hd), 1) < hd
    for t in range(tb // tq):
        rows = slice(t * tq, (t + 1) * tq)
        kpos = j * tb + t * tq - radius + kj
        valid = band & (kpos >= 0) & (kpos < seg)
        q, k, v = q_all[rows], k_all[t * tq:t * tq + tk], v_all[t * tq:t * tq + tk]
        for hp in range(ATTN_HEADS // 2):
            sl = slice(hp * 2 * hd, (hp + 1) * 2 * hd)
            qp, kp, vp = q[:, sl], k[:, sl], v[:, sl]
            o_pair, l_pair = None, None
            for sub in range(2):
                mine = low if sub == 0 else jnp.logical_not(low)
                qm = jnp.where(mine, qp, jnp.zeros_like(qp))
                s = lax.dot_general(qm, kp, (((1,), (1,)), ((), ())), preferred_element_type=F32)
                s = jnp.where(valid, s + alibi[2 * hp + sub], MASK_VALUE)
                m = jnp.max(s, axis=-1, keepdims=True)
                p = jnp.exp(s - m)
                den = jnp.sum(p, axis=-1, keepdims=True)
                o = _dot(p.astype(BF16), vp) / den
                l = jnp.broadcast_to(m + jnp.log(den), o.shape)
                o_pair = o if sub == 0 else jnp.where(low, o_pair, o)
                l_pair = l if sub == 0 else jnp.where(low, l_pair, l)
            o_ref[0, rows, sl] = o_pair.astype(BF16)
            lse_ref[0, rows, sl] = l_pair


def _attn_out_ffn_kernel(x_ref, o0_ref, o1_ref, o2_ref, l0_ref, l1_ref, l2_ref, wo_ref, gf_ref,
                         w1_ref, w3_ref, w2_ref, c1_ref, c3_ref, out_ref, b1_ref, b3_ref,
                         so_ref, sl_ref, hid_ref, *, fc):
    _copy_cast((c1_ref, c3_ref, b1_ref, b3_ref))
    aw = ATTN_WIDTH
    ls, os_ = [], []
    for gi, (o_ref, l_ref) in enumerate(((o0_ref, l0_ref), (o1_ref, l1_ref), (o2_ref, l2_ref))):
        dil = ATTN_GROUPS[gi][1]
        if dil == 1:
            os_.append(o_ref[0].astype(F32))
            ls.append(l_ref[0])
            continue
        n = o_ref.shape[1]
        nlc = aw // LANES
        for r in range(dil):
            ov = o_ref[0, :, r * aw:(r + 1) * aw].astype(F32)
            lv = l_ref[0, :, r * aw:(r + 1) * aw]
            for c in range(nlc):
                so_ref[gi - 1, c, pl.ds(r, n, stride=dil), :] = ov[:, c * LANES:(c + 1) * LANES]
                sl_ref[gi - 1, c, pl.ds(r, n, stride=dil), :] = lv[:, c * LANES:(c + 1) * LANES]
        os_.append(jnp.concatenate([so_ref[gi - 1, c] for c in range(nlc)], axis=1))
        ls.append(jnp.concatenate([sl_ref[gi - 1, c] for c in range(nlc)], axis=1))
    m = jnp.maximum(jnp.maximum(ls[0], ls[1]), ls[2])
    es = [jnp.exp(l - m) for l in ls]
    den = es[0] + es[1] + es[2]
    o = (es[0] * os_[0] + es[1] * os_[1] + es[2] * os_[2]) / den
    x1 = x_ref[0] + _dot(o.astype(BF16), wo_ref[...])
    hb = _rms(x1, gf_ref[...]).astype(BF16)
    out_ref[0] = x1 + _swiglu(hb, w1_ref.at[0], w3_ref.at[0], w2_ref.at[0], hid_ref, fc)


def _alibi_slopes():
    n = len(ATTN_GROUPS) * ATTN_HEADS
    s = np.float32(2.0) ** (np.float32(-8.0) * np.arange(1, n + 1, dtype=np.float32) / np.float32(n))
    return [float(v) for v in s]


def _attention_ffn_layer(x, g_mix, w_qkv, w_out, g_ffn, w1, w3, w2, layer, tm, cast_stacks, cast_layer):
    bsz, seq, d = x.shape
    nqkv = w_qkv.shape[1]
    aw = ATTN_WIDTH
    gw = 3 * aw
    assert nqkv == len(ATTN_GROUPS) * gw
    qviews, qin, qout, qshape = _side_cast(cast_stacks[2:], cast_layer, bsz, seq // tm)
    qkvs = pl.pallas_call(
        functools.partial(_qkv_kernel, nc=aw),
        grid=(bsz, seq // tm),
        in_specs=[pl.BlockSpec((1, tm, d), lambda b, i: (b, i, 0)), _full((1, d)), _full((d, nqkv))] + qin,
        out_specs=[pl.BlockSpec((1, tm // dil, dil * gw), lambda b, i: (b, i, 0)) for _, dil in ATTN_GROUPS] + qout,
        out_shape=[jax.ShapeDtypeStruct((bsz, seq // dil, dil * gw), BF16) for _, dil in ATTN_GROUPS] + qshape,
        scratch_shapes=[pltpu.VMEM((d // LANES, tm, LANES), F32)],
        compiler_params=_cparams("parallel", "parallel"),
        name="qkv_proj",
    )(x, g_mix.reshape(1, d), w_qkv.astype(BF16), *qviews)
    cast_w2 = _side_cast_results(qkvs[3:], cast_stacks[2:])
    qkvs = qkvs[:3]

    slopes = _alibi_slopes()
    outs, lses = [], []
    for gi, (window, dil) in enumerate(ATTN_GROUPS):
        radius = window // (2 * dil)
        seg = seq // dil
        tb = min(512, seg)
        tq = min(128, seg)
        hr = tb // radius
        last = seg // radius - 1
        view = qkvs[gi]

        def col(part, r):
            return r * 3 + part

        def cur(part):
            return pl.BlockSpec((1, tb, aw), lambda b, r, j, part=part: (b, j, col(part, r)))

        def prev(part):
            return pl.BlockSpec((1, radius, aw),
                                lambda b, r, j, part=part: (b, jnp.maximum(j * hr - 1, 0), col(part, r)))

        def nxt(part):
            return pl.BlockSpec((1, radius, aw),
                                lambda b, r, j, part=part: (b, jnp.minimum((j + 1) * hr, last), col(part, r)))

        o, lse = pl.pallas_call(
            functools.partial(_attn_kernel, tq=tq, seg=seg, dil=dil, radius=radius,
                              slopes=slopes[gi * ATTN_HEADS:(gi + 1) * ATTN_HEADS]),
            grid=(bsz, dil, seg // tb),
            in_specs=[cur(0), prev(1), cur(1), nxt(1), prev(2), cur(2), nxt(2)],
            out_specs=[pl.BlockSpec((1, tb, aw), lambda b, r, j: (b, j, r)),
                       pl.BlockSpec((1, tb, aw), lambda b, r, j: (b, j, r))],
            out_shape=[jax.ShapeDtypeStruct((bsz, seg, dil * aw), BF16),
                       jax.ShapeDtypeStruct((bsz, seg, dil * aw), F32)],
            compiler_params=_cparams("parallel", "parallel", "parallel"),
            name=f"dilated_attn_{dil}",
        )(view, view, view, view, view, view, view)
        outs.append(o)
        lses.append(lse)

    f = w1.shape[2]
    row = lambda width: pl.BlockSpec((1, tm, width), lambda b, i: (b, i, 0))
    dilated = [pl.BlockSpec((1, tm // dil, dil * aw), lambda b, i: (b, i, 0)) for _, dil in ATTN_GROUPS]
    cviews, cin, cout, cshape = _side_cast(cast_stacks[:2], cast_layer, bsz, seq // tm)
    out = pl.pallas_call(
        functools.partial(_attn_out_ffn_kernel, fc=256),
        grid=(bsz, seq // tm),
        in_specs=[row(d)] + dilated + dilated + [
            _full((aw, d)), _full((1, d)),
            _stacked((d, f), layer), _stacked((d, f), layer), _stacked((f, d), layer)] + cin,
        out_specs=[row(d)] + cout,
        out_shape=[jax.ShapeDtypeStruct(x.shape, F32)] + cshape,
        scratch_shapes=[pltpu.VMEM((len(ATTN_GROUPS) - 1, aw // LANES, tm, LANES), F32),
                        pltpu.VMEM((len(ATTN_GROUPS) - 1, aw // LANES, tm, LANES), F32),
                        pltpu.VMEM((tm, f), BF16)],
        compiler_params=_cparams("parallel", "parallel"),
        name="attn_out_ffn",
    )(x, *outs, *lses, w_out.astype(BF16), g_ffn.reshape(1, d), w1, w3, w2, *cviews)
    return out[0], _side_cast_results(out[1:], cast_stacks[:2]) + cast_w2


def _conv_glu_kernel(x_ref, g_ref, w_ref, u_ref):
    d = x_ref.shape[2]
    hb = _rms(x_ref[0], g_ref[...]).astype(BF16)
    a = _dot(hb, w_ref[:, :d])
    b = _dot(hb, w_ref[:, d:])
    u_ref[0] = a * jax.nn.sigmoid(b)


def _conv_out_kernel(x_ref, up_ref, uc_ref, un_ref, wdw_ref, bdw_ref, lg_ref, lb_ref, w2_ref,
                     gf_ref, wrt_ref, before_ref, xo_ref, h_ref, gate_ref, lrow_ref, len_ref, ext_ref, z_ref,
                     *, ts, seq, rows, lanes):
    i = pl.program_id(1)
    n_ext = ts + 2 * HALO
    ue = jnp.concatenate([up_ref[0], uc_ref[0], un_ref[0]], axis=0)
    pos = i * ts - HALO + lax.broadcasted_iota(jnp.int32, (n_ext, 1), 0)
    ext_ref[0] = jnp.where((pos >= 0) & (pos < seq), ue, 0.0)
    for r in range(1, SUBLANES):
        ext_ref[r, 0:n_ext - SUBLANES, :] = ext_ref[0, r:r + n_ext - SUBLANES, :]
    half = CONV_WIDTH // 2

    def row_chunk(c, carry):
        base = pl.multiple_of(c * rows, rows)
        for lo in range(0, z_ref.shape[1], lanes):
            acc = None
            for k in range(CONV_WIDTH):
                start = HALO + k - half
                r = start % SUBLANES
                first = pl.multiple_of(base + (start - r), SUBLANES)
                term = (ext_ref[r, pl.ds(first, rows), lo:lo + lanes].reshape(rows // SUBLANES, SUBLANES, lanes)
                        * wdw_ref[k, :, lo:lo + lanes])
                acc = term if acc is None else acc + term
            z_ref[pl.ds(base, rows), lo:lo + lanes] = acc.reshape(rows, lanes)
        return carry

    lax.fori_loop(0, ts // rows, row_chunk, 0)
    u = z_ref[...] + bdw_ref[...]
    mu = jnp.mean(u, axis=-1, keepdims=True)
    uc = u - mu
    var = jnp.mean(uc * uc, axis=-1, keepdims=True)
    z = uc * lax.rsqrt(var + LN_EPS) * lg_ref[...] + lb_ref[...]
    z = (z * jax.nn.sigmoid(z)).astype(BF16)
    x1 = x_ref[0] + _dot(z, w2_ref[...])
    _moe_prologue(x1, gf_ref, wrt_ref, before_ref, xo_ref, h_ref, gate_ref, lrow_ref, len_ref)


def _conv_layer(x, g_mix, w_pw1, w_dw, b_dw, ln_g, ln_b, w_pw2, g_ffn, w_router, ts):
    bsz, seq, d = x.shape
    nt = seq // ts
    vec = lambda a: a.reshape(1, d)
    u = pl.pallas_call(
        _conv_glu_kernel,
        grid=(bsz, nt),
        in_specs=[pl.BlockSpec((1, ts, d), lambda b, i: (b, i, 0)), _full((1, d)), _full((d, 2 * d))],
        out_specs=pl.BlockSpec((1, ts, d), lambda b, i: (b, i, 0)),
        out_shape=jax.ShapeDtypeStruct(x.shape, F32),
        compiler_params=_cparams("parallel", "parallel"),
        name="conv_glu",
    )(x, vec(g_mix), w_pw1.astype(BF16))

    moe_in, moe_out, moe_shape = _moe_prologue_specs(ts, nt, x.shape)
    return pl.pallas_call(
        functools.partial(_conv_out_kernel, ts=ts, seq=seq, rows=min(32, ts), lanes=min(2 * LANES, d)),
        grid=(bsz, nt),
        in_specs=[pl.BlockSpec((1, ts, d), lambda b, i: (b, i, 0))] + _halo_specs(ts, seq, d) + [
            _full((CONV_WIDTH, SUBLANES, d)), _full((1, d)), _full((1, d)), _full((1, d)), _full((d, d))] + moe_in,
        out_specs=moe_out,
        out_shape=moe_shape,
        scratch_shapes=[pltpu.VMEM((SUBLANES, ts + 2 * HALO, d), F32), pltpu.VMEM((ts, d), F32)],
        compiler_params=_cparams("parallel", "parallel"),
        name="conv_out",
    )(x, u, u, u, jnp.broadcast_to(w_dw[:, None, :], (CONV_WIDTH, SUBLANES, d)), vec(b_dw), vec(ln_g), vec(ln_b),
      w_pw2.astype(BF16), *_moe_prologue_args(g_ffn, w_router, ts))


def kernel(x, g_mix, g_ffn, g_final, a_w_grp, a_scale, b_w_out, c_w_qkv, c_w_out, d_w_pw1, d_w_dw, d_b_dw,
           d_ln_g, d_ln_b, d_w_pw2, ffn_w1, ffn_w3, ffn_w2, moe_router, moe_w1, moe_w3, moe_w2):
    assert g_mix.shape[0] == 4, "one layer of each mixer kind"
    tm = min(512, x.shape[1])
    ffn_w = [w.astype(BF16) for w in (ffn_w1, ffn_w3, ffn_w2)]
    moe_f32 = (moe_w1, moe_w3, moe_w2)
    x, moe_w = _pool_ffn_layer(x, g_mix[0], a_w_grp[0], a_scale[0], g_ffn[0], *ffn_w, 0, tm, moe_f32, 0)
    routed = _fourier_layer(x, g_mix[1], b_w_out[0], g_ffn[1], moe_router[0], tm)
    x = _moe_layer(routed, *moe_w, 0)
    x, moe_w = _attention_ffn_layer(x, g_mix[2], c_w_qkv[0], c_w_out[0], g_ffn[2], *ffn_w, 1, tm, moe_f32, 1)
    routed = _conv_layer(x, g_mix[3], d_w_pw1[0], d_w_dw[0], d_b_dw[0], d_ln_g[0], d_ln_b[0],
                         d_w_pw2[0], g_ffn[3], moe_router[1], tm)
    return _moe_layer(routed, *moe_w, 0, g_final=g_final)
```
